```python
import math
import jax, jax.numpy as jnp
from jax import lax
import numpy as np

D_MODEL = 1024
BATCH = 16
SEQ = 256
DEPTH = 2
DEC_BATCH = 2
DEC_SEQ = 4096
PAST_LEN = 256

GRID_W = 64
Q_BLOCK = 128
ROPE_BASE = 10000.0
EPS = 1e-6
HEAD_DIM = 64
W_A = D_MODEL // 4
W_B = D_MODEL // 4
W_C = D_MODEL // 4
W_D = D_MODEL - W_A - W_B - W_C
A_HEADS = W_A // HEAD_DIM
A_HALF = HEAD_DIM // 2
B_HEADS = W_B // HEAD_DIM
B_KV_HEADS = B_HEADS // 2
B_KV_DIM = B_KV_HEADS * HEAD_DIM
C_HEADS = W_C // HEAD_DIM
C_GROUPS = 2
C_STATE = 64
C_CONV = 3
C_CHUNK = 128
C_BC_DIM = C_GROUPS * C_STATE
C_CONV_DIM = W_C + 2 * C_BC_DIM
D_GROUP = 16
D_GROUPS = W_D // D_GROUP
D_STATE = 64
FFN_DIM = ((8 * D_MODEL // 3 + 127) // 128) * 128
FFN_CONV = 3
IN_SIZES = (W_A, W_A, W_A, W_B, B_KV_DIM, B_KV_DIM, W_C, W_C, C_BC_DIM, C_BC_DIM, 2 * C_HEADS, W_D)
IN_DIM = sum(IN_SIZES)

kernel_name = 'hybrid_flow_prefix_step'

F32 = jnp.float32


def rmsnorm(x, g):
    xf = x.astype(F32)
    y = xf * lax.rsqrt(jnp.mean(xf * xf, axis=-1, keepdims=True) + EPS)
    return (y * g.astype(F32)).astype(x.dtype)


def grid_angles(L, dim):
    rows = L // GRID_W
    row = jnp.repeat(jnp.arange(rows), GRID_W).astype(F32)
    col = jnp.tile(jnp.arange(GRID_W), rows).astype(F32)
    n = dim // 4
    freqs = ROPE_BASE ** (-jnp.arange(n, dtype=F32) / n)
    return jnp.concatenate([row[:, None] * freqs, col[:, None] * freqs], axis=-1)


def apply_rope(x, ang):
    d = x.shape[-1]
    xf = x.astype(F32)
    cos = jnp.cos(ang)[None, :, None, :]
    sin = jnp.sin(ang)[None, :, None, :]
    x1, x2 = xf[..., : d // 2], xf[..., d // 2:]
    return jnp.concatenate([x1 * cos - x2 * sin, x1 * sin + x2 * cos], axis=-1).astype(x.dtype)


def over_query_blocks(fn, q):
    b, L = q.shape[:2]
    nb = L // Q_BLOCK
    blocks = jnp.moveaxis(q.reshape(b, nb, Q_BLOCK, *q.shape[2:]), 1, 0)
    out = lax.map(fn, blocks)
    return jnp.moveaxis(out, 0, 1).reshape(b, L, *out.shape[3:])


def diff_attention(q, k, v, lam):
    scale = A_HALF ** -0.5
    k1, k2 = k[..., :A_HALF], k[..., A_HALF:]

    def block(qb):
        s1 = jnp.einsum('bqhd,bkhd->bhqk', qb[..., :A_HALF], k1).astype(F32) * scale
        s2 = jnp.einsum('bqhd,bkhd->bhqk', qb[..., A_HALF:], k2).astype(F32) * scale
        p = jax.nn.softmax(s1, axis=-1) - lam * jax.nn.softmax(s2, axis=-1)
        return jnp.einsum('bhqk,bkhd->bqhd', p.astype(v.dtype), v)

    return over_query_blocks(block, q)


def gqa_attention(q, k, v):
    b = q.shape[0]
    hq, d = q.shape[2], q.shape[3]
    hk = k.shape[2]
    g = hq // hk
    scale = d ** -0.5

    def block(qb):
        qg = qb.reshape(b, Q_BLOCK, hk, g, d)
        s = jnp.einsum('bqkgd,bskd->bkgqs', qg, k).astype(F32) * scale
        p = jax.nn.softmax(s, axis=-1)
        o = jnp.einsum('bkgqs,bskd->bqkgd', p.astype(v.dtype), v)
        return o.reshape(b, Q_BLOCK, hq, d)

    return over_query_blocks(block, q)


def dwconv(x, w, bias):
    width, ch = w.shape
    pad = width // 2
    y = lax.conv_general_dilated(x, w[:, None, :].astype(x.dtype), window_strides=(1,),
                                 padding=[(pad, pad)], dimension_numbers=('NWC', 'WIO', 'NWC'),
                                 feature_group_count=ch)
    return y + bias.astype(x.dtype)


def ssd_scan(x, dt, A, Bm, Cm, h0):
    b, L, h, p = x.shape
    n = Bm.shape[-1]
    T = L // C_CHUNK
    xc = (x.astype(F32) * dt[..., None]).reshape(b, T, C_CHUNK, h, p)
    Bc = Bm.astype(F32).reshape(b, T, C_CHUNK, h, n)
    Cc = Cm.astype(F32).reshape(b, T, C_CHUNK, h, n)
    a_cum = jnp.cumsum((dt * A).reshape(b, T, C_CHUNK, h), axis=2)
    seg = a_cum[:, :, :, None, :] - a_cum[:, :, None, :, :]
    mask = jnp.tril(jnp.ones((C_CHUNK, C_CHUNK), bool))[None, None, :, :, None]
    decay = jnp.where(mask, jnp.exp(jnp.where(mask, seg, 0.0)), 0.0)
    gmat = jnp.einsum('btlhn,btshn->btlsh', Cc, Bc) * decay
    y_diag = jnp.einsum('btlsh,btshp->btlhp', gmat, xc)
    decay_to_end = jnp.exp(a_cum[:, :, -1:, :] - a_cum)
    chunk_states = jnp.einsum('btshn,btsh,btshp->bthpn', Bc, decay_to_end, xc)
    chunk_decay = jnp.exp(a_cum[:, :, -1, :])

    def step(hc, inp):
        st, dec = inp
        return hc * dec[..., None, None] + st, hc

    h_final, h_enter = lax.scan(step, h0.astype(F32),
                                (jnp.moveaxis(chunk_states, 1, 0), jnp.moveaxis(chunk_decay, 1, 0)))
    h_enter = jnp.moveaxis(h_enter, 0, 1)
    y_off = jnp.einsum('btlhn,bthpn,btlh->btlhp', Cc, h_enter, jnp.exp(a_cum))
    return (y_diag + y_off).reshape(b, L, h, p), h_final


def ssd_mixer(z, xs, Bm, Cm, dt, conv_w, conv_b, dt_bias, a_log, d_skip, norm_g, h0):
    b, L, _ = xs.shape
    xbc = jax.nn.silu(dwconv(jnp.concatenate([xs, Bm, Cm], axis=-1), conv_w, conv_b))
    xs, Bm, Cm = jnp.split(xbc, [W_C, W_C + C_BC_DIM], axis=-1)
    xh = xs.reshape(b, L, C_HEADS, HEAD_DIM)
    rep = C_HEADS // C_GROUPS
    Bh = jnp.repeat(Bm.reshape(b, L, C_GROUPS, C_STATE), rep, axis=2)
    Ch = jnp.repeat(Cm.reshape(b, L, C_GROUPS, C_STATE), rep, axis=2)
    dtf = jax.nn.softplus(dt.astype(F32).reshape(b, L, 2, C_HEADS) + dt_bias.astype(F32))
    A = -jnp.exp(a_log.astype(F32))
    y_f, h_f = ssd_scan(xh, dtf[:, :, 0], A[0], Bh, Ch, h0[:, 0])
    flip = lambda t: jnp.flip(t, axis=1)
    y_b, h_b = ssd_scan(flip(xh), flip(dtf[:, :, 1]), A[1], flip(Bh), flip(Ch), h0[:, 1])
    y = y_f + flip(y_b) + d_skip.astype(F32)[:, None] * xh.astype(F32)
    y = y.reshape(b, L, W_C).astype(z.dtype)
    return rmsnorm(y * jax.nn.silu(z), norm_g), jnp.stack([h_f, h_b], axis=1)


def s5_combine(e1, e2):
    a1, b1 = e1
    a2, b2 = e2
    return a1 * a2, a2 * b1 + b2


def s5_mixer(u, lam_re, lam_im, log_step, b_ri, c_ri, d_skip, w_glu, h0):
    b, L, _ = u.shape
    uf = u.astype(F32)
    lam = lax.complex(lam_re.astype(F32), lam_im.astype(F32))
    delta = jnp.exp(log_step.astype(F32))[..., None]
    a_bar = jnp.exp(lam * delta)
    bmat = lax.complex(b_ri[..., 0].astype(F32), b_ri[..., 1].astype(F32))
    b_bar = ((a_bar - 1.0) / lam)[..., None] * bmat
    cmat = lax.complex(c_ri[..., 0].astype(F32), c_ri[..., 1].astype(F32))
    h0c = lax.complex(h0[..., 0].astype(F32), h0[..., 1].astype(F32))
    ug = uf.reshape(b, L, D_GROUPS, D_GROUP).astype(jnp.complex64)
    bu = jnp.einsum('dgnc,blgc->dblgn', b_bar, ug)
    bu_f = bu[0].at[:, 0].add(a_bar[0] * h0c[:, 0])
    _, xf = lax.associative_scan(s5_combine, (jnp.broadcast_to(a_bar[0], bu_f.shape), bu_f), axis=1)
    bu_b = bu[1].at[:, -1].add(a_bar[1] * h0c[:, 1])
    _, xb = lax.associative_scan(s5_combine, (jnp.broadcast_to(a_bar[1], bu_b.shape), bu_b), axis=1,
                                 reverse=True)
    y = (jnp.real(jnp.einsum('gcn,blgn->blgc', cmat[0], xf))
         + jnp.real(jnp.einsum('gcn,blgn->blgc', cmat[1], xb)))
    y = y.reshape(b, L, W_D) + d_skip.astype(F32) * uf
    y = jax.nn.gelu(y).astype(u.dtype)
    ga, gb = jnp.split(y @ w_glu, 2, axis=-1)
    state = jnp.stack([xf[:, -1], xb[:, 0]], axis=1)
    return ga * jax.nn.sigmoid(gb), jnp.stack([jnp.real(state), jnp.imag(state)], axis=-1)


def adaln(cond, w_mod, b_mod):
    return (jax.nn.silu(cond) @ w_mod + b_mod).reshape(cond.shape[0], 6, D_MODEL)


def trunk_layer(x, mod, P, layer_idx, ctx):
    latent = ctx is not None
    b, L, _ = x.shape
    shift1, scale1, gate1, shift2, scale2, gate2 = (mod[:, i][:, None, :] for i in range(6))
    h = rmsnorm(x, P['g_pre1']) * (1.0 + scale1) + shift1
    offsets = np.cumsum(IN_SIZES)[:-1].tolist()
    aq, ak, av, bq, bk, bv, cz, cx, cb, cc, cdt, du = jnp.split(h @ P['w_in'], offsets, axis=-1)
    aq = aq.reshape(b, L, A_HEADS, HEAD_DIM)
    ak = ak.reshape(b, L, A_HEADS, HEAD_DIM)
    av = av.reshape(b, L, A_HEADS, HEAD_DIM)
    bq = rmsnorm(bq.reshape(b, L, B_HEADS, HEAD_DIM), P['b_qnorm'])
    bk = rmsnorm(bk.reshape(b, L, B_KV_HEADS, HEAD_DIM), P['b_knorm'])
    bv = bv.reshape(b, L, B_KV_HEADS, HEAD_DIM)
    if latent:
        ctx_ak, ctx_av, ctx_bk, ctx_bv, ssd_h0, s5_h0 = ctx
        ang_a = grid_angles(L, A_HALF)
        ang_b = grid_angles(L, HEAD_DIM)

        def rope_diff(t):
            return jnp.concatenate([apply_rope(t[..., :A_HALF], ang_a), apply_rope(t[..., A_HALF:], ang_a)], axis=-1)

        a_q = rope_diff(aq)
        a_k = jnp.concatenate([rope_diff(ak), ctx_ak.astype(ak.dtype)], axis=1)
        a_v = jnp.concatenate([av, ctx_av.astype(av.dtype)], axis=1)
        b_q = apply_rope(bq, ang_b)
        b_k = jnp.concatenate([apply_rope(bk, ang_b), ctx_bk.astype(bk.dtype)], axis=1)
        b_v = jnp.concatenate([bv, ctx_bv.astype(bv.dtype)], axis=1)
    else:
        a_q, a_k, a_v, b_q, b_k, b_v = aq, ak, av, bq, bk, bv
        ssd_h0 = jnp.zeros((b, 2, C_HEADS, HEAD_DIM, C_STATE), F32)
        s5_h0 = jnp.zeros((b, 2, D_GROUPS, D_STATE, 2), F32)
    lam_init = 0.8 - 0.6 * math.exp(-0.3 * layer_idx)
    lp = P['a_lam'].astype(F32)
    lam = jnp.exp(jnp.sum(lp[0] * lp[1])) - jnp.exp(jnp.sum(lp[2] * lp[3])) + lam_init
    ya = diff_attention(a_q, a_k, a_v, lam)
    ya = (rmsnorm(ya, P['a_subln']) * (1.0 - lam_init)).reshape(b, L, W_A)
    yb = gqa_attention(b_q, b_k, b_v).reshape(b, L, W_B)
    yc, ssd_state = ssd_mixer(cz, cx, cb, cc, cdt, P['c_conv_w'], P['c_conv_b'], P['c_dt_bias'],
                              P['c_a_log'], P['c_d'], P['c_norm'], ssd_h0)
    yd, s5_state = s5_mixer(du, P['d_lam_re'], P['d_lam_im'], P['d_log_step'], P['d_b'], P['d_c'],
                            P['d_d'], P['d_glu'], s5_h0)
    y = jnp.concatenate([ya, yb, yc.astype(ya.dtype), yd.astype(ya.dtype)], axis=-1) @ P['w_out']
    x = x + gate1 * rmsnorm(y, P['g_post1'])
    h = rmsnorm(x, P['g_pre2']) * (1.0 + scale2) + shift2
    g, v = jnp.split(dwconv(h @ P['w_up'], P['ffn_conv_w'], P['ffn_conv_b']), 2, axis=-1)
    x = x + gate2 * rmsnorm((jax.nn.silu(g) * v) @ P['w_down'], P['g_post2'])
    if latent:
        return x, None
    return x, (ak, av, bk, bv, ssd_state, s5_state)


def setup_inputs(seed: int = 0) -> dict:
    key = jax.random.key(seed)
    ks = iter(jax.random.split(key, 64))

    def nrm(shape, scale):
        return jax.random.normal(next(ks), shape, F32) * scale

    def gain(shape):
        return 1.0 + nrm(shape, 0.02)

    def unif(shape, lo, hi):
        return jax.random.uniform(next(ks), shape, F32, lo, hi)

    D, L, F = D_MODEL, DEPTH, FFN_DIM
    dt0 = jnp.exp(unif((L, 2, C_HEADS), math.log(1e-3), math.log(1e-1)))
    n = jnp.arange(D_STATE, dtype=F32)
    return {
        'x_prompt': nrm((BATCH, SEQ, D), 1.0),
        'x_sample': nrm((DEC_BATCH, DEC_SEQ, D), 1.0),
        'cache_a_k': nrm((DEC_BATCH, L, PAST_LEN, A_HEADS, HEAD_DIM), 1.0),
        'cache_a_v': nrm((DEC_BATCH, L, PAST_LEN, A_HEADS, HEAD_DIM), 1.0),
        'cache_b_k': nrm((DEC_BATCH, L, PAST_LEN, B_KV_HEADS, HEAD_DIM), 1.0),
        'cache_b_v': nrm((DEC_BATCH, L, PAST_LEN, B_KV_HEADS, HEAD_DIM), 1.0),
        'state_ssd': nrm((DEC_BATCH, L, 2, C_HEADS, HEAD_DIM, C_STATE), 0.1),
        'state_s5': nrm((DEC_BATCH, L, 2, D_GROUPS, D_STATE, 2), 0.05),
        'c': nrm((DEC_BATCH, D), 1.0),
        'c_ctx': nrm((D,), 1.0),
        'w_mod': nrm((L, D, 6 * D), 0.3 * D ** -0.5),
        'b_mod': nrm((L, 6 * D), 0.02),
        'g_pre1': gain((L, D)),
        'g_post1': gain((L, D)),
        'g_pre2': gain((L, D)),
        'g_post2': gain((L, D)),
        'w_in': nrm((L, D, IN_DIM), D ** -0.5),
        'a_lam': nrm((L, 4, A_HALF), 0.1),
        'a_subln': gain((L, HEAD_DIM)),
        'b_qnorm': gain((L, HEAD_DIM)),
        'b_knorm': gain((L, HEAD_DIM)),
        'c_conv_w': nrm((L, C_CONV, C_CONV_DIM), C_CONV ** -0.5),
        'c_conv_b': nrm((L, C_CONV_DIM), 0.02),
        'c_dt_bias': dt0 + jnp.log(-jnp.expm1(-dt0)),
        'c_a_log': jnp.log(unif((L, 2, C_HEADS), 1.0, 16.0)),
        'c_d': 1.0 + nrm((L, C_HEADS), 0.1),
        'c_norm': gain((L, W_C)),
        'd_lam_re': -0.5 + nrm((L, 2, D_GROUPS, D_STATE), 0.01),
        'd_lam_im': math.pi * n + nrm((L, 2, D_GROUPS, D_STATE), 0.01),
        'd_log_step': unif((L, 2, D_GROUPS), math.log(1e-3), math.log(1e-1)),
        'd_b': nrm((L, 2, D_GROUPS, D_STATE, D_GROUP, 2), (2 * D_GROUP) ** -0.5),
        'd_c': nrm((L, 2, D_GROUPS, D_GROUP, D_STATE, 2), (2 * D_STATE) ** -0.5),
        'd_d': nrm((L, W_D), 0.5),
        'd_glu': nrm((L, W_D, 2 * W_D), W_D ** -0.5),
        'w_out': nrm((L, D, D), D ** -0.5),
        'w_up': nrm((L, D, 2 * F), D ** -0.5),
        'ffn_conv_w': nrm((L, FFN_CONV, 2 * F), FFN_CONV ** -0.5),
        'ffn_conv_b': nrm((L, 2 * F), 0.02),
        'w_down': nrm((L, F, D), F ** -0.5),
    }


def reference(x_prompt, x_sample, cache_a_k, cache_a_v, cache_b_k, cache_b_v, state_ssd, state_s5,
              c, c_ctx, w_mod, b_mod, g_pre1, g_post1, g_pre2, g_post2, w_in, a_lam, a_subln,
              b_qnorm, b_knorm, c_conv_w, c_conv_b, c_dt_bias, c_a_log, c_d, c_norm,
              d_lam_re, d_lam_im, d_log_step, d_b, d_c, d_d, d_glu, w_out, w_up,
              ffn_conv_w, ffn_conv_b, w_down):
    def layer_params(l):
        return {
            'g_pre1': g_pre1[l], 'g_post1': g_post1[l], 'g_pre2': g_pre2[l], 'g_post2': g_post2[l],
            'w_in': w_in[l], 'a_lam': a_lam[l], 'a_subln': a_subln[l],
            'b_qnorm': b_qnorm[l], 'b_knorm': b_knorm[l],
            'c_conv_w': c_conv_w[l], 'c_conv_b': c_conv_b[l], 'c_dt_bias': c_dt_bias[l],
            'c_a_log': c_a_log[l], 'c_d': c_d[l], 'c_norm': c_norm[l],
            'd_lam_re': d_lam_re[l], 'd_lam_im': d_lam_im[l], 'd_log_step': d_log_step[l],
            'd_b': d_b[l], 'd_c': d_c[l], 'd_d': d_d[l], 'd_glu': d_glu[l],
            'w_out': w_out[l], 'w_up': w_up[l], 'ffn_conv_w': ffn_conv_w[l],
            'ffn_conv_b': ffn_conv_b[l], 'w_down': w_down[l],
        }

    y_prompt = x_prompt
    ctx_lists = ([], [], [], [], [], [])
    for l in range(DEPTH):
        mod = adaln(c_ctx[None, :], w_mod[l], b_mod[l])
        y_prompt, new = trunk_layer(y_prompt, mod, layer_params(l), l, None)
        for lst, t in zip(ctx_lists, new):
            lst.append(t)

    y_sample = x_sample
    for l in range(DEPTH):
        mod = adaln(c, w_mod[l], b_mod[l])
        ctx = (cache_a_k[:, l], cache_a_v[:, l], cache_b_k[:, l], cache_b_v[:, l],
               state_ssd[:, l], state_s5[:, l])
        y_sample, _ = trunk_layer(y_sample, mod, layer_params(l), l, ctx)

    new_a_k = jnp.stack(ctx_lists[0], axis=1)
    new_a_v = jnp.stack(ctx_lists[1], axis=1)
    new_b_k = jnp.stack(ctx_lists[2], axis=1)
    new_b_v = jnp.stack(ctx_lists[3], axis=1)
    new_ssd = jnp.stack(ctx_lists[4], axis=1)
    new_s5 = jnp.stack(ctx_lists[5], axis=1)
    return (y_prompt, y_sample, new_a_k, new_a_v, new_b_k, new_b_v, new_ssd, new_s5)
```

```python
import functools
import math

import numpy as np
import jax
import jax.numpy as jnp
from jax import lax
from jax.experimental import pallas as pl
from jax.experimental.pallas import tpu as pltpu

F32 = jnp.float32
BF16 = jnp.bfloat16

EPS = 1e-6
GRID_W = 64
ROPE_BASE = 10000.0
HEAD_DIM = 64
A_HALF = HEAD_DIM // 2
N_HEADS = 4
WG = N_HEADS * HEAD_DIM
C_GROUPS = 2
C_STATE = 64
C_CHUNK = 128
D_GROUP = 16
D_GROUPS = WG // D_GROUP
D_STATE = 64
S5_LANES = D_GROUPS * D_STATE * 2

TM = 256
SUBLANES = 8
VMEM_LIMIT = 56 * 1024 * 1024

_C_AQ, _C_AK, _C_AV, _C_BQ, _C_BK, _C_BV, _C_CZ, _C_CX, _C_DU, _C_DT, _C_END = (
    0, 256, 512, 768, 1024, 1280, 1536, 1792, 2304, 2560, 2688)


def _sigmoid(x):
    return 1.0 / (1.0 + jnp.exp(-x))


def _silu(x):
    return x * _sigmoid(x)


def _cparams(sem):
    return pltpu.CompilerParams(dimension_semantics=sem, vmem_limit_bytes=VMEM_LIMIT)


def _lane_mask(width, lo, hi):
    lane = lax.broadcasted_iota(jnp.int32, (1, width), 1)
    return (lane >= lo) & (lane < hi)


def _mod_kernel(c_ref, w_ref, b_ref, o_ref):
    s = _silu(c_ref[...])
    o_ref[0] = jnp.dot(s.astype(BF16), w_ref[0].astype(BF16), preferred_element_type=F32) + b_ref[0]


def _modulation(cond8, w_mod, b_mod):
    depth, d, n = w_mod.shape
    tn = 1536
    return pl.pallas_call(
        _mod_kernel,
        grid=(depth, n // tn),
        in_specs=[pl.BlockSpec((SUBLANES, d), lambda l, j: (0, 0)),
                  pl.BlockSpec((1, d, tn), lambda l, j: (l, 0, j)),
                  pl.BlockSpec((1, 1, tn), lambda l, j: (l, 0, j))],
        out_specs=pl.BlockSpec((1, SUBLANES, tn), lambda l, j: (l, 0, j)),
        out_shape=jax.ShapeDtypeStruct((depth, SUBLANES, n), F32),
        compiler_params=_cparams(("parallel", "parallel")),
        name="modulation",
    )(cond8, w_mod, b_mod.reshape(depth, 1, n))


def _rms(x, g):
    return (x * lax.rsqrt(jnp.mean(x * x, axis=-1, keepdims=True) + EPS)) * g


def _inproj_kernel(x_ref, mod_ref, g_ref, w_ref, qn_ref, kn_ref, ones_ref, rt_ref, ct_ref,
                   qa_ref, ka_ref, va_ref, qb_ref, kb_ref, vb_ref, cz_ref, cx_ref, du_ref, dt_ref,
                   *, nct):
    i = pl.program_id(0)
    mod = mod_ref[0]
    h = _rms(x_ref[...], g_ref[...]) * (1.0 + mod[1:2]) + mod[0:1]
    p = jnp.dot(h.astype(BF16), w_ref[...], preferred_element_type=F32)
    va_ref[...] = p[:, _C_AV:_C_BQ]
    vb_ref[...] = p[:, _C_BV:_C_CZ]
    cz_ref[...] = p[:, _C_CZ:_C_CX]
    cx_ref[...] = p[:, _C_CX:_C_DU]
    du_ref[...] = p[:, _C_DU:_C_DT]
    dt_ref[...] = p[:, _C_DT:_C_END]

    def headnorm(t, gain):
        sq = t * t
        hi = sq.astype(BF16)
        lo = (sq - hi.astype(F32)).astype(BF16)
        ms = (jnp.dot(hi, ones_ref[...], preferred_element_type=F32)
              + jnp.dot(lo, ones_ref[...], preferred_element_type=F32))
        return (t * lax.rsqrt(ms + EPS)) * gain

    aq = p[:, _C_AQ:_C_AK] * (A_HALF ** -0.5)
    ak = p[:, _C_AK:_C_AV]
    bq = headnorm(p[:, _C_BQ:_C_BK], qn_ref[...]) * (HEAD_DIM ** -0.5)
    bk = headnorm(p[:, _C_BK:_C_BV], kn_ref[...])

    @pl.when(i < nct)
    def _():
        qa_ref[...] = aq
        ka_ref[...] = ak
        qb_ref[...] = bq
        kb_ref[...] = bk

    @pl.when(i >= nct)
    def _():
        rows = TM // GRID_W
        rt = rt_ref[0]
        lane = lax.broadcasted_iota(jnp.int32, (1, WG), 1)
        row_lane_a = (lane % (A_HALF // 2)) < A_HALF // 4
        row_lane_b = (lane % (HEAD_DIM // 2)) < HEAD_DIM // 4

        def table(k, row_lane):
            parts = [jnp.where(row_lane, rt[k, r:r + 1, :], ct_ref[k]) for r in range(rows)]
            return jnp.concatenate(parts, axis=0)

        def rope(t, base, row_lane, dist):
            return (t * table(base, row_lane)
                    + pltpu.roll(t, WG - dist, 1) * table(base + 1, row_lane)
                    + pltpu.roll(t, dist, 1) * table(base + 2, row_lane))

        qa_ref[...] = rope(aq, 0, row_lane_a, A_HALF // 2)
        ka_ref[...] = rope(ak, 0, row_lane_a, A_HALF // 2)
        qb_ref[...] = rope(bq, 3, row_lane_b, HEAD_DIM // 2)
        kb_ref[...] = rope(bk, 3, row_lane_b, HEAD_DIM // 2)


def _rope_tables(l2):
    rows = l2 // GRID_W
    lane = np.arange(WG)

    def one(block, n):
        freqs = (np.float32(ROPE_BASE) ** (-np.arange(n, dtype=np.float32) / np.float32(n))).astype(np.float32)
        p = lane % block
        idx = p % (block // 2)
        first = p < (block // 2)
        is_row = idx < n
        f = freqs[idx % n]

        def tabs(pos):
            ang = (pos[:, None].astype(np.float32) * f[None, :]).astype(np.float32)
            c, s = np.cos(ang).astype(np.float32), np.sin(ang).astype(np.float32)
            return [c, np.where(first[None, :], -s, 0.0).astype(np.float32),
                    np.where(first[None, :], 0.0, s).astype(np.float32)]

        return tabs(np.arange(rows)), tabs(np.arange(GRID_W)), is_row

    ra, ca, _ = one(A_HALF, A_HALF // 4)
    rb, cb, _ = one(HEAD_DIM, HEAD_DIM // 4)
    rt = np.stack(ra + rb, axis=0)
    ct = np.stack(ca + cb, axis=0)
    rpt = TM // GRID_W
    npt = l2 // TM
    rt = rt.reshape(6, npt, rpt, WG).transpose(1, 0, 2, 3)
    rt = np.concatenate([rt, np.zeros((npt, 6, SUBLANES - rpt, WG), np.float32)], axis=2)
    return jnp.asarray(rt), jnp.asarray(ct)


def _in_proj(x_all, mod, g_pre1, w_in_p, qn, kn, ones_blk, rt, ct, *, nct, tpl):
    n, d = x_all.shape
    nt = n // TM
    row = lambda i: (i, 0)
    const2 = lambda i: (0, 0)
    mod_map = lambda i: (jnp.where(i < nct, 0, 1 + jnp.maximum(i - nct, 0) // tpl), 0, 0)
    widths = (WG, WG, WG, WG, WG, WG, WG, 2 * WG, WG, 128)
    return pl.pallas_call(
        functools.partial(_inproj_kernel, nct=nct),
        grid=(nt,),
        in_specs=[pl.BlockSpec((TM, d), row),
                  pl.BlockSpec((1, 6, d), mod_map),
                  pl.BlockSpec((1, d), const2),
                  pl.BlockSpec(w_in_p.shape, const2),
                  pl.BlockSpec((1, WG), const2),
                  pl.BlockSpec((1, WG), const2),
                  pl.BlockSpec((WG, WG), const2),
                  pl.BlockSpec((1, 6, SUBLANES, WG), lambda i: (jnp.maximum(i - nct, 0) % tpl, 0, 0, 0)),
                  pl.BlockSpec(ct.shape, lambda i: (0, 0, 0))],
        out_specs=[pl.BlockSpec((TM, w), row) for w in widths],
        out_shape=[jax.ShapeDtypeStruct((n, w), F32) for w in widths],
        compiler_params=_cparams(("parallel",)),
        name="in_proj",
    )(x_all, mod, g_pre1, w_in_p, qn, kn, ones_blk, rt, ct)


def _attn_kernel(*refs, diff, tq, tk, lk_new, lc, lam_init):
    it = iter(refs)
    q_ref, k_ref, v_ref = next(it), next(it), next(it)
    kc_ref = vc_ref = None
    if lc:
        kc_ref, vc_ref = next(it), next(it)
    lam_ref = sub_ref = None
    if diff:
        lam_ref, sub_ref = next(it), next(it)
    prev_ref = next(it)
    o_ref = next(it)
    kbf, vbf, acc_ref, m_ref, l_ref = next(it), next(it), next(it), next(it), next(it)
    del prev_ref

    @pl.when(pl.program_id(1) == 0)
    def _():
        kbf[0:lk_new, :] = k_ref[...].astype(BF16)
        vbf[0:lk_new, :] = v_ref[...].astype(BF16)
        if lc:
            kbf[lk_new:lk_new + lc, :] = kc_ref[0, 0].astype(BF16)
            vbf[lk_new:lk_new + lc, :] = vc_ref[0, 0].astype(BF16)

    q = q_ref[...]
    zero_q = jnp.zeros_like(q)

    def chunk(qm, kc, vc):
        s = lax.dot_general(qm, kc, (((1,), (1,)), ((), ())), preferred_element_type=F32)
        m_prev = m_ref[...]
        m_new = jnp.maximum(m_prev, jnp.max(s, axis=-1, keepdims=True))
        alpha = jnp.exp(m_prev - m_new)
        p = jnp.exp(s - m_new[:, 0:1])
        l_ref[...] = alpha * l_ref[...] + jnp.sum(p, axis=-1, keepdims=True)
        acc_ref[...] = (acc_ref[...] * jnp.concatenate([alpha, alpha], axis=1)
                        + jnp.dot(p.astype(BF16), vc, preferred_element_type=F32))
        m_ref[...] = m_new

    def stream(lo, hi):
        qm = jnp.where(_lane_mask(WG, lo, hi), q, zero_q).astype(BF16)
        m_ref[...] = jnp.full(m_ref.shape, -1e30, F32)
        l_ref[...] = jnp.zeros(l_ref.shape, F32)
        acc_ref[...] = jnp.zeros(acc_ref.shape, F32)

        def body(c, carry):
            r0 = pl.multiple_of(c * tk, tk)
            chunk(qm, kbf[pl.ds(r0, tk), :], vbf[pl.ds(r0, tk), :])
            return carry

        lax.fori_loop(0, lk_new // tk, body, 0)
        if lc:
            chunk(qm, kbf[lk_new:lk_new + lc, :], vbf[lk_new:lk_new + lc, :])
        inv = 1.0 / l_ref[...]
        return acc_ref[...] * jnp.concatenate([inv, inv], axis=1)

    out = jnp.zeros((tq, WG), F32)
    if diff:
        lp = lam_ref[...]
        lam = (jnp.exp(jnp.sum(lp[0:1] * lp[1:2], axis=-1, keepdims=True))
               - jnp.exp(jnp.sum(lp[2:3] * lp[3:4], axis=-1, keepdims=True)) + lam_init)
    for h in range(N_HEADS):
        lo = h * HEAD_DIM
        hm = _lane_mask(WG, lo, lo + HEAD_DIM)
        if diff:
            o = stream(lo, lo + A_HALF) - lam * stream(lo + A_HALF, lo + HEAD_DIM)
            ms = jnp.sum(jnp.where(hm, o * o, 0.0), axis=-1, keepdims=True) * (1.0 / HEAD_DIM)
            o = ((o * lax.rsqrt(ms + EPS)) * sub_ref[...]) * (1.0 - lam_init)
        else:
            o = stream(lo, lo + HEAD_DIM)
        out = jnp.where(hm, o, out)
    o_ref[...] = out


def _attention(q, k, v, prev, *, diff, n_seq, seq_len, row_off, tq, tk, ctx=None, layer=0,
               lam=None, subln=None, lam_init=0.0):
    n = q.shape[0]
    assert row_off % seq_len == 0 and seq_len % tq == 0 and seq_len % tk == 0
    qpb = seq_len // tq
    q_map = lambda s, j: (row_off // tq + s * qpb + j, 0)
    kv_map = lambda s, j: (row_off // seq_len + s, 0)
    in_specs = [pl.BlockSpec((tq, WG), q_map), pl.BlockSpec((seq_len, WG), kv_map),
                pl.BlockSpec((seq_len, WG), kv_map)]
    args = [q, k, v]
    lc = 0
    if ctx is not None:
        kc, vc = ctx
        lc = kc.shape[2]
        cmap = lambda s, j: (s, layer, 0, 0)
        in_specs += [pl.BlockSpec((1, 1, lc, WG), cmap), pl.BlockSpec((1, 1, lc, WG), cmap)]
        args += [kc, vc]
    if diff:
        in_specs += [pl.BlockSpec(lam.shape, lambda s, j: (0, 0)), pl.BlockSpec((1, WG), lambda s, j: (0, 0))]
        args += [lam, subln]
    aliases = {}
    if prev is None:
        prev = jnp.zeros((SUBLANES, 128), F32)
    else:
        aliases = {len(args): 0}
    in_specs.append(pl.BlockSpec(memory_space=pl.ANY))
    args.append(prev)
    lk = seq_len + lc
    return pl.pallas_call(
        functools.partial(_attn_kernel, diff=diff, tq=tq, tk=tk, lk_new=seq_len, lc=lc, lam_init=lam_init),
        grid=(n_seq, qpb),
        in_specs=in_specs,
        out_specs=pl.BlockSpec((tq, WG), q_map),
        out_shape=jax.ShapeDtypeStruct((n, WG), F32),
        scratch_shapes=[pltpu.VMEM((lk, WG), BF16), pltpu.VMEM((lk, WG), BF16),
                        pltpu.VMEM((tq, WG), F32), pltpu.VMEM((tq, 128), F32), pltpu.VMEM((tq, 128), F32)],
        input_output_aliases=aliases,
        compiler_params=_cparams(("parallel", "arbitrary")),
        name="attn_diff" if diff else "attn_gqa",
    )(*args)


def _ssd_kernel(*refs, reverse, ncc, cpc, cpl):
    if reverse:
        (x_ref, xp_ref, xn_ref, dt_ref, h0_ref, cw_ref, cb_ref, dtb_ref, av_ref,
         z_ref, yf_ref, dsk_ref, cn_ref, y_ref, st_ref, s_ref) = refs
    else:
        (x_ref, xp_ref, xn_ref, dt_ref, h0_ref, cw_ref, cb_ref, dtb_ref, av_ref,
         y_ref, st_ref, s_ref) = refs
    q = C_CHUNK
    t = pl.program_id(0)
    c = pl.num_programs(0) - 1 - t if reverse else t
    is_lat = c >= ncc
    pos = jnp.where(is_lat, jnp.maximum(c - ncc, 0) % cpl, c % cpc)
    nper = jnp.where(is_lat, cpl, cpc)
    enter = (pos == nper - 1) if reverse else (pos == 0)

    @pl.when(enter)
    def _():
        s_ref[...] = h0_ref[0]

    x = x_ref[...]
    rid = lax.broadcasted_iota(jnp.int32, (q, 1), 0)
    prow = jnp.where(pos > 0, xp_ref[SUBLANES - 1:SUBLANES, :], 0.0)
    nrow = jnp.where(pos < nper - 1, xn_ref[0:1, :], 0.0)
    xm1 = jnp.where(rid == 0, prow, pltpu.roll(x, 1, 0))
    xp1 = jnp.where(rid == q - 1, nrow, pltpu.roll(x, q - 1, 0))
    cw = cw_ref[...]
    xbc = _silu(xm1 * cw[0:1] + x * cw[1:2] + xp1 * cw[2:3] + cb_ref[...])
    xs = xbc[:, 0:WG]
    bm = xbc[:, WG:WG + 128].astype(BF16)
    cm = xbc[:, WG + 128:WG + 256]

    raw = dt_ref[...] + dtb_ref[...]
    dt = jnp.maximum(raw, 0.0) + jnp.log1p(jnp.exp(-jnp.abs(raw)))
    dta = dt * av_ref[...]
    li = lax.broadcasted_iota(jnp.int32, (q, q), 0)
    si = lax.broadcasted_iota(jnp.int32, (q, q), 1)
    causal = (si >= li) if reverse else (si <= li)
    cum = jnp.dot(causal.astype(F32), dta, preferred_element_type=F32, precision=lax.Precision.HIGHEST)
    cum_t = cum.T
    end = 0 if reverse else q - 1
    d0 = N_HEADS if reverse else 0

    s_in = s_ref[...]
    rowh = lax.broadcasted_iota(jnp.int32, (WG, 1), 0) // HEAD_DIM
    colg = lax.broadcasted_iota(jnp.int32, (1, 128), 1) // C_STATE
    blk = (rowh // (N_HEADS // C_GROUPS)) == colg

    y = jnp.zeros((q, WG), F32)
    xw = jnp.zeros((q, WG), F32)
    ecum = jnp.zeros((q, WG), F32)
    cdec = jnp.zeros((WG, 1), F32)
    gmat = None
    for h in range(N_HEADS):
        j = d0 + h
        g = h // (N_HEADS // C_GROUPS)
        if h % (N_HEADS // C_GROUPS) == 0:
            cg = jnp.where(_lane_mask(128, g * C_STATE, (g + 1) * C_STATE), cm, 0.0).astype(BF16)
            gmat = lax.dot_general(cg, bm, (((1,), (1,)), ((), ())), preferred_element_type=F32)
        col = cum[:, j:j + 1]
        seg = col - cum_t[j:j + 1, :]
        decay = jnp.where(causal, jnp.exp(jnp.where(causal, seg, 0.0)), 0.0)
        hm = _lane_mask(WG, h * HEAD_DIM, (h + 1) * HEAD_DIM)
        xdt = jnp.where(hm, xs * dt[:, j:j + 1], 0.0)
        y = y + jnp.dot((gmat * decay).astype(BF16), xdt.astype(BF16), preferred_element_type=F32)
        cend = cum[end:end + 1, j:j + 1]
        xw = xw + xdt * jnp.exp(cend - col)
        ecum = jnp.where(hm, jnp.exp(col), ecum)
        cdec = jnp.where(rowh == h, jnp.exp(cend), cdec)

    s_msk = jnp.where(blk, s_in, 0.0).astype(BF16)
    y = y + lax.dot_general(cm.astype(BF16), s_msk, (((1,), (1,)), ((), ())), preferred_element_type=F32) * ecum
    st = lax.dot_general(xw.astype(BF16), bm, (((0,), (0,)), ((), ())), preferred_element_type=F32)
    s_new = s_in * cdec + st
    s_ref[...] = s_new
    st_ref[0] = s_new

    if reverse:
        y = y + yf_ref[...] + dsk_ref[...] * xs
        y = y * _silu(z_ref[...])
        y_ref[...] = _rms(y, cn_ref[...])
    else:
        y_ref[...] = y


def _ssd(cx, cdt, cz, h0, conv_w, conv_b, dtb, av, dsk, cnorm, *, ncc, cpc, cpl, n_ctx_seq):
    n = cx.shape[0]
    nc = n // C_CHUNK
    n_seq = h0.shape[1]
    sub = C_CHUNK // SUBLANES
    nb8 = n // SUBLANES

    def run(reverse, yf):
        cidx = (lambda t: nc - 1 - t) if reverse else (lambda t: t)

        def seq_of(t):
            c = cidx(t)
            return jnp.where(c < ncc, c // cpc, n_ctx_seq + jnp.maximum(c - ncc, 0) // cpl)

        d = 1 if reverse else 0
        row = lambda t: (cidx(t), 0)
        const2 = lambda t: (0, 0)
        in_specs = [pl.BlockSpec((C_CHUNK, 2 * WG), row),
                    pl.BlockSpec((SUBLANES, 2 * WG), lambda t: (jnp.maximum(cidx(t) * sub - 1, 0), 0)),
                    pl.BlockSpec((SUBLANES, 2 * WG), lambda t: (jnp.minimum((cidx(t) + 1) * sub, nb8 - 1), 0)),
                    pl.BlockSpec((C_CHUNK, 128), row),
                    pl.BlockSpec((1, 1, WG, 128), lambda t: (d, seq_of(t), 0, 0)),
                    pl.BlockSpec((3, 2 * WG), const2),
                    pl.BlockSpec((1, 2 * WG), const2),
                    pl.BlockSpec((1, 1, 128), lambda t: (d, 0, 0)),
                    pl.BlockSpec((1, 1, 128), lambda t: (d, 0, 0))]
        args = [cx, cx, cx, cdt, h0, conv_w, conv_b, dtb, av]
        if reverse:
            in_specs += [pl.BlockSpec((C_CHUNK, WG), row), pl.BlockSpec((C_CHUNK, WG), row),
                         pl.BlockSpec((1, WG), const2), pl.BlockSpec((1, WG), const2)]
            args += [cz, yf, dsk, cnorm]
        return pl.pallas_call(
            functools.partial(_ssd_kernel_wrap, reverse=reverse, ncc=ncc, cpc=cpc, cpl=cpl),
            grid=(nc,),
            in_specs=in_specs,
            out_specs=[pl.BlockSpec((C_CHUNK, WG), row),
                       pl.BlockSpec((1, WG, 128), lambda t: (seq_of(t), 0, 0))],
            out_shape=[jax.ShapeDtypeStruct((n, WG), F32), jax.ShapeDtypeStruct((n_seq, WG, 128), F32)],
            scratch_shapes=[pltpu.VMEM((WG, 128), F32)],
            compiler_params=_cparams(("arbitrary",)),
            name="ssd_bwd" if reverse else "ssd_fwd",
        )(*args)

    yf, st_f = run(False, None)
    yc, st_b = run(True, yf)
    return yc, st_f, st_b


def _ssd_kernel_wrap(*refs, reverse, ncc, cpc, cpl):
    refs = list(refs)
    refs[4] = refs[4].at[0]
    refs[7] = refs[7].at[0]
    refs[8] = refs[8].at[0]
    _ssd_kernel(*refs, reverse=reverse, ncc=ncc, cpc=cpc, cpl=cpl)


def _s5_kernel(u_ref, wb_ref, are_ref, aim_ref, wc_ref, h0r_ref, h0i_ref,
               y_ref, sr_ref, si_ref, bu_ref, xs_ref, cr_ref, ci_ref, *, tt, nl):
    hl = nl // 2

    @pl.when(pl.program_id(1) == 0)
    def _():
        cr_ref[...] = h0r_ref[0]
        ci_ref[...] = h0i_ref[0]

    bu_ref[...] = jnp.dot(u_ref[0], wb_ref[0], preferred_element_type=F32)
    a_re = are_ref[0]
    a_im = aim_ref[0]

    def step(t, carry):
        xr, xi = carry
        r0 = pl.multiple_of(t * SUBLANES, SUBLANES)
        nr = a_re * xr - a_im * xi + bu_ref[pl.ds(r0, SUBLANES), 0:hl]
        ni = a_re * xi + a_im * xr + bu_ref[pl.ds(r0, SUBLANES), hl:nl]
        xs_ref[pl.ds(r0, SUBLANES), 0:hl] = nr
        xs_ref[pl.ds(r0, SUBLANES), hl:nl] = ni
        return nr, ni

    xr, xi = lax.fori_loop(0, tt, step, (cr_ref[...], ci_ref[...]))
    cr_ref[...] = xr
    ci_ref[...] = xi
    sr_ref[0] = xr
    si_ref[0] = xi
    y_ref[0] = jnp.dot(xs_ref[...].astype(BF16), wc_ref[0], preferred_element_type=F32)


def _s5_scan(u_slab, wb, a_re, a_im, wc, h0r, h0i, *, slabs_per_dir, tt):
    n_slab, rows, _ = u_slab.shape
    nl = wb.shape[-1]
    hl = nl // 2
    steps = rows // SUBLANES
    assert steps % tt == 0
    dmap = lambda s, j: (s // slabs_per_dir, 0, 0)
    smap = lambda s, j: (s, 0, 0)
    return pl.pallas_call(
        functools.partial(_s5_kernel, tt=tt, nl=nl),
        grid=(n_slab, steps // tt),
        in_specs=[pl.BlockSpec((1, tt * SUBLANES, WG), lambda s, j: (s, j, 0)),
                  pl.BlockSpec((1, WG, nl), dmap),
                  pl.BlockSpec((1, SUBLANES, hl), smap),
                  pl.BlockSpec((1, SUBLANES, hl), smap),
                  pl.BlockSpec((1, nl, WG), dmap),
                  pl.BlockSpec((1, SUBLANES, hl), smap),
                  pl.BlockSpec((1, SUBLANES, hl), smap)],
        out_specs=[pl.BlockSpec((1, tt * SUBLANES, WG), lambda s, j: (s, j, 0)),
                   pl.BlockSpec((1, SUBLANES, hl), smap),
                   pl.BlockSpec((1, SUBLANES, hl), smap)],
        out_shape=[jax.ShapeDtypeStruct((n_slab, rows, WG), F32),
                   jax.ShapeDtypeStruct((n_slab, SUBLANES, hl), F32),
                   jax.ShapeDtypeStruct((n_slab, SUBLANES, hl), F32)],
        scratch_shapes=[pltpu.VMEM((tt * SUBLANES, nl), F32), pltpu.VMEM((tt * SUBLANES, nl), F32),
                        pltpu.VMEM((SUBLANES, hl), F32), pltpu.VMEM((SUBLANES, hl), F32)],
        compiler_params=_cparams(("parallel", "arbitrary")),
        name="s5_scan",
    )(u_slab, wb, a_re, a_im, wc, h0r, h0i)


def _s5_params(lam_re, lam_im, log_step, b_ri, c_ri):
    lam = lax.complex(lam_re, lam_im)
    a_bar = jnp.exp(lam * jnp.exp(log_step)[..., None])
    b_bar = ((a_bar - 1.0) / lam)[..., None] * lax.complex(b_ri[..., 0], b_ri[..., 1])
    return jnp.real(a_bar), jnp.imag(a_bar), jnp.real(b_bar), jnp.imag(b_bar), c_ri[..., 0], c_ri[..., 1]


def _s5_weights(a_re, a_im, bb_re, bb_im, c_re, c_im, fold):
    gpr = D_GROUPS // fold
    eye = jnp.eye(gpr, dtype=F32)

    def wb_part(bb):
        t = bb.reshape(2, fold, gpr, D_STATE, D_GROUP)
        src = jnp.transpose(t, (0, 1, 2, 4, 3))
        w = jnp.where(eye[None, None, :, None, :, None] > 0,
                      jnp.broadcast_to(src[:, :, :, :, None, :], (2, fold, gpr, D_GROUP, gpr, D_STATE)), 0.0)
        return w.reshape(2, WG, gpr * D_STATE)

    wb = jnp.concatenate([wb_part(bb_re), wb_part(bb_im)], axis=-1)

    def wc_part(cc):
        t = cc.reshape(2, fold, gpr, D_GROUP, D_STATE)
        src = jnp.transpose(t, (0, 2, 4, 1, 3))
        w = jnp.where(eye[None, :, None, None, :, None] > 0,
                      jnp.broadcast_to(src[:, :, :, :, None, :], (2, gpr, D_STATE, fold, gpr, D_GROUP)), 0.0)
        return w.reshape(2, gpr * D_STATE, WG)

    wc = jnp.concatenate([wc_part(c_re), -wc_part(c_im)], axis=1)

    def a_rows(a):
        return a.reshape(2, fold, gpr * D_STATE)

    return wb.astype(BF16), wc.astype(BF16), a_rows(a_re), a_rows(a_im)


def _outproj_kernel(x_ref, mod_ref, ya_ref, yb_ref, yc_ref, ydf_ref, ydb_ref, du_ref, dd_ref,
                    wglu_ref, wout_ref, g_ref, o_ref):
    mod = mod_ref[0]
    yd = ydf_ref[...] + ydb_ref[...] + dd_ref[...] * du_ref[...]
    yd = yd * (0.5 * (1.0 + jnp.tanh(math.sqrt(2.0 / math.pi) * (yd + 0.044715 * (yd * yd * yd)))))
    gl = jnp.dot(yd.astype(BF16), wglu_ref[...], preferred_element_type=F32)
    yd = gl[:, 0:WG] * _sigmoid(gl[:, WG:2 * WG])
    cat = jnp.concatenate([ya_ref[...].astype(BF16), yb_ref[...].astype(BF16),
                           yc_ref[...].astype(BF16), yd.astype(BF16)], axis=-1)
    y = jnp.dot(cat, wout_ref[...], preferred_element_type=F32)
    o_ref[...] = x_ref[...] + mod[2:3] * _rms(y, g_ref[...])


def _out_proj(x_all, mod, ya, yb, yc, ydf, ydb, du, dd, wglu, wout, g_post1, *, nct, tpl):
    n, d = x_all.shape
    row = lambda i: (i, 0)
    const2 = lambda i: (0, 0)
    mod_map = lambda i: (jnp.where(i < nct, 0, 1 + jnp.maximum(i - nct, 0) // tpl), 0, 0)
    sm = pl.BlockSpec((TM, WG), row)
    return pl.pallas_call(
        _outproj_kernel,
        grid=(n // TM,),
        in_specs=[pl.BlockSpec((TM, d), row), pl.BlockSpec((1, 6, d), mod_map),
                  sm, sm, sm, sm, sm, sm,
                  pl.BlockSpec((1, WG), const2), pl.BlockSpec(wglu.shape, const2),
                  pl.BlockSpec(wout.shape, const2), pl.BlockSpec((1, d), const2)],
        out_specs=pl.BlockSpec((TM, d), row),
        out_shape=jax.ShapeDtypeStruct((n, d), F32),
        compiler_params=_cparams(("parallel",)),
        name="out_proj",
    )(x_all, mod, ya, yb, yc, ydf, ydb, du, dd, wglu, wout, g_post1)


def _ffn_kernel(x_ref, xp_ref, xn_ref, mod_ref, g_ref, wup_ref, cw_ref, cb_ref, wdn_ref, gp_ref,
                o_ref, hext_ref, *, f, fc, nct, tpc, tpl):
    i = pl.program_id(0)
    is_lat = i >= nct
    pos = jnp.where(is_lat, jnp.maximum(i - nct, 0) % tpl, i % tpc)
    nper = jnp.where(is_lat, tpl, tpc)
    mod = mod_ref[0]

    def hfun(x):
        return _rms(x, g_ref[...]) * (1.0 + mod[4:5]) + mod[3:4]

    x = x_ref[...]
    hext_ref[0:SUBLANES, :] = jnp.where(pos > 0, hfun(xp_ref[...]), 0.0)
    hext_ref[SUBLANES:SUBLANES + TM, :] = hfun(x)
    hext_ref[SUBLANES + TM:2 * SUBLANES + TM, :] = jnp.where(pos < nper - 1, hfun(xn_ref[...]), 0.0)
    he = hext_ref[...].astype(BF16)
    rows = TM + 2 * SUBLANES

    def conv(u, c0):
        cw = cw_ref[:, c0:c0 + fc]
        um1 = pltpu.roll(u, 1, 0)[SUBLANES:SUBLANES + TM]
        up1 = pltpu.roll(u, rows - 1, 0)[SUBLANES:SUBLANES + TM]
        return (um1 * cw[0:1] + u[SUBLANES:SUBLANES + TM] * cw[1:2] + up1 * cw[2:3]
                + cb_ref[:, c0:c0 + fc])

    acc = jnp.zeros((TM, x.shape[1]), F32)
    for j in range(f // fc):
        ug = jnp.dot(he, wup_ref[:, j * fc:(j + 1) * fc], preferred_element_type=F32)
        uv = jnp.dot(he, wup_ref[:, f + j * fc:f + (j + 1) * fc], preferred_element_type=F32)
        a = _silu(conv(ug, j * fc)) * conv(uv, f + j * fc)
        acc = acc + jnp.dot(a.astype(BF16), wdn_ref[j * fc:(j + 1) * fc, :], preferred_element_type=F32)
    o_ref[...] = x + mod[5:6] * _rms(acc, gp_ref[...])


def _ffn(x_all, mod, g_pre2, wup, cw, cb, wdn, g_post2, *, nct, tpc, tpl):
    n, d = x_all.shape
    f = wdn.shape[0]
    fc = 256
    assert f % fc == 0
    sub = TM // SUBLANES
    nb8 = n // SUBLANES
    row = lambda i: (i, 0)
    const2 = lambda i: (0, 0)
    mod_map = lambda i: (jnp.where(i < nct, 0, 1 + jnp.maximum(i - nct, 0) // tpl), 0, 0)
    single = dict(pipeline_mode=pl.Buffered(1))
    return pl.pallas_call(
        functools.partial(_ffn_kernel, f=f, fc=fc, nct=nct, tpc=tpc, tpl=tpl),
        grid=(n // TM,),
        in_specs=[pl.BlockSpec((TM, d), row),
                  pl.BlockSpec((SUBLANES, d), lambda i: (jnp.maximum(i * sub - 1, 0), 0)),
                  pl.BlockSpec((SUBLANES, d), lambda i: (jnp.minimum((i + 1) * sub, nb8 - 1), 0)),
                  pl.BlockSpec((1, 6, d), mod_map),
                  pl.BlockSpec((1, d), const2),
                  pl.BlockSpec(wup.shape, const2, **single),
                  pl.BlockSpec(cw.shape, const2),
                  pl.BlockSpec(cb.shape, const2),
                  pl.BlockSpec(wdn.shape, const2, **single),
                  pl.BlockSpec((1, d), const2)],
        out_specs=pl.BlockSpec((TM, d), row),
        out_shape=jax.ShapeDtypeStruct((n, d), F32),
        scratch_shapes=[pltpu.VMEM((TM + 2 * SUBLANES, d), F32)],
        compiler_params=_cparams(("parallel",)),
        name="conv_ffn",
    )(x_all, x_all, x_all, mod, g_pre2, wup, cw, cb, wdn, g_post2)


def _ssd_expand(st):
    b = st.shape[0]
    z = jnp.zeros_like(st)
    hpg = N_HEADS // C_GROUPS
    left = jnp.concatenate([st[:, :hpg], z[:, hpg:]], axis=1)
    right = jnp.concatenate([z[:, :hpg], st[:, hpg:]], axis=1)
    return jnp.concatenate([left, right], axis=-1).reshape(b, WG, 2 * C_STATE)


def _ssd_extract(s):
    b = s.shape[0]
    s = s.reshape(b, N_HEADS, HEAD_DIM, C_GROUPS, C_STATE)
    hpg = N_HEADS // C_GROUPS
    return jnp.stack([s[:, h, :, h // hpg] for h in range(N_HEADS)], axis=1)


def kernel(x_prompt, x_sample, cache_a_k, cache_a_v, cache_b_k, cache_b_v, state_ssd, state_s5,
           c, c_ctx, w_mod, b_mod, g_pre1, g_post1, g_pre2, g_post2, w_in, a_lam, a_subln,
           b_qnorm, b_knorm, c_conv_w, c_conv_b, c_dt_bias, c_a_log, c_d, c_norm,
           d_lam_re, d_lam_im, d_log_step, d_b, d_c, d_d, d_glu, w_out, w_up,
           ffn_conv_w, ffn_conv_b, w_down):
    b1, l1, d = x_prompt.shape
    b2, l2, _ = x_sample.shape
    depth = w_mod.shape[0]
    past = cache_a_k.shape[2]
    n1, n2 = b1 * l1, b2 * l2
    n = n1 + n2
    assert d == 4 * WG and l1 % TM == 0 and l2 % TM == 0 and n1 % l2 == 0
    assert b1 % SUBLANES == 0 and SUBLANES % b2 == 0 and 1 + b2 <= SUBLANES
    nct, tpc, tpl = n1 // TM, l1 // TM, l2 // TM
    fold = SUBLANES // b2
    gpr = D_GROUPS // fold

    x_all = jnp.concatenate([x_prompt.reshape(n1, d), x_sample.reshape(n2, d)], axis=0)
    cond8 = jnp.concatenate([c_ctx[None, :], c, jnp.zeros((SUBLANES - 1 - b2, d), F32)], axis=0)
    mod_all = _modulation(cond8, w_mod, b_mod).reshape(depth, SUBLANES, 6, d)

    rt, ct = _rope_tables(l2)
    ones_blk = jnp.asarray(np.kron(np.eye(N_HEADS, dtype=np.float32),
                                   np.full((HEAD_DIM, HEAD_DIM), 1.0 / HEAD_DIM, np.float32))).astype(BF16)
    tile_h = lambda g: jnp.tile(g, N_HEADS)[None, :]
    rep_kv = lambda t: jnp.repeat(t, 2, axis=-2).reshape(*t.shape[:-2], WG)
    cak = cache_a_k.reshape(b2, depth, past, WG)
    cav = cache_a_v.reshape(b2, depth, past, WG)
    cbk = rep_kv(cache_b_k)
    cbv = rep_kv(cache_b_v)

    new_ak, new_av, new_bk, new_bv, new_ssd, new_s5 = [], [], [], [], [], []
    for l in range(depth):
        mod = mod_all[l]
        lam_init = 0.8 - 0.6 * math.exp(-0.3 * l)
        w = w_in[l]
        bk_w, bv_w = w[:, 1024:1152], w[:, 1152:1280]
        dup = lambda t: jnp.concatenate([t[:, :64], t[:, :64], t[:, 64:], t[:, 64:]], axis=1)
        w_in_p = jnp.concatenate(
            [w[:, 0:1024], dup(bk_w), dup(bv_w), w[:, 1280:2048], w[:, 2056:2312],
             w[:, 2048:2056], jnp.zeros((d, 120), F32)], axis=1).astype(BF16)

        qa, ka, va, qb, kb, vb, cz, cx, du, cdt = _in_proj(
            x_all, mod, g_pre1[l][None, :], w_in_p, tile_h(b_qnorm[l]), tile_h(b_knorm[l]),
            ones_blk, rt, ct, nct=nct, tpl=tpl)

        attn = functools.partial(_attention, tq=TM, tk=TM)
        lam_p, sub_p = a_lam[l], tile_h(a_subln[l])
        ya = attn(qa, ka, va, None, diff=True, n_seq=b1, seq_len=l1, row_off=0,
                  lam=lam_p, subln=sub_p, lam_init=lam_init)
        ya = attn(qa, ka, va, ya, diff=True, n_seq=b2, seq_len=l2, row_off=n1, ctx=(cak, cav), layer=l,
                  lam=lam_p, subln=sub_p, lam_init=lam_init)
        yb = attn(qb, kb, vb, None, diff=False, n_seq=b1, seq_len=l1, row_off=0)
        yb = attn(qb, kb, vb, yb, diff=False, n_seq=b2, seq_len=l2, row_off=n1, ctx=(cbk, cbv), layer=l)

        h0_lat = jnp.stack([_ssd_expand(state_ssd[:, l, dd_]) for dd_ in range(2)], axis=0)
        h0 = jnp.concatenate([jnp.zeros((2, b1, WG, 2 * C_STATE), F32), h0_lat], axis=1)
        pad8 = lambda t: jnp.concatenate([t, jnp.zeros((2, 128 - 2 * N_HEADS), F32)], axis=1)
        dtb = pad8(jnp.stack([jnp.concatenate([c_dt_bias[l, 0], jnp.zeros((N_HEADS,), F32)]),
                              jnp.concatenate([jnp.zeros((N_HEADS,), F32), c_dt_bias[l, 1]])]))
        a_neg = -jnp.exp(c_a_log[l])
        av = pad8(jnp.stack([jnp.concatenate([a_neg[0], jnp.zeros((N_HEADS,), F32)]),
                             jnp.concatenate([jnp.zeros((N_HEADS,), F32), a_neg[1]])]))
        yc, st_f, st_b = _ssd(cx, cdt, cz, h0, c_conv_w[l], c_conv_b[l][None, :],
                              dtb[:, None, :], av[:, None, :],
                              jnp.repeat(c_d[l], HEAD_DIM)[None, :], c_norm[l][None, :],
                              ncc=n1 // C_CHUNK, cpc=l1 // C_CHUNK, cpl=l2 // C_CHUNK, n_ctx_seq=b1)
        new_ssd.append(jnp.stack([_ssd_extract(st_f[:b1]), _ssd_extract(st_b[:b1])], axis=1))

        pr = _s5_params(d_lam_re[l], d_lam_im[l], d_log_step[l], d_b[l], d_c[l])
        u1 = du[:n1].reshape(b1, l1, WG)
        u2 = du[n1:].reshape(b2, l2, WG)
        wb1, wc1, ar1, ai1 = _s5_weights(*pr, 1)
        nsl = b1 // SUBLANES

        def slabs1(u):
            return jnp.transpose(u.reshape(nsl, SUBLANES, l1, WG), (0, 2, 1, 3)).reshape(nsl, l1 * SUBLANES, WG)

        us1 = jnp.concatenate([slabs1(u1), slabs1(u1[:, ::-1])], axis=0).astype(BF16)
        arow = lambda a: jnp.broadcast_to(a[:, None, :, :], (2, nsl, 1, a.shape[-1])) * jnp.ones((1, 1, SUBLANES, 1), F32)
        zero1 = jnp.zeros((2 * nsl, SUBLANES, S5_LANES // 2), F32)
        y1, s1r, s1i = _s5_scan(us1, wb1, arow(ar1).reshape(2 * nsl, SUBLANES, -1),
                                arow(ai1).reshape(2 * nsl, SUBLANES, -1), wc1, zero1, zero1,
                                slabs_per_dir=nsl, tt=min(l1, 128))
        y1 = jnp.transpose(y1.reshape(2, nsl, l1, SUBLANES, WG), (0, 1, 3, 2, 4)).reshape(2, b1, l1, WG)
        st1 = jnp.stack([s1r, s1i], axis=-1).reshape(2, b1, D_GROUPS, D_STATE, 2)
        new_s5.append(jnp.transpose(st1, (1, 0, 2, 3, 4)))
        wb2, wc2, ar2, ai2 = _s5_weights(*pr, fold)
        qmask = (jnp.arange(WG)[None, :] // (gpr * D_GROUP) == jnp.arange(fold)[:, None]).astype(F32)

        def slab2(u):
            t = jnp.transpose(u, (1, 0, 2))[:, :, None, :] * qmask[None, None, :, :]
            return t.reshape(l2 * SUBLANES, WG)

        us2 = jnp.stack([slab2(u2), slab2(u2[:, ::-1])], axis=0).astype(BF16)
        a2 = lambda a: jnp.broadcast_to(a[:, None, :, :], (2, b2, fold, a.shape[-1])).reshape(2, SUBLANES, -1)
        h0s = state_s5[:, l].reshape(b2, 2, fold, gpr * D_STATE, 2)
        h0s = jnp.transpose(h0s, (1, 0, 2, 3, 4)).reshape(2, SUBLANES, gpr * D_STATE, 2)
        y2, _, _ = _s5_scan(us2, wb2, a2(ar2), a2(ai2), wc2, h0s[..., 0], h0s[..., 1],
                            slabs_per_dir=1, tt=min(l2, 512))
        y2 = y2.reshape(2, l2, b2, fold, fold, WG // fold)
        y2 = jnp.concatenate([y2[:, :, :, qq, qq, :] for qq in range(fold)], axis=-1)
        y2 = jnp.transpose(y2, (0, 2, 1, 3))
        ydf = jnp.concatenate([y1[0].reshape(n1, WG), y2[0].reshape(n2, WG)], axis=0)
        ydb = jnp.concatenate([y1[1, :, ::-1].reshape(n1, WG), y2[1, :, ::-1].reshape(n2, WG)], axis=0)

        x_all = _out_proj(x_all, mod, ya, yb, yc, ydf, ydb, du, d_d[l][None, :],
                          d_glu[l].astype(BF16), w_out[l].astype(BF16), g_post1[l][None, :], nct=nct, tpl=tpl)
        x_all = _ffn(x_all, mod, g_pre2[l][None, :], w_up[l].astype(BF16), ffn_conv_w[l],
                     ffn_conv_b[l][None, :], w_down[l].astype(BF16), g_post2[l][None, :],
                     nct=nct, tpc=tpc, tpl=tpl)

        new_ak.append(ka[:n1].reshape(b1, l1, N_HEADS, HEAD_DIM))
        new_av.append(va[:n1].reshape(b1, l1, N_HEADS, HEAD_DIM))
        pick = lambda t: t[:n1].reshape(b1, l1, N_HEADS, HEAD_DIM)[:, :, ::2, :]
        new_bk.append(pick(kb))
        new_bv.append(pick(vb))

    y_prompt = x_all[:n1].reshape(b1, l1, d)
    y_sample = x_all[n1:].reshape(b2, l2, d)
    st = lambda xs: jnp.stack(xs, axis=1)
    return (y_prompt, y_sample, st(new_ak), st(new_av), st(new_bk), st(new_bv), st(new_ssd), st(new_s5))
```

```python
import functools
import math

import numpy as np
import jax
import jax.numpy as jnp
from jax import lax
from jax.experimental import pallas as pl
from jax.experimental.pallas import tpu as pltpu

F32 = jnp.float32
BF16 = jnp.bfloat16

EPS = 1e-6
LOG2E = math.log2(math.e)
GRID_W = 64
ROPE_BASE = 10000.0
HEAD_DIM = 64
A_HALF = HEAD_DIM // 2
N_HEADS = 4
WG = N_HEADS * HEAD_DIM
C_GROUPS = 2
C_STATE = 64
C_CHUNK = 128
D_GROUP = 16
D_GROUPS = WG // D_GROUP
D_STATE = 64
S5_LANES = D_GROUPS * D_STATE * 2

TM = 256
SUBLANES = 8
VMEM_LIMIT = 56 * 1024 * 1024

_C_AQ, _C_AK, _C_AV, _C_BQ, _C_BK, _C_BV, _C_CZ, _C_CX, _C_DU, _C_DT, _C_END = (
    0, 256, 512, 768, 1024, 1280, 1536, 1792, 2304, 2560, 2688)


def _sigmoid(x):
    return 1.0 / (1.0 + jnp.exp(-x))


def _silu(x):
    return x * _sigmoid(x)


def _cparams(sem):
    return pltpu.CompilerParams(dimension_semantics=sem, vmem_limit_bytes=VMEM_LIMIT)


def _lane_mask(width, lo, hi):
    lane = lax.broadcasted_iota(jnp.int32, (1, width), 1)
    return (lane >= lo) & (lane < hi)


def _mod_kernel(c_ref, w_ref, b_ref, o_ref):
    s = _silu(c_ref[...])
    o_ref[0] = jnp.dot(s.astype(BF16), w_ref[0].astype(BF16), preferred_element_type=F32) + b_ref[0]


def _modulation(cond8, w_mod, b_mod):
    depth, d, n = w_mod.shape
    tn = 1536
    return pl.pallas_call(
        _mod_kernel,
        grid=(depth, n // tn),
        in_specs=[pl.BlockSpec((SUBLANES, d), lambda l, j: (0, 0)),
                  pl.BlockSpec((1, d, tn), lambda l, j: (l, 0, j)),
                  pl.BlockSpec((1, 1, tn), lambda l, j: (l, 0, j))],
        out_specs=pl.BlockSpec((1, SUBLANES, tn), lambda l, j: (l, 0, j)),
        out_shape=jax.ShapeDtypeStruct((depth, SUBLANES, n), F32),
        compiler_params=_cparams(("parallel", "parallel")),
        name="modulation",
    )(cond8, w_mod, b_mod.reshape(depth, 1, n))


def _rms(x, g):
    return (x * lax.rsqrt(jnp.mean(x * x, axis=-1, keepdims=True) + EPS)) * g


def _inproj_kernel(x_ref, mod_ref, g_ref, w_ref, qn_ref, kn_ref, ones_ref, rt_ref, ct_ref,
                   qa_ref, ka_ref, va_ref, qb_ref, kb_ref, vb_ref,
                   ka32_ref, va32_ref, kb32_ref, vb32_ref, cz_ref, cx_ref, du_ref, dt_ref,
                   *, nct):
    i = pl.program_id(0)
    mod = mod_ref[0]
    h = _rms(x_ref[...], g_ref[...]) * (1.0 + mod[1:2]) + mod[0:1]
    p = jnp.dot(h.astype(BF16), w_ref[...], preferred_element_type=F32)
    va_ref[...] = p[:, _C_AV:_C_BQ].astype(BF16)
    vb_ref[...] = p[:, _C_BV:_C_CZ].astype(BF16)
    va32_ref[...] = p[:, _C_AV:_C_BQ]
    vb32_ref[...] = p[:, _C_BV:_C_CZ]
    cz_ref[...] = p[:, _C_CZ:_C_CX]
    cx_ref[...] = p[:, _C_CX:_C_DU]
    du_ref[...] = p[:, _C_DU:_C_DT]
    dt_ref[...] = p[:, _C_DT:_C_END]

    def headnorm(t, gain):
        sq = t * t
        hi = sq.astype(BF16)
        lo = (sq - hi.astype(F32)).astype(BF16)
        ms = (jnp.dot(hi, ones_ref[...], preferred_element_type=F32)
              + jnp.dot(lo, ones_ref[...], preferred_element_type=F32))
        return (t * lax.rsqrt(ms + EPS)) * gain

    aq = p[:, _C_AQ:_C_AK] * (A_HALF ** -0.5 * LOG2E)
    ak = p[:, _C_AK:_C_AV]
    bq = headnorm(p[:, _C_BQ:_C_BK], qn_ref[...]) * (HEAD_DIM ** -0.5 * LOG2E)
    bk = headnorm(p[:, _C_BK:_C_BV], kn_ref[...])
    ka32_ref[...] = ak
    kb32_ref[...] = bk

    @pl.when(i < nct)
    def _():
        qa_ref[...] = aq.astype(BF16)
        ka_ref[...] = ak.astype(BF16)
        qb_ref[...] = bq.astype(BF16)
        kb_ref[...] = bk.astype(BF16)

    @pl.when(i >= nct)
    def _():
        rows = TM // GRID_W
        rt = rt_ref[0]
        lane = lax.broadcasted_iota(jnp.int32, (1, WG), 1)
        row_lane_a = (lane % (A_HALF // 2)) < A_HALF // 4
        row_lane_b = (lane % (HEAD_DIM // 2)) < HEAD_DIM // 4

        def table(k, row_lane):
            parts = [jnp.where(row_lane, rt[k, r:r + 1, :], ct_ref[k]) for r in range(rows)]
            return jnp.concatenate(parts, axis=0)

        def rope(t, base, row_lane, dist):
            return (t * table(base, row_lane)
                    + pltpu.roll(t, WG - dist, 1) * table(base + 1, row_lane)
                    + pltpu.roll(t, dist, 1) * table(base + 2, row_lane))

        qa_ref[...] = rope(aq, 0, row_lane_a, A_HALF // 2).astype(BF16)
        ka_ref[...] = rope(ak, 0, row_lane_a, A_HALF // 2).astype(BF16)
        qb_ref[...] = rope(bq, 3, row_lane_b, HEAD_DIM // 2).astype(BF16)
        kb_ref[...] = rope(bk, 3, row_lane_b, HEAD_DIM // 2).astype(BF16)


def _rope_tables(l2):
    rows = l2 // GRID_W
    lane = np.arange(WG)

    def one(block, n):
        freqs = (np.float32(ROPE_BASE) ** (-np.arange(n, dtype=np.float32) / np.float32(n))).astype(np.float32)
        p = lane % block
        idx = p % (block // 2)
        first = p < (block // 2)
        is_row = idx < n
        f = freqs[idx % n]

        def tabs(pos):
            ang = (pos[:, None].astype(np.float32) * f[None, :]).astype(np.float32)
            c, s = np.cos(ang).astype(np.float32), np.sin(ang).astype(np.float32)
            return [c, np.where(first[None, :], -s, 0.0).astype(np.float32),
                    np.where(first[None, :], 0.0, s).astype(np.float32)]

        return tabs(np.arange(rows)), tabs(np.arange(GRID_W)), is_row

    ra, ca, _ = one(A_HALF, A_HALF // 4)
    rb, cb, _ = one(HEAD_DIM, HEAD_DIM // 4)
    rt = np.stack(ra + rb, axis=0)
    ct = np.stack(ca + cb, axis=0)
    rpt = TM // GRID_W
    npt = l2 // TM
    rt = rt.reshape(6, npt, rpt, WG).transpose(1, 0, 2, 3)
    rt = np.concatenate([rt, np.zeros((npt, 6, SUBLANES - rpt, WG), np.float32)], axis=2)
    return jnp.asarray(rt), jnp.asarray(ct)


def _in_proj(x_all, mod, g_pre1, w_in_p, qn, kn, ones_blk, rt, ct, *, nct, tpl):
    n, d = x_all.shape
    nt = n // TM
    row = lambda i: (i, 0)
    const2 = lambda i: (0, 0)
    mod_map = lambda i: (jnp.where(i < nct, 0, 1 + jnp.maximum(i - nct, 0) // tpl), 0, 0)
    ctx_row = lambda i: (jnp.minimum(i, nct), 0)
    n_ctx = (nct + 1) * TM
    out_specs = ([pl.BlockSpec((TM, WG), row)] * 6 + [pl.BlockSpec((TM, WG), ctx_row)] * 4
                 + [pl.BlockSpec((TM, w), row) for w in (WG, 2 * WG, WG, 128)])
    out_shape = ([jax.ShapeDtypeStruct((n, WG), BF16)] * 6 + [jax.ShapeDtypeStruct((n_ctx, WG), F32)] * 4
                 + [jax.ShapeDtypeStruct((n, w), F32) for w in (WG, 2 * WG, WG, 128)])
    return pl.pallas_call(
        functools.partial(_inproj_kernel, nct=nct),
        grid=(nt,),
        in_specs=[pl.BlockSpec((TM, d), row),
                  pl.BlockSpec((1, 6, d), mod_map),
                  pl.BlockSpec((1, d), const2),
                  pl.BlockSpec(w_in_p.shape, const2),
                  pl.BlockSpec((1, WG), const2),
                  pl.BlockSpec((1, WG), const2),
                  pl.BlockSpec((WG, WG), const2),
                  pl.BlockSpec((1, 6, SUBLANES, WG), lambda i: (jnp.maximum(i - nct, 0) % tpl, 0, 0, 0)),
                  pl.BlockSpec(ct.shape, lambda i: (0, 0, 0))],
        out_specs=out_specs,
        out_shape=out_shape,
        compiler_params=_cparams(("arbitrary",)),
        name="in_proj",
    )(x_all, mod, g_pre1, w_in_p, qn, kn, ones_blk, rt, ct)


def _attn_kernel(*refs, diff, tq, lk_new, lc, ck, lam_init):
    it = iter(refs)
    q_ref, k_ref, v_ref = next(it), next(it), next(it)
    kc_ref = vc_ref = None
    if lc:
        kc_ref, vc_ref = next(it).at[0, 0], next(it).at[0, 0]
    lam_ref = sub_ref = None
    if diff:
        lam_ref, sub_ref = next(it), next(it)
    prev_ref = next(it)
    o_ref = next(it)
    s_refs = (next(it), next(it))
    oacc_ref = next(it)
    del prev_ref

    blocks = [(k_ref, v_ref, r0, min(ck, lk_new - r0), r0) for r0 in range(0, lk_new, ck)]
    if lc:
        blocks.append((kc_ref, vc_ref, 0, lc, lk_new))

    q = q_ref[...]
    lane = lax.broadcasted_iota(jnp.int32, (1, WG), 1)

    def stream(lo, hi, s_ref):
        qm = jnp.where((lane >= lo) & (lane < hi), q, jnp.zeros_like(q))
        mrun = jnp.full((tq, 128), -jnp.inf, F32)
        for kr, _, r0, rows, c0 in blocks:
            s = lax.dot_general(qm, kr[r0:r0 + rows, :], (((1,), (1,)), ((), ())), preferred_element_type=F32)
            s_ref[:, c0:c0 + rows] = s
            for t in range(rows // 128):
                mrun = jnp.maximum(mrun, s[:, t * 128:(t + 1) * 128])
        m = jnp.max(mrun, axis=-1, keepdims=True)
        lrun = jnp.zeros((tq, 128), F32)
        acc = jnp.zeros((tq, WG), F32)
        for _, vr, r0, rows, c0 in blocks:
            p = jnp.exp2(s_ref[:, c0:c0 + rows] - m)
            for t in range(rows // 128):
                lrun = lrun + p[:, t * 128:(t + 1) * 128]
            acc = acc + jnp.dot(p.astype(BF16), vr[r0:r0 + rows, :], preferred_element_type=F32)
        return acc * (1.0 / jnp.sum(lrun, axis=-1, keepdims=True))

    oacc_ref[...] = jnp.zeros(oacc_ref.shape, F32)
    if diff:
        lp = lam_ref[...]
        lam = (jnp.exp(jnp.sum(lp[0:1] * lp[1:2], axis=-1, keepdims=True))
               - jnp.exp(jnp.sum(lp[2:3] * lp[3:4], axis=-1, keepdims=True)) + lam_init)

        def head(h, carry):
            lo = h * HEAD_DIM
            hm = (lane >= lo) & (lane < lo + HEAD_DIM)
            o = stream(lo, lo + A_HALF, s_refs[0]) - lam * stream(lo + A_HALF, lo + HEAD_DIM, s_refs[1])
            ms = jnp.sum(jnp.where(hm, o * o, 0.0), axis=-1, keepdims=True) * (1.0 / HEAD_DIM)
            o = ((o * lax.rsqrt(ms + EPS)) * sub_ref[...]) * (1.0 - lam_init)
            oacc_ref[...] = jnp.where(hm, o, oacc_ref[...])
            return carry

        lax.fori_loop(0, N_HEADS, head, 0)
    else:
        def head_pair(g, carry):
            for e in range(2):
                lo = (2 * g + e) * HEAD_DIM
                hm = (lane >= lo) & (lane < lo + HEAD_DIM)
                oacc_ref[...] = jnp.where(hm, stream(lo, lo + HEAD_DIM, s_refs[e]), oacc_ref[...])
            return carry

        lax.fori_loop(0, N_HEADS // 2, head_pair, 0)
    o_ref[...] = oacc_ref[...]


def _attention(q, k, v, prev, *, diff, n_seq, seq_len, row_off, tq, ck, ctx=None, layer=0,
               lam=None, subln=None, lam_init=0.0):
    n = q.shape[0]
    assert row_off % seq_len == 0 and seq_len % tq == 0
    qpb = seq_len // tq
    q_map = lambda s, j: (row_off // tq + s * qpb + j, 0)
    kv_map = lambda s, j: (row_off // seq_len + s, 0)
    in_specs = [pl.BlockSpec((tq, WG), q_map), pl.BlockSpec((seq_len, WG), kv_map),
                pl.BlockSpec((seq_len, WG), kv_map)]
    args = [q, k, v]
    lc = 0
    if ctx is not None:
        kc, vc = ctx
        lc = kc.shape[2]
        cmap = lambda s, j: (s, layer, 0, 0)
        in_specs += [pl.BlockSpec((1, 1, lc, WG), cmap), pl.BlockSpec((1, 1, lc, WG), cmap)]
        args += [kc, vc]
    if diff:
        in_specs += [pl.BlockSpec(lam.shape, lambda s, j: (0, 0)), pl.BlockSpec((1, WG), lambda s, j: (0, 0))]
        args += [lam, subln]
    aliases = {}
    if prev is None:
        prev = jnp.zeros((SUBLANES, 128), F32)
    else:
        aliases = {len(args): 0}
    in_specs.append(pl.BlockSpec(memory_space=pl.ANY))
    args.append(prev)
    lk = seq_len + lc
    return pl.pallas_call(
        functools.partial(_attn_kernel, diff=diff, tq=tq, lk_new=seq_len, lc=lc, ck=min(ck, seq_len),
                          lam_init=lam_init),
        grid=(n_seq, qpb),
        in_specs=in_specs,
        out_specs=pl.BlockSpec((tq, WG), q_map),
        out_shape=jax.ShapeDtypeStruct((n, WG), F32),
        scratch_shapes=[pltpu.VMEM((tq, lk), F32), pltpu.VMEM((tq, lk), F32), pltpu.VMEM((tq, WG), F32)],
        input_output_aliases=aliases,
        compiler_params=_cparams(("parallel", "parallel")),
        name="attn_diff" if diff else "attn_gqa",
    )(*args)


def _ssd_kernel(*refs, reverse, ncc, cpc, cpl):
    if reverse:
        (x_ref, xp_ref, xn_ref, dt_ref, h0_ref, cw_ref, cb_ref, dtb_ref, av_ref,
         z_ref, yf_ref, dsk_ref, cn_ref, y_ref, st_ref, s_ref) = refs
    else:
        (x_ref, xp_ref, xn_ref, dt_ref, h0_ref, cw_ref, cb_ref, dtb_ref, av_ref,
         y_ref, st_ref, s_ref) = refs
    q = C_CHUNK
    t = pl.program_id(0)
    c = pl.num_programs(0) - 1 - t if reverse else t
    is_lat = c >= ncc
    pos = jnp.where(is_lat, jnp.maximum(c - ncc, 0) % cpl, c % cpc)
    nper = jnp.where(is_lat, cpl, cpc)
    enter = (pos == nper - 1) if reverse else (pos == 0)

    @pl.when(enter)
    def _():
        s_ref[...] = h0_ref[0]

    x = x_ref[...]
    rid = lax.broadcasted_iota(jnp.int32, (q, 1), 0)
    prow = jnp.where(pos > 0, xp_ref[SUBLANES - 1:SUBLANES, :], 0.0)
    nrow = jnp.where(pos < nper - 1, xn_ref[0:1, :], 0.0)
    xm1 = jnp.where(rid == 0, prow, pltpu.roll(x, 1, 0))
    xp1 = jnp.where(rid == q - 1, nrow, pltpu.roll(x, q - 1, 0))
    cw = cw_ref[...]
    xbc = _silu(xm1 * cw[0:1] + x * cw[1:2] + xp1 * cw[2:3] + cb_ref[...])
    xs = xbc[:, 0:WG]
    bm = xbc[:, WG:WG + 128].astype(BF16)
    cm = xbc[:, WG + 128:WG + 256]

    raw = dt_ref[...] + dtb_ref[...]
    dt = jnp.maximum(raw, 0.0) + jnp.log1p(jnp.exp(-jnp.abs(raw)))
    dta = dt * av_ref[...]
    li = lax.broadcasted_iota(jnp.int32, (q, q), 0)
    si = lax.broadcasted_iota(jnp.int32, (q, q), 1)
    causal = (si >= li) if reverse else (si <= li)
    cum = jnp.dot(causal.astype(F32), dta, preferred_element_type=F32, precision=lax.Precision.HIGHEST)
    cum_t = cum.T
    end = 0 if reverse else q - 1
    d0 = N_HEADS if reverse else 0

    s_in = s_ref[...]
    rowh = lax.broadcasted_iota(jnp.int32, (WG, 1), 0) // HEAD_DIM
    colg = lax.broadcasted_iota(jnp.int32, (1, 128), 1) // C_STATE
    blk = (rowh // (N_HEADS // C_GROUPS)) == colg

    y = jnp.zeros((q, WG), F32)
    xw = jnp.zeros((q, WG), F32)
    ecum = jnp.zeros((q, WG), F32)
    cdec = jnp.zeros((WG, 1), F32)
    gmat = None
    for h in range(N_HEADS):
        j = d0 + h
        g = h // (N_HEADS // C_GROUPS)
        if h % (N_HEADS // C_GROUPS) == 0:
            cg = jnp.where(_lane_mask(128, g * C_STATE, (g + 1) * C_STATE), cm, 0.0).astype(BF16)
            gmat = lax.dot_general(cg, bm, (((1,), (1,)), ((), ())), preferred_element_type=F32)
        col = cum[:, j:j + 1]
        seg = col - cum_t[j:j + 1, :]
        decay = jnp.where(causal, jnp.exp(jnp.where(causal, seg, 0.0)), 0.0)
        hm = _lane_mask(WG, h * HEAD_DIM, (h + 1) * HEAD_DIM)
        xdt = jnp.where(hm, xs * dt[:, j:j + 1], 0.0)
        y = y + jnp.dot((gmat * decay).astype(BF16), xdt.astype(BF16), preferred_element_type=F32)
        cend = cum[end:end + 1, j:j + 1]
        xw = xw + xdt * jnp.exp(cend - col)
        ecum = jnp.where(hm, jnp.exp(col), ecum)
        cdec = jnp.where(rowh == h, jnp.exp(cend), cdec)

    s_msk = jnp.where(blk, s_in, 0.0).astype(BF16)
    y = y + lax.dot_general(cm.astype(BF16), s_msk, (((1,), (1,)), ((), ())), preferred_element_type=F32) * ecum
    st = lax.dot_general(xw.astype(BF16), bm, (((0,), (0,)), ((), ())), preferred_element_type=F32)
    s_new = s_in * cdec + st
    s_ref[...] = s_new
    st_ref[0] = s_new

    if reverse:
        y = y + yf_ref[...] + dsk_ref[...] * xs
        y = y * _silu(z_ref[...])
        y_ref[...] = _rms(y, cn_ref[...])
    else:
        y_ref[...] = y


def _ssd(cx, cdt, cz, h0, conv_w, conv_b, dtb, av, dsk, cnorm, *, ncc, cpc, cpl, n_ctx_seq):
    n = cx.shape[0]
    nc = n // C_CHUNK
    n_seq = h0.shape[1]
    sub = C_CHUNK // SUBLANES
    nb8 = n // SUBLANES

    def run(reverse, yf):
        cidx = (lambda t: nc - 1 - t) if reverse else (lambda t: t)

        def seq_of(t):
            c = cidx(t)
            return jnp.where(c < ncc, c // cpc, n_ctx_seq + jnp.maximum(c - ncc, 0) // cpl)

        d = 1 if reverse else 0
        row = lambda t: (cidx(t), 0)
        const2 = lambda t: (0, 0)
        in_specs = [pl.BlockSpec((C_CHUNK, 2 * WG), row),
                    pl.BlockSpec((SUBLANES, 2 * WG), lambda t: (jnp.maximum(cidx(t) * sub - 1, 0), 0)),
                    pl.BlockSpec((SUBLANES, 2 * WG), lambda t: (jnp.minimum((cidx(t) + 1) * sub, nb8 - 1), 0)),
                    pl.BlockSpec((C_CHUNK, 128), row),
                    pl.BlockSpec((1, 1, WG, 128), lambda t: (d, seq_of(t), 0, 0)),
                    pl.BlockSpec((3, 2 * WG), const2),
                    pl.BlockSpec((1, 2 * WG), const2),
                    pl.BlockSpec((1, 1, 128), lambda t: (d, 0, 0)),
                    pl.BlockSpec((1, 1, 128), lambda t: (d, 0, 0))]
        args = [cx, cx, cx, cdt, h0, conv_w, conv_b, dtb, av]
        if reverse:
            in_specs += [pl.BlockSpec((C_CHUNK, WG), row), pl.BlockSpec((C_CHUNK, WG), row),
                         pl.BlockSpec((1, WG), const2), pl.BlockSpec((1, WG), const2)]
            args += [cz, yf, dsk, cnorm]
        return pl.pallas_call(
            functools.partial(_ssd_kernel_wrap, reverse=reverse, ncc=ncc, cpc=cpc, cpl=cpl),
            grid=(nc,),
            in_specs=in_specs,
            out_specs=[pl.BlockSpec((C_CHUNK, WG), row),
                       pl.BlockSpec((1, WG, 128), lambda t: (seq_of(t), 0, 0))],
            out_shape=[jax.ShapeDtypeStruct((n, WG), F32), jax.ShapeDtypeStruct((n_seq, WG, 128), F32)],
            scratch_shapes=[pltpu.VMEM((WG, 128), F32)],
            compiler_params=_cparams(("arbitrary",)),
            name="ssd_bwd" if reverse else "ssd_fwd",
        )(*args)

    yf, st_f = run(False, None)
    yc, st_b = run(True, yf)
    return yc, st_f, st_b


def _ssd_kernel_wrap(*refs, reverse, ncc, cpc, cpl):
    refs = list(refs)
    refs[4] = refs[4].at[0]
    refs[7] = refs[7].at[0]
    refs[8] = refs[8].at[0]
    _ssd_kernel(*refs, reverse=reverse, ncc=ncc, cpc=cpc, cpl=cpl)


def _s5_kernel(u_ref, wb_ref, are_ref, aim_ref, wc_ref, h0r_ref, h0i_ref,
               y_ref, sr_ref, si_ref, bu_ref, xs_ref, cr_ref, ci_ref, *, tt, nl):
    hl = nl // 2

    @pl.when(pl.program_id(1) == 0)
    def _():
        cr_ref[...] = h0r_ref[0]
        ci_ref[...] = h0i_ref[0]

    bu_ref[...] = jnp.dot(u_ref[0], wb_ref[0], preferred_element_type=F32)
    a_re = are_ref[0]
    a_im = aim_ref[0]

    def step(t, carry):
        xr, xi = carry
        r0 = pl.multiple_of(t * SUBLANES, SUBLANES)
        nr = a_re * xr - a_im * xi + bu_ref[pl.ds(r0, SUBLANES), 0:hl]
        ni = a_re * xi + a_im * xr + bu_ref[pl.ds(r0, SUBLANES), hl:nl]
        xs_ref[pl.ds(r0, SUBLANES), 0:hl] = nr
        xs_ref[pl.ds(r0, SUBLANES), hl:nl] = ni
        return nr, ni

    xr, xi = lax.fori_loop(0, tt, step, (cr_ref[...], ci_ref[...]))
    cr_ref[...] = xr
    ci_ref[...] = xi
    sr_ref[0] = xr
    si_ref[0] = xi
    y_ref[0] = jnp.dot(xs_ref[...].astype(BF16), wc_ref[0], preferred_element_type=F32)


def _s5_scan(u_slab, wb, a_re, a_im, wc, h0r, h0i, *, slabs_per_dir, tt):
    n_slab, rows, _ = u_slab.shape
    nl = wb.shape[-1]
    hl = nl // 2
    steps = rows // SUBLANES
    assert steps % tt == 0
    dmap = lambda s, j: (s // slabs_per_dir, 0, 0)
    smap = lambda s, j: (s, 0, 0)
    return pl.pallas_call(
        functools.partial(_s5_kernel, tt=tt, nl=nl),
        grid=(n_slab, steps // tt),
        in_specs=[pl.BlockSpec((1, tt * SUBLANES, WG), lambda s, j: (s, j, 0)),
                  pl.BlockSpec((1, WG, nl), dmap),
                  pl.BlockSpec((1, SUBLANES, hl), smap),
                  pl.BlockSpec((1, SUBLANES, hl), smap),
                  pl.BlockSpec((1, nl, WG), dmap),
                  pl.BlockSpec((1, SUBLANES, hl), smap),
                  pl.BlockSpec((1, SUBLANES, hl), smap)],
        out_specs=[pl.BlockSpec((1, tt * SUBLANES, WG), lambda s, j: (s, j, 0)),
                   pl.BlockSpec((1, SUBLANES, hl), smap),
                   pl.BlockSpec((1, SUBLANES, hl), smap)],
        out_shape=[jax.ShapeDtypeStruct((n_slab, rows, WG), F32),
                   jax.ShapeDtypeStruct((n_slab, SUBLANES, hl), F32),
                   jax.ShapeDtypeStruct((n_slab, SUBLANES, hl), F32)],
        scratch_shapes=[pltpu.VMEM((tt * SUBLANES, nl), F32), pltpu.VMEM((tt * SUBLANES, nl), F32),
                        pltpu.VMEM((SUBLANES, hl), F32), pltpu.VMEM((SUBLANES, hl), F32)],
        compiler_params=_cparams(("parallel", "arbitrary")),
        name="s5_scan",
    )(u_slab, wb, a_re, a_im, wc, h0r, h0i)


def _s5_params(lam_re, lam_im, log_step, b_ri, c_ri):
    lam = lax.complex(lam_re, lam_im)
    a_bar = jnp.exp(lam * jnp.exp(log_step)[..., None])
    b_bar = ((a_bar - 1.0) / lam)[..., None] * lax.complex(b_ri[..., 0], b_ri[..., 1])
    return jnp.real(a_bar), jnp.imag(a_bar), jnp.real(b_bar), jnp.imag(b_bar), c_ri[..., 0], c_ri[..., 1]


def _s5_weights(a_re, a_im, bb_re, bb_im, c_re, c_im, fold):
    gpr = D_GROUPS // fold
    eye = jnp.eye(gpr, dtype=F32)

    def wb_part(bb):
        t = bb.reshape(2, fold, gpr, D_STATE, D_GROUP)
        src = jnp.transpose(t, (0, 1, 2, 4, 3))
        w = jnp.where(eye[None, None, :, None, :, None] > 0,
                      jnp.broadcast_to(src[:, :, :, :, None, :], (2, fold, gpr, D_GROUP, gpr, D_STATE)), 0.0)
        return w.reshape(2, WG, gpr * D_STATE)

    wb = jnp.concatenate([wb_part(bb_re), wb_part(bb_im)], axis=-1)

    def wc_part(cc):
        t = cc.reshape(2, fold, gpr, D_GROUP, D_STATE)
        src = jnp.transpose(t, (0, 2, 4, 1, 3))
        w = jnp.where(eye[None, :, None, None, :, None] > 0,
                      jnp.broadcast_to(src[:, :, :, :, None, :], (2, gpr, D_STATE, fold, gpr, D_GROUP)), 0.0)
        return w.reshape(2, gpr * D_STATE, WG)

    wc = jnp.concatenate([wc_part(c_re), -wc_part(c_im)], axis=1)

    def a_rows(a):
        return a.reshape(2, fold, gpr * D_STATE)

    return wb.astype(BF16), wc.astype(BF16), a_rows(a_re), a_rows(a_im)


def _outproj_kernel(x_ref, mod_ref, ya_ref, yb_ref, yc_ref, ydf_ref, ydb_ref, du_ref, dd_ref,
                    wglu_ref, wout_ref, g_ref, o_ref):
    mod = mod_ref[0]
    yd = ydf_ref[...] + ydb_ref[...] + dd_ref[...] * du_ref[...]
    yd = yd * (0.5 * (1.0 + jnp.tanh(math.sqrt(2.0 / math.pi) * (yd + 0.044715 * (yd * yd * yd)))))
    gl = jnp.dot(yd.astype(BF16), wglu_ref[...], preferred_element_type=F32)
    yd = gl[:, 0:WG] * _sigmoid(gl[:, WG:2 * WG])
    cat = jnp.concatenate([ya_ref[...].astype(BF16), yb_ref[...].astype(BF16),
                           yc_ref[...].astype(BF16), yd.astype(BF16)], axis=-1)
    y = jnp.dot(cat, wout_ref[...], preferred_element_type=F32)
    o_ref[...] = x_ref[...] + mod[2:3] * _rms(y, g_ref[...])


def _out_proj(x_all, mod, ya, yb, yc, ydf, ydb, du, dd, wglu, wout, g_post1, *, nct, tpl):
    n, d = x_all.shape
    row = lambda i: (i, 0)
    const2 = lambda i: (0, 0)
    mod_map = lambda i: (jnp.where(i < nct, 0, 1 + jnp.maximum(i - nct, 0) // tpl), 0, 0)
    sm = pl.BlockSpec((TM, WG), row)
    return pl.pallas_call(
        _outproj_kernel,
        grid=(n // TM,),
        in_specs=[pl.BlockSpec((TM, d), row), pl.BlockSpec((1, 6, d), mod_map),
                  sm, sm, sm, sm, sm, sm,
                  pl.BlockSpec((1, WG), const2), pl.BlockSpec(wglu.shape, const2),
                  pl.BlockSpec(wout.shape, const2), pl.BlockSpec((1, d), const2)],
        out_specs=pl.BlockSpec((TM, d), row),
        out_shape=jax.ShapeDtypeStruct((n, d), F32),
        compiler_params=_cparams(("parallel",)),
        name="out_proj",
    )(x_all, mod, ya, yb, yc, ydf, ydb, du, dd, wglu, wout, g_post1)


def _ffn_kernel(x_ref, xp_ref, xn_ref, mod_ref, g_ref, wup_ref, cw_ref, cb_ref, wdn_ref, gp_ref,
                o_ref, hext_ref, *, f, fc, nct, tpc, tpl):
    i = pl.program_id(0)
    is_lat = i >= nct
    pos = jnp.where(is_lat, jnp.maximum(i - nct, 0) % tpl, i % tpc)
    nper = jnp.where(is_lat, tpl, tpc)
    mod = mod_ref[0]

    def hfun(x):
        return _rms(x, g_ref[...]) * (1.0 + mod[4:5]) + mod[3:4]

    x = x_ref[...]
    hext_ref[0:SUBLANES, :] = jnp.where(pos > 0, hfun(xp_ref[...]), 0.0)
    hext_ref[SUBLANES:SUBLANES + TM, :] = hfun(x)
    hext_ref[SUBLANES + TM:2 * SUBLANES + TM, :] = jnp.where(pos < nper - 1, hfun(xn_ref[...]), 0.0)
    he = hext_ref[...].astype(BF16)
    rows = TM + 2 * SUBLANES

    def conv(u, c0):
        cw = cw_ref[:, c0:c0 + fc]
        um1 = pltpu.roll(u, 1, 0)[SUBLANES:SUBLANES + TM]
        up1 = pltpu.roll(u, rows - 1, 0)[SUBLANES:SUBLANES + TM]
        return (um1 * cw[0:1] + u[SUBLANES:SUBLANES + TM] * cw[1:2] + up1 * cw[2:3]
                + cb_ref[:, c0:c0 + fc])

    acc = jnp.zeros((TM, x.shape[1]), F32)
    for j in range(f // fc):
        ug = jnp.dot(he, wup_ref[:, j * fc:(j + 1) * fc], preferred_element_type=F32)
        uv = jnp.dot(he, wup_ref[:, f + j * fc:f + (j + 1) * fc], preferred_element_type=F32)
        a = _silu(conv(ug, j * fc)) * conv(uv, f + j * fc)
        acc = acc + jnp.dot(a.astype(BF16), wdn_ref[j * fc:(j + 1) * fc, :], preferred_element_type=F32)
    o_ref[...] = x + mod[5:6] * _rms(acc, gp_ref[...])


def _ffn(x_all, mod, g_pre2, wup, cw, cb, wdn, g_post2, *, nct, tpc, tpl):
    n, d = x_all.shape
    f = wdn.shape[0]
    fc = 256
    assert f % fc == 0
    sub = TM // SUBLANES
    nb8 = n // SUBLANES
    row = lambda i: (i, 0)
    const2 = lambda i: (0, 0)
    mod_map = lambda i: (jnp.where(i < nct, 0, 1 + jnp.maximum(i - nct, 0) // tpl), 0, 0)
    single = dict(pipeline_mode=pl.Buffered(1))
    return pl.pallas_call(
        functools.partial(_ffn_kernel, f=f, fc=fc, nct=nct, tpc=tpc, tpl=tpl),
        grid=(n // TM,),
        in_specs=[pl.BlockSpec((TM, d), row),
                  pl.BlockSpec((SUBLANES, d), lambda i: (jnp.maximum(i * sub - 1, 0), 0)),
                  pl.BlockSpec((SUBLANES, d), lambda i: (jnp.minimum((i + 1) * sub, nb8 - 1), 0)),
                  pl.BlockSpec((1, 6, d), mod_map),
                  pl.BlockSpec((1, d), const2),
                  pl.BlockSpec(wup.shape, const2, **single),
                  pl.BlockSpec(cw.shape, const2),
                  pl.BlockSpec(cb.shape, const2),
                  pl.BlockSpec(wdn.shape, const2, **single),
                  pl.BlockSpec((1, d), const2)],
        out_specs=pl.BlockSpec((TM, d), row),
        out_shape=jax.ShapeDtypeStruct((n, d), F32),
        scratch_shapes=[pltpu.VMEM((TM + 2 * SUBLANES, d), F32)],
        compiler_params=_cparams(("parallel",)),
        name="conv_ffn",
    )(x_all, x_all, x_all, mod, g_pre2, wup, cw, cb, wdn, g_post2)


def _ssd_expand(st):
    b = st.shape[0]
    z = jnp.zeros_like(st)
    hpg = N_HEADS // C_GROUPS
    left = jnp.concatenate([st[:, :hpg], z[:, hpg:]], axis=1)
    right = jnp.concatenate([z[:, :hpg], st[:, hpg:]], axis=1)
    return jnp.concatenate([left, right], axis=-1).reshape(b, WG, 2 * C_STATE)


def _ssd_extract(s):
    b = s.shape[0]
    s = s.reshape(b, N_HEADS, HEAD_DIM, C_GROUPS, C_STATE)
    hpg = N_HEADS // C_GROUPS
    return jnp.stack([s[:, h, :, h // hpg] for h in range(N_HEADS)], axis=1)


def kernel(x_prompt, x_sample, cache_a_k, cache_a_v, cache_b_k, cache_b_v, state_ssd, state_s5,
           c, c_ctx, w_mod, b_mod, g_pre1, g_post1, g_pre2, g_post2, w_in, a_lam, a_subln,
           b_qnorm, b_knorm, c_conv_w, c_conv_b, c_dt_bias, c_a_log, c_d, c_norm,
           d_lam_re, d_lam_im, d_log_step, d_b, d_c, d_d, d_glu, w_out, w_up,
           ffn_conv_w, ffn_conv_b, w_down):
    b1, l1, d = x_prompt.shape
    b2, l2, _ = x_sample.shape
    depth = w_mod.shape[0]
    past = cache_a_k.shape[2]
    n1, n2 = b1 * l1, b2 * l2
    n = n1 + n2
    assert d == 4 * WG and l1 % TM == 0 and l2 % TM == 0 and n1 % l2 == 0
    assert b1 % SUBLANES == 0 and SUBLANES % b2 == 0 and 1 + b2 <= SUBLANES
    nct, tpc, tpl = n1 // TM, l1 // TM, l2 // TM
    fold = SUBLANES // b2
    gpr = D_GROUPS // fold

    x_all = jnp.concatenate([x_prompt.reshape(n1, d), x_sample.reshape(n2, d)], axis=0)
    cond8 = jnp.concatenate([c_ctx[None, :], c, jnp.zeros((SUBLANES - 1 - b2, d), F32)], axis=0)
    mod_all = _modulation(cond8, w_mod, b_mod).reshape(depth, SUBLANES, 6, d)

    rt, ct = _rope_tables(l2)
    ones_blk = jnp.asarray(np.kron(np.eye(N_HEADS, dtype=np.float32),
                                   np.full((HEAD_DIM, HEAD_DIM), 1.0 / HEAD_DIM, np.float32))).astype(BF16)
    tile_h = lambda g: jnp.tile(g, N_HEADS)[None, :]
    rep_kv = lambda t: jnp.repeat(t, 2, axis=-2).reshape(*t.shape[:-2], WG)
    cak = cache_a_k.reshape(b2, depth, past, WG).astype(BF16)
    cav = cache_a_v.reshape(b2, depth, past, WG).astype(BF16)
    cbk = rep_kv(cache_b_k).astype(BF16)
    cbv = rep_kv(cache_b_v).astype(BF16)

    new_ak, new_av, new_bk, new_bv, new_ssd, new_s5 = [], [], [], [], [], []
    for l in range(depth):
        mod = mod_all[l]
        lam_init = 0.8 - 0.6 * math.exp(-0.3 * l)
        w = w_in[l]
        bk_w, bv_w = w[:, 1024:1152], w[:, 1152:1280]
        dup = lambda t: jnp.concatenate([t[:, :64], t[:, :64], t[:, 64:], t[:, 64:]], axis=1)
        w_in_p = jnp.concatenate(
            [w[:, 0:1024], dup(bk_w), dup(bv_w), w[:, 1280:2048], w[:, 2056:2312],
             w[:, 2048:2056], jnp.zeros((d, 120), F32)], axis=1).astype(BF16)

        qa, ka, va, qb, kb, vb, ka32, va32, kb32, vb32, cz, cx, du, cdt = _in_proj(
            x_all, mod, g_pre1[l][None, :], w_in_p, tile_h(b_qnorm[l]), tile_h(b_knorm[l]),
            ones_blk, rt, ct, nct=nct, tpl=tpl)

        attn = functools.partial(_attention, tq=TM, ck=1024)
        lam_p, sub_p = a_lam[l], tile_h(a_subln[l])
        ya = attn(qa, ka, va, None, diff=True, n_seq=b1, seq_len=l1, row_off=0,
                  lam=lam_p, subln=sub_p, lam_init=lam_init)
        ya = attn(qa, ka, va, ya, diff=True, n_seq=b2, seq_len=l2, row_off=n1, ctx=(cak, cav), layer=l,
                  lam=lam_p, subln=sub_p, lam_init=lam_init)
        yb = attn(qb, kb, vb, None, diff=False, n_seq=b1, seq_len=l1, row_off=0)
        yb = attn(qb, kb, vb, yb, diff=False, n_seq=b2, seq_len=l2, row_off=n1, ctx=(cbk, cbv), layer=l)

        h0_lat = jnp.stack([_ssd_expand(state_ssd[:, l, dd_]) for dd_ in range(2)], axis=0)
        h0 = jnp.concatenate([jnp.zeros((2, b1, WG, 2 * C_STATE), F32), h0_lat], axis=1)
        pad8 = lambda t: jnp.concatenate([t, jnp.zeros((2, 128 - 2 * N_HEADS), F32)], axis=1)
        dtb = pad8(jnp.stack([jnp.concatenate([c_dt_bias[l, 0], jnp.zeros((N_HEADS,), F32)]),
                              jnp.concatenate([jnp.zeros((N_HEADS,), F32), c_dt_bias[l, 1]])]))
        a_neg = -jnp.exp(c_a_log[l])
        av = pad8(jnp.stack([jnp.concatenate([a_neg[0], jnp.zeros((N_HEADS,), F32)]),
                             jnp.concatenate([jnp.zeros((N_HEADS,), F32), a_neg[1]])]))
        yc, st_f, st_b = _ssd(cx, cdt, cz, h0, c_conv_w[l], c_conv_b[l][None, :],
                              dtb[:, None, :], av[:, None, :],
                              jnp.repeat(c_d[l], HEAD_DIM)[None, :], c_norm[l][None, :],
                              ncc=n1 // C_CHUNK, cpc=l1 // C_CHUNK, cpl=l2 // C_CHUNK, n_ctx_seq=b1)
        new_ssd.append(jnp.stack([_ssd_extract(st_f[:b1]), _ssd_extract(st_b[:b1])], axis=1))

        pr = _s5_params(d_lam_re[l], d_lam_im[l], d_log_step[l], d_b[l], d_c[l])
        u1 = du[:n1].reshape(b1, l1, WG)
        u2 = du[n1:].reshape(b2, l2, WG)
        wb1, wc1, ar1, ai1 = _s5_weights(*pr, 1)
        nsl = b1 // SUBLANES

        def slabs1(u):
            return jnp.transpose(u.reshape(nsl, SUBLANES, l1, WG), (0, 2, 1, 3)).reshape(nsl, l1 * SUBLANES, WG)

        us1 = jnp.concatenate([slabs1(u1), slabs1(u1[:, ::-1])], axis=0).astype(BF16)
        arow = lambda a: jnp.broadcast_to(a[:, None, :, :], (2, nsl, 1, a.shape[-1])) * jnp.ones((1, 1, SUBLANES, 1), F32)
        zero1 = jnp.zeros((2 * nsl, SUBLANES, S5_LANES // 2), F32)
        y1, s1r, s1i = _s5_scan(us1, wb1, arow(ar1).reshape(2 * nsl, SUBLANES, -1),
                                arow(ai1).reshape(2 * nsl, SUBLANES, -1), wc1, zero1, zero1,
                                slabs_per_dir=nsl, tt=min(l1, 128))
        y1 = jnp.transpose(y1.reshape(2, nsl, l1, SUBLANES, WG), (0, 1, 3, 2, 4)).reshape(2, b1, l1, WG)
        st1 = jnp.stack([s1r, s1i], axis=-1).reshape(2, b1, D_GROUPS, D_STATE, 2)
        new_s5.append(jnp.transpose(st1, (1, 0, 2, 3, 4)))
        wb2, wc2, ar2, ai2 = _s5_weights(*pr, fold)
        qmask = (jnp.arange(WG)[None, :] // (gpr * D_GROUP) == jnp.arange(fold)[:, None]).astype(F32)

        def slab2(u):
            t = jnp.transpose(u, (1, 0, 2))[:, :, None, :] * qmask[None, None, :, :]
            return t.reshape(l2 * SUBLANES, WG)

        us2 = jnp.stack([slab2(u2), slab2(u2[:, ::-1])], axis=0).astype(BF16)
        a2 = lambda a: jnp.broadcast_to(a[:, None, :, :], (2, b2, fold, a.shape[-1])).reshape(2, SUBLANES, -1)
        h0s = state_s5[:, l].reshape(b2, 2, fold, gpr * D_STATE, 2)
        h0s = jnp.transpose(h0s, (1, 0, 2, 3, 4)).reshape(2, SUBLANES, gpr * D_STATE, 2)
        y2, _, _ = _s5_scan(us2, wb2, a2(ar2), a2(ai2), wc2, h0s[..., 0], h0s[..., 1],
                            slabs_per_dir=1, tt=min(l2, 512))
        y2 = y2.reshape(2, l2, b2, fold, fold, WG // fold)
        y2 = jnp.concatenate([y2[:, :, :, qq, qq, :] for qq in range(fold)], axis=-1)
        y2 = jnp.transpose(y2, (0, 2, 1, 3))
        ydf = jnp.concatenate([y1[0].reshape(n1, WG), y2[0].reshape(n2, WG)], axis=0)
        ydb = jnp.concatenate([y1[1, :, ::-1].reshape(n1, WG), y2[1, :, ::-1].reshape(n2, WG)], axis=0)

        x_all = _out_proj(x_all, mod, ya, yb, yc, ydf, ydb, du, d_d[l][None, :],
                          d_glu[l].astype(BF16), w_out[l].astype(BF16), g_post1[l][None, :], nct=nct, tpl=tpl)
        x_all = _ffn(x_all, mod, g_pre2[l][None, :], w_up[l].astype(BF16), ffn_conv_w[l],
                     ffn_conv_b[l][None, :], w_down[l].astype(BF16), g_post2[l][None, :],
                     nct=nct, tpc=tpc, tpl=tpl)

        new_ak.append(ka32[:n1].reshape(b1, l1, N_HEADS, HEAD_DIM))
        new_av.append(va32[:n1].reshape(b1, l1, N_HEADS, HEAD_DIM))
        pick = lambda t: t[:n1].reshape(b1, l1, N_HEADS, HEAD_DIM)[:, :, ::2, :]
        new_bk.append(pick(kb32))
        new_bv.append(pick(vb32))

    y_prompt = x_all[:n1].reshape(b1, l1, d)
    y_sample = x_all[n1:].reshape(b2, l2, d)
    st = lambda xs: jnp.stack(xs, axis=1)
    return (y_prompt, y_sample, st(new_ak), st(new_av), st(new_bk), st(new_bv), st(new_ssd), st(new_s5))
```

```python
import functools
import math

import numpy as np
import jax
import jax.numpy as jnp
from jax import lax
from jax.experimental import pallas as pl
from jax.experimental.pallas import tpu as pltpu

F32 = jnp.float32
BF16 = jnp.bfloat16

EPS = 1e-6
LOG2E = math.log2(math.e)
GRID_W = 64
ROPE_BASE = 10000.0
HEAD_DIM = 64
A_HALF = HEAD_DIM // 2
N_HEADS = 4
WG = N_HEADS * HEAD_DIM
C_GROUPS = 2
C_STATE = 64
C_CHUNK = 128
D_GROUP = 16
D_GROUPS = WG // D_GROUP
D_STATE = 64
S5_LANES = D_GROUPS * D_STATE * 2

TM = 256
SUBLANES = 8
VMEM_LIMIT = 56 * 1024 * 1024

_C_AQ, _C_AK, _C_AV, _C_BQ, _C_BK, _C_BV, _C_CZ, _C_CX, _C_DU, _C_DT, _C_END = (
    0, 256, 512, 768, 1024, 1280, 1536, 1792, 2304, 2560, 2688)


def _sigmoid(x):
    return 1.0 / (1.0 + jnp.exp(-x))


def _silu(x):
    return x * _sigmoid(x)


def _cparams(sem):
    return pltpu.CompilerParams(dimension_semantics=sem, vmem_limit_bytes=VMEM_LIMIT)


def _lane_mask(width, lo, hi):
    lane = lax.broadcasted_iota(jnp.int32, (1, width), 1)
    return (lane >= lo) & (lane < hi)


def _rms(x, g):
    return (x * lax.rsqrt(jnp.mean(x * x, axis=-1, keepdims=True) + EPS)) * g


def _mod_row(nct, tpl):
    return lambda i: jnp.where(i < nct, 0, 1 + jnp.maximum(i - nct, 0) // tpl)


def _mod_kernel(c_ref, w_ref, b_ref, o_ref):
    s = _silu(c_ref[...])
    o_ref[0] = jnp.dot(s.astype(BF16), w_ref[0].astype(BF16), preferred_element_type=F32) + b_ref[0]


def _modulation(cond8, w_mod, b_mod):
    depth, d, n = w_mod.shape
    tn = 1536
    return pl.pallas_call(
        _mod_kernel,
        grid=(depth, n // tn),
        in_specs=[pl.BlockSpec((SUBLANES, d), lambda l, j: (0, 0)),
                  pl.BlockSpec((1, d, tn), lambda l, j: (l, 0, j)),
                  pl.BlockSpec((1, 1, tn), lambda l, j: (l, 0, j))],
        out_specs=pl.BlockSpec((1, SUBLANES, tn), lambda l, j: (l, 0, j)),
        out_shape=jax.ShapeDtypeStruct((depth, SUBLANES, n), F32),
        compiler_params=_cparams(("parallel", "parallel")),
        name="modulation",
    )(cond8, w_mod, b_mod.reshape(depth, 1, n))


def _inproj_kernel(x_ref, mod_ref, g_ref, w_ref, qn_ref, kn_ref, ones_ref, rt_ref, ct_ref,
                   qa_ref, ka_ref, va_ref, qb_ref, kb_ref, vb_ref,
                   ka32_ref, va32_ref, kb32_ref, vb32_ref, cz_ref, cx_ref, du_ref, dt_ref,
                   *, nct):
    i = pl.program_id(0)
    mod = mod_ref[0, 0]
    h = _rms(x_ref[...], g_ref[0]) * (1.0 + mod[1:2]) + mod[0:1]
    p = jnp.dot(h.astype(BF16), w_ref[0], preferred_element_type=F32)
    va_ref[...] = p[:, _C_AV:_C_BQ].astype(BF16)
    vb_ref[...] = p[:, _C_BV:_C_CZ].astype(BF16)
    va32_ref[...] = p[:, _C_AV:_C_BQ]
    vb32_ref[...] = p[:, _C_BV:_C_CZ]
    cz_ref[...] = p[:, _C_CZ:_C_CX]
    cx_ref[...] = p[:, _C_CX:_C_DU]
    du_ref[...] = p[:, _C_DU:_C_DT]
    dt_ref[...] = p[:, _C_DT:_C_END]

    def headnorm(t, gain):
        sq = t * t
        hi = sq.astype(BF16)
        lo = (sq - hi.astype(F32)).astype(BF16)
        ms = (jnp.dot(hi, ones_ref[...], preferred_element_type=F32)
              + jnp.dot(lo, ones_ref[...], preferred_element_type=F32))
        return (t * lax.rsqrt(ms + EPS)) * gain

    aq = p[:, _C_AQ:_C_AK] * (A_HALF ** -0.5 * LOG2E)
    ak = p[:, _C_AK:_C_AV]
    bq = headnorm(p[:, _C_BQ:_C_BK], qn_ref[0]) * (HEAD_DIM ** -0.5 * LOG2E)
    bk = headnorm(p[:, _C_BK:_C_BV], kn_ref[0])
    ka32_ref[...] = ak
    kb32_ref[...] = bk

    @pl.when(i < nct)
    def _():
        qa_ref[...] = aq.astype(BF16)
        ka_ref[...] = ak.astype(BF16)
        qb_ref[...] = bq.astype(BF16)
        kb_ref[...] = bk.astype(BF16)

    @pl.when(i >= nct)
    def _():
        rows = TM // GRID_W
        rt = rt_ref[0]
        lane = lax.broadcasted_iota(jnp.int32, (1, WG), 1)
        row_lane_a = (lane % (A_HALF // 2)) < A_HALF // 4
        row_lane_b = (lane % (HEAD_DIM // 2)) < HEAD_DIM // 4

        def table(k, row_lane):
            parts = [jnp.where(row_lane, rt[k, r:r + 1, :], ct_ref[k]) for r in range(rows)]
            return jnp.concatenate(parts, axis=0)

        def rope(t, base, row_lane, dist):
            return (t * table(base, row_lane)
                    + pltpu.roll(t, WG - dist, 1) * table(base + 1, row_lane)
                    + pltpu.roll(t, dist, 1) * table(base + 2, row_lane))

        qa_ref[...] = rope(aq, 0, row_lane_a, A_HALF // 2).astype(BF16)
        ka_ref[...] = rope(ak, 0, row_lane_a, A_HALF // 2).astype(BF16)
        qb_ref[...] = rope(bq, 3, row_lane_b, HEAD_DIM // 2).astype(BF16)
        kb_ref[...] = rope(bk, 3, row_lane_b, HEAD_DIM // 2).astype(BF16)


def _rope_tables(l2):
    rows = l2 // GRID_W
    lane = np.arange(WG)

    def one(block, n):
        freqs = (np.float32(ROPE_BASE) ** (-np.arange(n, dtype=np.float32) / np.float32(n))).astype(np.float32)
        p = lane % block
        idx = p % (block // 2)
        first = p < (block // 2)
        f = freqs[idx % n]

        def tabs(pos):
            ang = (pos[:, None].astype(np.float32) * f[None, :]).astype(np.float32)
            c, s = np.cos(ang).astype(np.float32), np.sin(ang).astype(np.float32)
            return [c, np.where(first[None, :], -s, 0.0).astype(np.float32),
                    np.where(first[None, :], 0.0, s).astype(np.float32)]

        return tabs(np.arange(rows)), tabs(np.arange(GRID_W))

    ra, ca = one(A_HALF, A_HALF // 4)
    rb, cb = one(HEAD_DIM, HEAD_DIM // 4)
    rt = np.stack(ra + rb, axis=0)
    ct = np.stack(ca + cb, axis=0)
    rpt = TM // GRID_W
    npt = l2 // TM
    rt = rt.reshape(6, npt, rpt, WG).transpose(1, 0, 2, 3)
    rt = np.concatenate([rt, np.zeros((npt, 6, SUBLANES - rpt, WG), np.float32)], axis=2)
    return jnp.asarray(rt), jnp.asarray(ct)


def _in_proj(x_all, mod_all, g_pre1, w_in_p, qn, kn, ones_blk, rt, ct, *, layer, nct, tpl):
    n, d = x_all.shape
    nt = n // TM
    row = lambda i: (i, 0)
    const2 = lambda i: (0, 0)
    lay3 = lambda i: (layer, 0, 0)
    mrow = _mod_row(nct, tpl)
    ctx_row = lambda i: (jnp.minimum(i, nct), 0)
    n_ctx = (nct + 1) * TM
    out_specs = ([pl.BlockSpec((TM, WG), row)] * 6 + [pl.BlockSpec((TM, WG), ctx_row)] * 4
                 + [pl.BlockSpec((TM, w), row) for w in (WG, 2 * WG, WG, 128)])
    out_shape = ([jax.ShapeDtypeStruct((n, WG), BF16)] * 6 + [jax.ShapeDtypeStruct((n_ctx, WG), F32)] * 4
                 + [jax.ShapeDtypeStruct((n, w), F32) for w in (WG, 2 * WG, WG, 128)])
    return pl.pallas_call(
        functools.partial(_inproj_kernel, nct=nct),
        grid=(nt,),
        in_specs=[pl.BlockSpec((TM, d), row),
                  pl.BlockSpec((1, 1, 6, d), lambda i: (layer, mrow(i), 0, 0)),
                  pl.BlockSpec((1, 1, d), lay3),
                  pl.BlockSpec((1,) + w_in_p.shape[1:], lay3),
                  pl.BlockSpec((1, 1, WG), lay3),
                  pl.BlockSpec((1, 1, WG), lay3),
                  pl.BlockSpec((WG, WG), const2),
                  pl.BlockSpec((1, 6, SUBLANES, WG), lambda i: (jnp.maximum(i - nct, 0) % tpl, 0, 0, 0)),
                  pl.BlockSpec(ct.shape, lambda i: (0, 0, 0))],
        out_specs=out_specs,
        out_shape=out_shape,
        compiler_params=_cparams(("arbitrary",)),
        name="in_proj",
    )(x_all, mod_all, g_pre1, w_in_p, qn, kn, ones_blk, rt, ct)


def _attn_kernel(*refs, diff, tq, lk_new, lc, ck, lam_init):
    it = iter(refs)
    q_ref, k_ref, v_ref = next(it), next(it), next(it)
    kc_ref = vc_ref = None
    if lc:
        kc_ref, vc_ref = next(it).at[0, 0], next(it).at[0, 0]
    lam_ref = sub_ref = None
    if diff:
        lam_ref, sub_ref = next(it), next(it)
    prev_ref = next(it)
    o_ref = next(it)
    s_ref = (next(it), next(it))
    m_ref = (next(it), next(it))
    oall_ref = next(it)
    del prev_ref

    blocks = [(k_ref, v_ref, r0, min(ck, lk_new - r0), r0) for r0 in range(0, lk_new, ck)]
    if lc:
        blocks.append((kc_ref, vc_ref, 0, lc, lk_new))

    n_streams = 2 * N_HEADS if diff else N_HEADS
    width = A_HALF if diff else HEAD_DIM
    q = q_ref[...]
    lane = lax.broadcasted_iota(jnp.int32, (1, WG), 1)

    def step(i, slot, scores, values):
        if scores:
            lo = i * width
            qm = jnp.where((lane >= lo) & (lane < lo + width), q, jnp.zeros_like(q))
            s_cur = s_ref[slot]
            mrun = jnp.full((tq, 128), -jnp.inf, F32)
        if values:
            s_prev = s_ref[1 - slot]
            m = jnp.max(m_ref[1 - slot][...], axis=-1, keepdims=True)
            lrun = jnp.zeros((tq, 128), F32)
            acc = jnp.zeros((tq, WG), F32)
        for kr, vr, r0, rows, c0 in blocks:
            if scores:
                s = lax.dot_general(qm, kr[r0:r0 + rows, :], (((1,), (1,)), ((), ())),
                                    preferred_element_type=F32)
                s_cur[:, c0:c0 + rows] = s
                for t in range(rows // 128):
                    mrun = jnp.maximum(mrun, s[:, t * 128:(t + 1) * 128])
            if values:
                p = jnp.exp2(s_prev[:, c0:c0 + rows] - m)
                for t in range(rows // 128):
                    lrun = lrun + p[:, t * 128:(t + 1) * 128]
                acc = acc + jnp.dot(p.astype(BF16), vr[r0:r0 + rows, :], preferred_element_type=F32)
        if scores:
            m_ref[slot][...] = mrun
        if values:
            oall_ref[i - 1] = acc * (1.0 / jnp.sum(lrun, axis=-1, keepdims=True))

    step(0, 0, True, False)

    def body(g, carry):
        step(2 * g + 1, 1, True, True)
        step(2 * g + 2, 0, True, True)
        return carry

    lax.fori_loop(0, (n_streams - 2) // 2, body, 0)
    step(n_streams - 1, 1, True, True)
    step(n_streams, 0, False, True)

    out = jnp.zeros((tq, WG), F32)
    if diff:
        lp = lam_ref[0]
        lam = (jnp.exp(jnp.sum(lp[0:1] * lp[1:2], axis=-1, keepdims=True))
               - jnp.exp(jnp.sum(lp[2:3] * lp[3:4], axis=-1, keepdims=True)) + lam_init)
    for h in range(N_HEADS):
        hm = _lane_mask(WG, h * HEAD_DIM, (h + 1) * HEAD_DIM)
        if diff:
            o = oall_ref[2 * h] - lam * oall_ref[2 * h + 1]
            ms = jnp.sum(jnp.where(hm, o * o, 0.0), axis=-1, keepdims=True) * (1.0 / HEAD_DIM)
            o = ((o * lax.rsqrt(ms + EPS)) * sub_ref[0]) * (1.0 - lam_init)
        else:
            o = oall_ref[h]
        out = jnp.where(hm, o, out)
    o_ref[...] = out


def _attention(q, k, v, prev, *, diff, n_seq, seq_len, row_off, tq, ck, layer, ctx=None,
               lam=None, subln=None, lam_init=0.0):
    n = q.shape[0]
    assert row_off % seq_len == 0 and seq_len % tq == 0
    qpb = seq_len // tq
    q_map = lambda s, j: (row_off // tq + s * qpb + j, 0)
    kv_map = lambda s, j: (row_off // seq_len + s, 0)
    in_specs = [pl.BlockSpec((tq, WG), q_map), pl.BlockSpec((seq_len, WG), kv_map),
                pl.BlockSpec((seq_len, WG), kv_map)]
    args = [q, k, v]
    lc = 0
    if ctx is not None:
        kc, vc = ctx
        lc = kc.shape[2]
        cmap = lambda s, j: (s, layer, 0, 0)
        in_specs += [pl.BlockSpec((1, 1, lc, WG), cmap), pl.BlockSpec((1, 1, lc, WG), cmap)]
        args += [kc, vc]
    if diff:
        lay3 = lambda s, j: (layer, 0, 0)
        in_specs += [pl.BlockSpec((1,) + lam.shape[1:], lay3), pl.BlockSpec((1, 1, WG), lay3)]
        args += [lam, subln]
    aliases = {}
    if prev is None:
        prev = jnp.zeros((SUBLANES, 128), F32)
    else:
        aliases = {len(args): 0}
    in_specs.append(pl.BlockSpec(memory_space=pl.ANY))
    args.append(prev)
    lk = seq_len + lc
    return pl.pallas_call(
        functools.partial(_attn_kernel, diff=diff, tq=tq, lk_new=seq_len, lc=lc, ck=min(ck, seq_len),
                          lam_init=lam_init),
        grid=(n_seq, qpb),
        in_specs=in_specs,
        out_specs=pl.BlockSpec((tq, WG), q_map),
        out_shape=jax.ShapeDtypeStruct((n, WG), F32),
        scratch_shapes=[pltpu.VMEM((tq, lk), F32), pltpu.VMEM((tq, lk), F32),
                        pltpu.VMEM((tq, 128), F32), pltpu.VMEM((tq, 128), F32),
                        pltpu.VMEM((2 * N_HEADS if diff else N_HEADS, tq, WG), F32)],
        input_output_aliases=aliases,
        compiler_params=_cparams(("parallel", "parallel")),
        name="attn_diff" if diff else "attn_gqa",
    )(*args)


def _ssd_chunk(x_ref, xp_ref, xn_ref, dt_ref, h0_ref, cw_ref, cb_ref, dtb_ref, av_ref, s_ref,
               c, *, reverse, ncc, cpc, cpl):
    q = C_CHUNK
    is_lat = c >= ncc
    pos = jnp.where(is_lat, jnp.maximum(c - ncc, 0) % cpl, c % cpc)
    nper = jnp.where(is_lat, cpl, cpc)
    enter = (pos == nper - 1) if reverse else (pos == 0)

    @pl.when(enter)
    def _():
        s_ref[...] = h0_ref[0, 0]

    x = x_ref[...]
    rid = lax.broadcasted_iota(jnp.int32, (q, 1), 0)
    prow = jnp.where(pos > 0, xp_ref[SUBLANES - 1:SUBLANES, :], 0.0)
    nrow = jnp.where(pos < nper - 1, xn_ref[0:1, :], 0.0)
    xm1 = jnp.where(rid == 0, prow, pltpu.roll(x, 1, 0))
    xp1 = jnp.where(rid == q - 1, nrow, pltpu.roll(x, q - 1, 0))
    cw = cw_ref[0]
    xbc = _silu(xm1 * cw[0:1] + x * cw[1:2] + xp1 * cw[2:3] + cb_ref[0])
    xs = xbc[:, 0:WG]
    bm = xbc[:, WG:WG + 128].astype(BF16)
    cm = xbc[:, WG + 128:WG + 256]

    raw = dt_ref[...] + dtb_ref[...]
    dt = jnp.maximum(raw, 0.0) + jnp.log1p(jnp.exp(-jnp.abs(raw)))
    dta = dt * av_ref[...]
    li = lax.broadcasted_iota(jnp.int32, (q, q), 0)
    si = lax.broadcasted_iota(jnp.int32, (q, q), 1)
    causal = (si >= li) if reverse else (si <= li)
    cum = jnp.dot(causal.astype(F32), dta, preferred_element_type=F32, precision=lax.Precision.HIGHEST)
    cum_t = cum.T
    end = 0 if reverse else q - 1
    d0 = N_HEADS if reverse else 0

    s_in = s_ref[...]
    rowh = lax.broadcasted_iota(jnp.int32, (WG, 1), 0) // HEAD_DIM
    colg = lax.broadcasted_iota(jnp.int32, (1, 128), 1) // C_STATE
    blk = (rowh // (N_HEADS // C_GROUPS)) == colg

    y = jnp.zeros((q, WG), F32)
    xw = jnp.zeros((q, WG), F32)
    ecum = jnp.zeros((q, WG), F32)
    cdec = jnp.zeros((WG, 1), F32)
    gmat = None
    for h in range(N_HEADS):
        j = d0 + h
        g = h // (N_HEADS // C_GROUPS)
        if h % (N_HEADS // C_GROUPS) == 0:
            cg = jnp.where(_lane_mask(128, g * C_STATE, (g + 1) * C_STATE), cm, 0.0).astype(BF16)
            gmat = lax.dot_general(cg, bm, (((1,), (1,)), ((), ())), preferred_element_type=F32)
        col = cum[:, j:j + 1]
        seg = col - cum_t[j:j + 1, :]
        decay = jnp.where(causal, jnp.exp(jnp.where(causal, seg, 0.0)), 0.0)
        hm = _lane_mask(WG, h * HEAD_DIM, (h + 1) * HEAD_DIM)
        xdt = jnp.where(hm, xs * dt[:, j:j + 1], 0.0)
        y = y + jnp.dot((gmat * decay).astype(BF16), xdt.astype(BF16), preferred_element_type=F32)
        cend = cum[end:end + 1, j:j + 1]
        xw = xw + xdt * jnp.exp(cend - col)
        ecum = jnp.where(hm, jnp.exp(col), ecum)
        cdec = jnp.where(rowh == h, jnp.exp(cend), cdec)

    s_msk = jnp.where(blk, s_in, 0.0).astype(BF16)
    y = y + lax.dot_general(cm.astype(BF16), s_msk, (((1,), (1,)), ((), ())), preferred_element_type=F32) * ecum
    st = lax.dot_general(xw.astype(BF16), bm, (((0,), (0,)), ((), ())), preferred_element_type=F32)
    s_new = s_in * cdec + st
    s_ref[...] = s_new
    return y, xs, s_new


def _ssd_kernel(xf_ref, xpf_ref, xnf_ref, dtf_ref, h0f_ref, xb_ref, xpb_ref, xnb_ref, dtr_ref, h0b_ref,
                cw_ref, cb_ref, dtb_ref, av_ref, dsk_ref,
                yf_ref, yb_ref, stf_ref, stb_ref, sf_ref, sb_ref, *, ncc, cpc, cpl):
    t = pl.program_id(0)
    kw = dict(ncc=ncc, cpc=cpc, cpl=cpl)
    y, xs, s_new = _ssd_chunk(xf_ref, xpf_ref, xnf_ref, dtf_ref, h0f_ref, cw_ref, cb_ref,
                              dtb_ref.at[0, 0], av_ref.at[0, 0], sf_ref, t, reverse=False, **kw)
    yf_ref[...] = y + dsk_ref[0] * xs
    stf_ref[0] = s_new
    y, xs, s_new = _ssd_chunk(xb_ref, xpb_ref, xnb_ref, dtr_ref, h0b_ref, cw_ref, cb_ref,
                              dtb_ref.at[0, 1], av_ref.at[0, 1], sb_ref, pl.num_programs(0) - 1 - t,
                              reverse=True, **kw)
    yb_ref[...] = y
    stb_ref[0] = s_new


def _ssd(cx, cdt, h0, conv_w, conv_b, dtb, av, dsk, *, layer, ncc, cpc, cpl, n_ctx_seq):
    n = cx.shape[0]
    nc = n // C_CHUNK
    n_seq = h0.shape[1]
    sub = C_CHUNK // SUBLANES
    nb8 = n // SUBLANES
    lay3 = lambda t: (layer, 0, 0)

    def specs(cidx, d):
        def seq_of(t):
            c = cidx(t)
            return jnp.where(c < ncc, c // cpc, n_ctx_seq + jnp.maximum(c - ncc, 0) // cpl)

        row = lambda t: (cidx(t), 0)
        ins = [pl.BlockSpec((C_CHUNK, 2 * WG), row),
               pl.BlockSpec((SUBLANES, 2 * WG), lambda t: (jnp.maximum(cidx(t) * sub - 1, 0), 0)),
               pl.BlockSpec((SUBLANES, 2 * WG), lambda t: (jnp.minimum((cidx(t) + 1) * sub, nb8 - 1), 0)),
               pl.BlockSpec((C_CHUNK, 128), row),
               pl.BlockSpec((1, 1, WG, 128), lambda t: (d, seq_of(t), 0, 0))]
        outs = [pl.BlockSpec((C_CHUNK, WG), row), pl.BlockSpec((1, WG, 128), lambda t: (seq_of(t), 0, 0))]
        return ins, outs

    ins_f, outs_f = specs(lambda t: t, 0)
    ins_b, outs_b = specs(lambda t: nc - 1 - t, 1)
    return pl.pallas_call(
        functools.partial(_ssd_kernel, ncc=ncc, cpc=cpc, cpl=cpl),
        grid=(nc,),
        in_specs=ins_f + ins_b + [pl.BlockSpec((1, 3, 2 * WG), lay3),
                                  pl.BlockSpec((1, 1, 2 * WG), lay3),
                                  pl.BlockSpec((1, 2, 1, 128), lambda t: (layer, 0, 0, 0)),
                                  pl.BlockSpec((1, 2, 1, 128), lambda t: (layer, 0, 0, 0)),
                                  pl.BlockSpec((1, 1, WG), lay3)],
        out_specs=[outs_f[0], outs_b[0], outs_f[1], outs_b[1]],
        out_shape=[jax.ShapeDtypeStruct((n, WG), F32), jax.ShapeDtypeStruct((n, WG), F32),
                   jax.ShapeDtypeStruct((n_seq, WG, 128), F32), jax.ShapeDtypeStruct((n_seq, WG, 128), F32)],
        scratch_shapes=[pltpu.VMEM((WG, 128), F32), pltpu.VMEM((WG, 128), F32)],
        compiler_params=_cparams(("arbitrary",)),
        name="ssd",
    )(cx, cx, cx, cdt, h0, cx, cx, cx, cdt, h0, conv_w, conv_b, dtb, av, dsk)


def _s5_kernel(*refs, tt, nl, nb_in, fold):
    u_refs = refs[:nb_in]
    (wb_ref, are_ref, aim_ref, wc_ref, h0r_ref, h0i_ref, y_ref, sr_ref, si_ref,
     ust_ref, ytm_ref, bu_ref, xs_ref, cr_ref, ci_ref) = refs[nb_in:]
    hl = nl // 2
    d = pl.program_id(0)

    @pl.when(pl.program_id(2) == 0)
    def _():
        cr_ref[...] = h0r_ref[0, 0]
        ci_ref[...] = h0i_ref[0, 0]

    lane = lax.broadcasted_iota(jnp.int32, (1, WG), 1)
    gw = WG // fold
    for qq in range(fold):
        for b in range(nb_in):
            c = qq * nb_in + b
            u = u_refs[b][...]
            if fold > 1:
                u = jnp.where((lane >= qq * gw) & (lane < (qq + 1) * gw), u, 0.0)
            for k in range(WG // 128):
                ust_ref[k, pl.ds(c, tt, stride=SUBLANES), :] = u[:, k * 128:(k + 1) * 128]
    ust = jnp.concatenate([ust_ref[k].astype(BF16) for k in range(WG // 128)], axis=1)
    bu_ref[...] = jnp.dot(ust, wb_ref[0, 0], preferred_element_type=F32)
    a_re = are_ref[0, 0]
    a_im = aim_ref[0, 0]

    def step(t, carry):
        xr, xi = carry
        te = t + d * (tt - 1 - 2 * t)
        r0 = pl.multiple_of(te * SUBLANES, SUBLANES)
        nr = a_re * xr - a_im * xi + bu_ref[pl.ds(r0, SUBLANES), 0:hl]
        ni = a_re * xi + a_im * xr + bu_ref[pl.ds(r0, SUBLANES), hl:nl]
        xs_ref[pl.ds(r0, SUBLANES), 0:hl] = nr
        xs_ref[pl.ds(r0, SUBLANES), hl:nl] = ni
        return nr, ni

    xr, xi = lax.fori_loop(0, tt, step, (cr_ref[...], ci_ref[...]))
    cr_ref[...] = xr
    ci_ref[...] = xi
    sr_ref[0, 0] = xr
    si_ref[0, 0] = xi
    y = jnp.dot(xs_ref[...].astype(BF16), wc_ref[0, 0], preferred_element_type=F32)
    for k in range(WG // 128):
        ytm_ref[k] = y[:, k * 128:(k + 1) * 128]

    def chain_rows(c):
        return jnp.concatenate([ytm_ref[k, pl.ds(c, tt, stride=SUBLANES), :] for k in range(WG // 128)], axis=1)

    for b in range(nb_in):
        if fold == 1:
            y_ref[0, b] = chain_rows(b)
        else:
            acc = jnp.zeros((tt, WG), F32)
            for qq in range(fold):
                acc = jnp.where((lane >= qq * gw) & (lane < (qq + 1) * gw), chain_rows(qq * nb_in + b), acc)
            y_ref[0, b] = acc


def _s5_scan(du, wb, a_re, a_im, wc, h0r, h0i, *, layer, n_b, seq_len, row_off, tt, fold):
    nb_in = SUBLANES // fold
    n_slab = n_b // nb_in
    nl = wb.shape[-1]
    hl = nl // 2
    nblk = seq_len // tt
    assert seq_len % tt == 0 and row_off % tt == 0
    tblk = lambda d, j: j + d * (nblk - 1 - 2 * j)
    u_specs = [pl.BlockSpec((tt, WG), functools.partial(
        lambda d, s, j, b: (row_off // tt + (s * nb_in + b) * nblk + tblk(d, j), 0), b=b)) for b in range(nb_in)]
    wmap = lambda d, s, j: (layer, d, 0, 0)
    smap = lambda d, s, j: (d, s, 0, 0)
    st_spec = pl.BlockSpec((1, 1, SUBLANES, hl), smap)
    rows = SUBLANES * tt
    return pl.pallas_call(
        functools.partial(_s5_kernel, tt=tt, nl=nl, nb_in=nb_in, fold=fold),
        grid=(2, n_slab, nblk),
        in_specs=u_specs + [pl.BlockSpec((1, 1, WG, nl), wmap),
                            pl.BlockSpec((1, 1, SUBLANES, hl), wmap),
                            pl.BlockSpec((1, 1, SUBLANES, hl), wmap),
                            pl.BlockSpec((1, 1, nl, WG), wmap),
                            st_spec, st_spec],
        out_specs=[pl.BlockSpec((1, nb_in, tt, WG), lambda d, s, j: (d, s, tblk(d, j), 0)), st_spec, st_spec],
        out_shape=[jax.ShapeDtypeStruct((2, n_b, seq_len, WG), F32),
                   jax.ShapeDtypeStruct((2, n_slab, SUBLANES, hl), F32),
                   jax.ShapeDtypeStruct((2, n_slab, SUBLANES, hl), F32)],
        scratch_shapes=[pltpu.VMEM((WG // 128, rows, 128), F32), pltpu.VMEM((WG // 128, rows, 128), F32),
                        pltpu.VMEM((rows, nl), F32), pltpu.VMEM((rows, nl), F32),
                        pltpu.VMEM((SUBLANES, hl), F32), pltpu.VMEM((SUBLANES, hl), F32)],
        compiler_params=_cparams(("parallel", "parallel", "arbitrary")),
        name="s5_scan",
    )(*([du] * nb_in), wb, a_re, a_im, wc, h0r, h0i)


def _s5_params(lam_re, lam_im, log_step, b_ri, c_ri):
    lam = lax.complex(lam_re, lam_im)
    a_bar = jnp.exp(lam * jnp.exp(log_step)[..., None])
    b_bar = ((a_bar - 1.0) / lam)[..., None] * lax.complex(b_ri[..., 0], b_ri[..., 1])
    return jnp.real(a_bar), jnp.imag(a_bar), jnp.real(b_bar), jnp.imag(b_bar), c_ri[..., 0], c_ri[..., 1]


def _s5_weights(a_re, a_im, bb_re, bb_im, c_re, c_im, fold, chain_rep):
    lead = a_re.shape[:2]
    m = lead[0] * lead[1]
    gpr = D_GROUPS // fold
    eye = jnp.eye(gpr, dtype=F32)

    def wb_part(bb):
        t = bb.reshape(m, fold, gpr, D_STATE, D_GROUP)
        src = jnp.transpose(t, (0, 1, 2, 4, 3))
        w = jnp.where(eye[None, None, :, None, :, None] > 0,
                      jnp.broadcast_to(src[:, :, :, :, None, :], (m, fold, gpr, D_GROUP, gpr, D_STATE)), 0.0)
        return w.reshape(m, WG, gpr * D_STATE)

    wb = jnp.concatenate([wb_part(bb_re), wb_part(bb_im)], axis=-1)

    def wc_part(cc):
        t = cc.reshape(m, fold, gpr, D_GROUP, D_STATE)
        src = jnp.transpose(t, (0, 2, 4, 1, 3))
        w = jnp.where(eye[None, :, None, None, :, None] > 0,
                      jnp.broadcast_to(src[:, :, :, :, None, :], (m, gpr, D_STATE, fold, gpr, D_GROUP)), 0.0)
        return w.reshape(m, gpr * D_STATE, WG)

    wc = jnp.concatenate([wc_part(c_re), -wc_part(c_im)], axis=1)

    def a_rows(a):
        t = a.reshape(m, fold, 1, gpr * D_STATE)
        return jnp.broadcast_to(t, (m, fold, chain_rep, gpr * D_STATE)).reshape(lead + (SUBLANES, gpr * D_STATE))

    nl = 2 * gpr * D_STATE
    return (wb.astype(BF16).reshape(lead + (WG, nl)), wc.astype(BF16).reshape(lead + (nl, WG)),
            a_rows(a_re), a_rows(a_im))


def _outproj_kernel(x_ref, mod_ref, ya_ref, yb_ref, ycf_ref, ycb_ref, cz_ref, cn_ref,
                    ydc0_ref, ydc1_ref, ydl0_ref, ydl1_ref, du_ref, dd_ref,
                    wglu_ref, wout_ref, g_ref, o_ref, *, nct):
    mod = mod_ref[0, 0]
    yc = _rms((ycf_ref[...] + ycb_ref[...]) * _silu(cz_ref[...]), cn_ref[0])
    is_ctx = pl.program_id(0) < nct
    yd = jnp.where(is_ctx, ydc0_ref[0] + ydc1_ref[0], ydl0_ref[0] + ydl1_ref[0]) + dd_ref[0] * du_ref[...]
    yd = yd * (0.5 * (1.0 + jnp.tanh(math.sqrt(2.0 / math.pi) * (yd + 0.044715 * (yd * yd * yd)))))
    gl = jnp.dot(yd.astype(BF16), wglu_ref[0], preferred_element_type=F32)
    yd = gl[:, 0:WG] * _sigmoid(gl[:, WG:2 * WG])
    cat = jnp.concatenate([ya_ref[...].astype(BF16), yb_ref[...].astype(BF16),
                           yc.astype(BF16), yd.astype(BF16)], axis=-1)
    y = jnp.dot(cat, wout_ref[0], preferred_element_type=F32)
    o_ref[...] = x_ref[...] + mod[2:3] * _rms(y, g_ref[0])


def _out_proj(x_all, mod_all, ya, yb, ycf, ycb, cz, cnorm, ydc, ydl, du, dd, wglu, wout, g_post1,
              *, layer, nct, tpl):
    n, d = x_all.shape
    row = lambda i: (i, 0)
    lay3 = lambda i: (layer, 0, 0)
    mrow = _mod_row(nct, tpl)
    sm = pl.BlockSpec((TM, WG), row)
    ydc_spec = lambda dd_: pl.BlockSpec((1, TM, WG), lambda i: (dd_, jnp.minimum(i, nct - 1), 0))
    ydl_spec = lambda dd_: pl.BlockSpec((1, TM, WG), lambda i: (dd_, jnp.maximum(i - nct, 0), 0))
    return pl.pallas_call(
        functools.partial(_outproj_kernel, nct=nct),
        grid=(n // TM,),
        in_specs=[pl.BlockSpec((TM, d), row), pl.BlockSpec((1, 1, 6, d), lambda i: (layer, mrow(i), 0, 0)),
                  sm, sm, sm, sm, sm, pl.BlockSpec((1, 1, WG), lay3),
                  ydc_spec(0), ydc_spec(1), ydl_spec(0), ydl_spec(1), sm,
                  pl.BlockSpec((1, 1, WG), lay3), pl.BlockSpec((1,) + wglu.shape[1:], lay3),
                  pl.BlockSpec((1,) + wout.shape[1:], lay3), pl.BlockSpec((1, 1, d), lay3)],
        out_specs=pl.BlockSpec((TM, d), row),
        out_shape=jax.ShapeDtypeStruct((n, d), F32),
        compiler_params=_cparams(("parallel",)),
        name="out_proj",
    )(x_all, mod_all, ya, yb, ycf, ycb, cz, cnorm, ydc, ydc, ydl, ydl, du, dd, wglu, wout, g_post1)


def _ffn_kernel(x_ref, xp_ref, xn_ref, mod_ref, g_ref, wup_ref, cw_ref, cb_ref, wdn_ref, gp_ref,
                *rest, f, fc, nct, tpc, tpl, split):
    if split:
        oc_ref, ol_ref, hext_ref = rest
    else:
        o_ref, hext_ref = rest
    i = pl.program_id(0)
    is_lat = i >= nct
    pos = jnp.where(is_lat, jnp.maximum(i - nct, 0) % tpl, i % tpc)
    nper = jnp.where(is_lat, tpl, tpc)
    mod = mod_ref[0, 0]

    def hfun(x):
        return _rms(x, g_ref[0]) * (1.0 + mod[4:5]) + mod[3:4]

    x = x_ref[...]
    hext_ref[0:SUBLANES, :] = jnp.where(pos > 0, hfun(xp_ref[...]), 0.0)
    hext_ref[SUBLANES:SUBLANES + TM, :] = hfun(x)
    hext_ref[SUBLANES + TM:2 * SUBLANES + TM, :] = jnp.where(pos < nper - 1, hfun(xn_ref[...]), 0.0)
    he = hext_ref[...].astype(BF16)
    rows = TM + 2 * SUBLANES

    def conv(u, c0):
        cw = cw_ref[0, :, c0:c0 + fc]
        um1 = pltpu.roll(u, 1, 0)[SUBLANES:SUBLANES + TM]
        up1 = pltpu.roll(u, rows - 1, 0)[SUBLANES:SUBLANES + TM]
        return (um1 * cw[0:1] + u[SUBLANES:SUBLANES + TM] * cw[1:2] + up1 * cw[2:3]
                + cb_ref[0, :, c0:c0 + fc])

    acc = jnp.zeros((TM, x.shape[1]), F32)
    for j in range(f // fc):
        ug = jnp.dot(he, wup_ref[0, :, j * fc:(j + 1) * fc], preferred_element_type=F32)
        uv = jnp.dot(he, wup_ref[0, :, f + j * fc:f + (j + 1) * fc], preferred_element_type=F32)
        a = _silu(conv(ug, j * fc)) * conv(uv, f + j * fc)
        acc = acc + jnp.dot(a.astype(BF16), wdn_ref[0, j * fc:(j + 1) * fc, :], preferred_element_type=F32)
    out = x + mod[5:6] * _rms(acc, gp_ref[0])
    if split:
        @pl.when(i < nct)
        def _():
            oc_ref[...] = out

        @pl.when(i >= nct)
        def _():
            ol_ref[...] = out
    else:
        o_ref[...] = out


def _ffn(x_all, mod_all, g_pre2, wup, cw, cb, wdn, g_post2, *, layer, nct, tpc, tpl, split):
    n, d = x_all.shape
    f = wdn.shape[1]
    fc = 256
    assert f % fc == 0
    sub = TM // SUBLANES
    nb8 = n // SUBLANES
    row = lambda i: (i, 0)
    lay3 = lambda i: (layer, 0, 0)
    mrow = _mod_row(nct, tpl)
    single = dict(pipeline_mode=pl.Buffered(1))
    if split:
        out_specs = [pl.BlockSpec((TM, d), lambda i: (jnp.minimum(i, nct - 1), 0)),
                     pl.BlockSpec((TM, d), lambda i: (jnp.maximum(i - nct, 0), 0))]
        out_shape = [jax.ShapeDtypeStruct((nct * TM, d), F32), jax.ShapeDtypeStruct((n - nct * TM, d), F32)]
    else:
        out_specs = pl.BlockSpec((TM, d), row)
        out_shape = jax.ShapeDtypeStruct((n, d), F32)
    return pl.pallas_call(
        functools.partial(_ffn_kernel, f=f, fc=fc, nct=nct, tpc=tpc, tpl=tpl, split=split),
        grid=(n // TM,),
        in_specs=[pl.BlockSpec((TM, d), row),
                  pl.BlockSpec((SUBLANES, d), lambda i: (jnp.maximum(i * sub - 1, 0), 0)),
                  pl.BlockSpec((SUBLANES, d), lambda i: (jnp.minimum((i + 1) * sub, nb8 - 1), 0)),
                  pl.BlockSpec((1, 1, 6, d), lambda i: (layer, mrow(i), 0, 0)),
                  pl.BlockSpec((1, 1, d), lay3),
                  pl.BlockSpec((1,) + wup.shape[1:], lay3, **single),
                  pl.BlockSpec((1,) + cw.shape[1:], lay3),
                  pl.BlockSpec((1,) + cb.shape[1:], lay3),
                  pl.BlockSpec((1,) + wdn.shape[1:], lay3, **single),
                  pl.BlockSpec((1, 1, d), lay3)],
        out_specs=out_specs,
        out_shape=out_shape,
        scratch_shapes=[pltpu.VMEM((TM + 2 * SUBLANES, d), F32)],
        compiler_params=_cparams(("arbitrary",)),
        name="conv_ffn",
    )(x_all, x_all, x_all, mod_all, g_pre2, wup, cw, cb, wdn, g_post2)


def _ssd_expand(st):
    z = jnp.zeros_like(st)
    hpg = N_HEADS // C_GROUPS
    left = jnp.concatenate([st[..., :hpg, :, :], z[..., hpg:, :, :]], axis=-3)
    right = jnp.concatenate([z[..., :hpg, :, :], st[..., hpg:, :, :]], axis=-3)
    return jnp.concatenate([left, right], axis=-1).reshape(st.shape[:-3] + (WG, 2 * C_STATE))


def _ssd_extract(s):
    s = s.reshape(s.shape[:-2] + (N_HEADS, HEAD_DIM, C_GROUPS, C_STATE))
    hpg = N_HEADS // C_GROUPS
    return jnp.stack([s[..., h, :, h // hpg, :] for h in range(N_HEADS)], axis=-3)


def kernel(x_prompt, x_sample, cache_a_k, cache_a_v, cache_b_k, cache_b_v, state_ssd, state_s5,
           c, c_ctx, w_mod, b_mod, g_pre1, g_post1, g_pre2, g_post2, w_in, a_lam, a_subln,
           b_qnorm, b_knorm, c_conv_w, c_conv_b, c_dt_bias, c_a_log, c_d, c_norm,
           d_lam_re, d_lam_im, d_log_step, d_b, d_c, d_d, d_glu, w_out, w_up,
           ffn_conv_w, ffn_conv_b, w_down):
    b1, l1, d = x_prompt.shape
    b2, l2, _ = x_sample.shape
    depth = w_mod.shape[0]
    past = cache_a_k.shape[2]
    n1, n2 = b1 * l1, b2 * l2
    assert d == 4 * WG and l1 % TM == 0 and l2 % TM == 0 and n1 % l2 == 0
    assert b1 % SUBLANES == 0 and SUBLANES % b2 == 0 and 1 + b2 <= SUBLANES
    nct, tpc, tpl = n1 // TM, l1 // TM, l2 // TM
    fold = SUBLANES // b2

    x_all = jnp.concatenate([x_prompt.reshape(n1, d), x_sample.reshape(n2, d)], axis=0)
    cond8 = jnp.concatenate([c_ctx[None, :], c, jnp.zeros((SUBLANES - 1 - b2, d), F32)], axis=0)
    mod_all = _modulation(cond8, w_mod, b_mod).reshape(depth, SUBLANES, 6, d)

    vec = lambda t: t.reshape(depth, 1, t.shape[-1])
    tile_h = lambda g: jnp.tile(g, (1, N_HEADS)).reshape(depth, 1, WG)
    hd = HEAD_DIM
    perm = np.concatenate([np.arange(0, 1024), 1024 + np.r_[0:hd, 0:hd, hd:2 * hd, hd:2 * hd],
                           1152 + np.r_[0:hd, 0:hd, hd:2 * hd, hd:2 * hd], np.arange(1280, 2048),
                           np.arange(2056, 2312), np.arange(2048, 2056), np.zeros(120, np.int64)])
    keep = np.concatenate([np.ones(_C_END - 120, np.float32), np.zeros(120, np.float32)])
    w_in_p = (jnp.take(w_in, jnp.asarray(perm, jnp.int32), axis=2) * jnp.asarray(keep)).astype(BF16)
    w_up16, w_dn16, w_out16, w_glu16 = (t.astype(BF16) for t in (w_up, w_down, w_out, d_glu))
    rt, ct = _rope_tables(l2)
    ones_blk = jnp.asarray(np.kron(np.eye(N_HEADS, dtype=np.float32),
                                   np.full((HEAD_DIM, HEAD_DIM), 1.0 / HEAD_DIM, np.float32))).astype(BF16)
    rep_kv = lambda t: jnp.repeat(t, 2, axis=-2).reshape(*t.shape[:-2], WG)
    cak = cache_a_k.reshape(b2, depth, past, WG).astype(BF16)
    cav = cache_a_v.reshape(b2, depth, past, WG).astype(BF16)
    cbk = rep_kv(cache_b_k).astype(BF16)
    cbv = rep_kv(cache_b_v).astype(BF16)
    qn, kn, subln = tile_h(b_qnorm), tile_h(b_knorm), tile_h(a_subln)
    h0_lat = jnp.transpose(_ssd_expand(state_ssd), (1, 2, 0, 3, 4))
    h0_ssd = jnp.concatenate([jnp.zeros((depth, 2, b1, WG, 2 * C_STATE), F32), h0_lat], axis=2)
    dir_lanes = lambda t: jnp.concatenate(
        [jnp.stack([t[:, 0], jnp.zeros_like(t[:, 0])], axis=1), jnp.stack([jnp.zeros_like(t[:, 1]), t[:, 1]], axis=1),
         jnp.zeros((depth, 2, 128 - 2 * N_HEADS), F32)], axis=-1)[:, :, None, :]
    dtb = dir_lanes(c_dt_bias)
    av = dir_lanes(-jnp.exp(c_a_log))
    dsk = jnp.repeat(c_d, HEAD_DIM, axis=-1).reshape(depth, 1, WG)
    pr = _s5_params(d_lam_re, d_lam_im, d_log_step, d_b, d_c)
    wb1, wc1, ar1, ai1 = _s5_weights(*pr, 1, SUBLANES)
    wb2, wc2, ar2, ai2 = _s5_weights(*pr, fold, b2)
    gpr = D_GROUPS // fold
    zero1 = jnp.zeros((2, b1 // SUBLANES, SUBLANES, S5_LANES // 2), F32)
    h0s = state_s5.reshape(b2, depth, 2, fold, gpr * D_STATE, 2)
    h0s = jnp.transpose(h0s, (1, 2, 3, 0, 4, 5)).reshape(depth, 2, 1, SUBLANES, gpr * D_STATE, 2)

    new_ak, new_av, new_bk, new_bv, new_ssd, new_s5 = [], [], [], [], [], []
    y_prompt = y_sample = None
    for l in range(depth):
        lam_init = 0.8 - 0.6 * math.exp(-0.3 * l)
        qa, ka, va, qb, kb, vb, ka32, va32, kb32, vb32, cz, cx, du, cdt = _in_proj(
            x_all, mod_all, vec(g_pre1), w_in_p, qn, kn, ones_blk, rt, ct, layer=l, nct=nct, tpl=tpl)

        attn = functools.partial(_attention, tq=TM, ck=1024, layer=l)
        ya = attn(qa, ka, va, None, diff=True, n_seq=b1, seq_len=l1, row_off=0,
                  lam=a_lam, subln=subln, lam_init=lam_init)
        ya = attn(qa, ka, va, ya, diff=True, n_seq=b2, seq_len=l2, row_off=n1, ctx=(cak, cav),
                  lam=a_lam, subln=subln, lam_init=lam_init)
        yb = attn(qb, kb, vb, None, diff=False, n_seq=b1, seq_len=l1, row_off=0)
        yb = attn(qb, kb, vb, yb, diff=False, n_seq=b2, seq_len=l2, row_off=n1, ctx=(cbk, cbv))

        ycf, ycb, st_f, st_b = _ssd(cx, cdt, h0_ssd[l], c_conv_w, vec(c_conv_b), dtb, av, dsk, layer=l,
                                    ncc=n1 // C_CHUNK, cpc=l1 // C_CHUNK, cpl=l2 // C_CHUNK, n_ctx_seq=b1)
        new_ssd.append(jnp.stack([_ssd_extract(st_f[:b1]), _ssd_extract(st_b[:b1])], axis=1))

        ydc, s1r, s1i = _s5_scan(du, wb1, ar1, ai1, wc1, zero1, zero1, layer=l, n_b=b1, seq_len=l1,
                                 row_off=0, tt=min(l1, 128), fold=1)
        ydl, _, _ = _s5_scan(du, wb2, ar2, ai2, wc2, h0s[l, ..., 0], h0s[l, ..., 1], layer=l, n_b=b2,
                             seq_len=l2, row_off=n1, tt=min(l2, 512), fold=fold)
        st1 = jnp.stack([s1r, s1i], axis=-1).reshape(2, b1, D_GROUPS, D_STATE, 2)
        new_s5.append(jnp.transpose(st1, (1, 0, 2, 3, 4)))

        x_all = _out_proj(x_all, mod_all, ya, yb, ycf, ycb, cz, vec(c_norm), ydc.reshape(2, n1, WG),
                          ydl.reshape(2, n2, WG), du, vec(d_d), w_glu16, w_out16, vec(g_post1),
                          layer=l, nct=nct, tpl=tpl)
        last = l == depth - 1
        res = _ffn(x_all, mod_all, vec(g_pre2), w_up16, ffn_conv_w, vec(ffn_conv_b), w_dn16, vec(g_post2),
                   layer=l, nct=nct, tpc=tpc, tpl=tpl, split=last)
        if last:
            y_prompt, y_sample = res[0].reshape(b1, l1, d), res[1].reshape(b2, l2, d)
        else:
            x_all = res

        new_ak.append(ka32[:n1].reshape(b1, l1, N_HEADS, HEAD_DIM))
        new_av.append(va32[:n1].reshape(b1, l1, N_HEADS, HEAD_DIM))
        pick = lambda t: t[:n1].reshape(b1, l1, N_HEADS, HEAD_DIM)[:, :, ::2, :]
        new_bk.append(pick(kb32))
        new_bv.append(pick(vb32))

    st = lambda xs: jnp.stack(xs, axis=1)
    return (y_prompt, y_sample, st(new_ak), st(new_av), st(new_bk), st(new_bv), st(new_ssd), st(new_s5))
```

```python
import functools
import math

import numpy as np
import jax
import jax.numpy as jnp
from jax import lax
from jax.experimental import pallas as pl
from jax.experimental.pallas import tpu as pltpu

F32 = jnp.float32
BF16 = jnp.bfloat16

EPS = 1e-6
LOG2E = math.log2(math.e)
GRID_W = 64
ROPE_BASE = 10000.0
HEAD_DIM = 64
A_HALF = HEAD_DIM // 2
N_HEADS = 4
WG = N_HEADS * HEAD_DIM
C_GROUPS = 2
C_STATE = 64
C_CHUNK = 128
D_GROUP = 16
D_GROUPS = WG // D_GROUP
D_STATE = 64
S5_LANES = D_GROUPS * D_STATE * 2

TM = 256
TQ_LATENT = 512
SUBLANES = 8
VMEM_LIMIT = 56 * 1024 * 1024

_C_AQ, _C_AK, _C_AV, _C_BQ, _C_BK, _C_BV, _C_CZ, _C_CX, _C_DU, _C_DT, _C_END = (
    0, 256, 512, 768, 1024, 1280, 1536, 1792, 2304, 2560, 2688)


def _sigmoid(x):
    return 1.0 / (1.0 + jnp.exp(-x))


def _silu(x):
    return x * _sigmoid(x)


def _cparams(sem):
    return pltpu.CompilerParams(dimension_semantics=sem, vmem_limit_bytes=VMEM_LIMIT)


def _lane_mask(width, lo, hi):
    lane = lax.broadcasted_iota(jnp.int32, (1, width), 1)
    return (lane >= lo) & (lane < hi)


def _rms(x, g):
    return (x * lax.rsqrt(jnp.mean(x * x, axis=-1, keepdims=True) + EPS)) * g


def _mod_row(nct, tpl):
    return lambda i: jnp.where(i < nct, 0, 1 + jnp.maximum(i - nct, 0) // tpl)


def _mod_kernel(c_ref, w_ref, b_ref, o_ref):
    s = _silu(c_ref[...])
    o_ref[0] = jnp.dot(s.astype(BF16), w_ref[0].astype(BF16), preferred_element_type=F32) + b_ref[0]


def _modulation(cond8, w_mod, b_mod):
    depth, d, n = w_mod.shape
    tn = 1536
    return pl.pallas_call(
        _mod_kernel,
        grid=(depth, n // tn),
        in_specs=[pl.BlockSpec((SUBLANES, d), lambda l, j: (0, 0)),
                  pl.BlockSpec((1, d, tn), lambda l, j: (l, 0, j)),
                  pl.BlockSpec((1, 1, tn), lambda l, j: (l, 0, j))],
        out_specs=pl.BlockSpec((1, SUBLANES, tn), lambda l, j: (l, 0, j)),
        out_shape=jax.ShapeDtypeStruct((depth, SUBLANES, n), F32),
        compiler_params=_cparams(("parallel", "parallel")),
        name="modulation",
    )(cond8, w_mod, b_mod.reshape(depth, 1, n))


def _x_specs(x, nct):
    if isinstance(x, tuple):
        d = x[0].shape[1]
        return [pl.BlockSpec((TM, d), lambda i: (jnp.minimum(i, nct - 1), 0)),
                pl.BlockSpec((TM, d), lambda i: (jnp.maximum(i - nct, 0), 0))], list(x)
    return [pl.BlockSpec((TM, x.shape[1]), lambda i: (i, 0))], [x]


def _load_x(x_refs, nct):
    if len(x_refs) == 2:
        return jnp.where(pl.program_id(0) < nct, x_refs[0][...], x_refs[1][...])
    return x_refs[0][...]


def _inproj_kernel(*refs, nct, n_x):
    x_refs = refs[:n_x]
    (mod_ref, g_ref, w_ref, qn_ref, kn_ref, ones_ref, rt_ref, ct_ref,
     qa_ref, ka_ref, va_ref, qb_ref, kb_ref, vb_ref,
     ka32_ref, va32_ref, kb32_ref, vb32_ref, cz_ref, cx_ref, du_ref, dt_ref) = refs[n_x:]
    i = pl.program_id(0)
    mod = mod_ref[0, 0]
    h = _rms(_load_x(x_refs, nct), g_ref[0]) * (1.0 + mod[1:2]) + mod[0:1]
    p = jnp.dot(h.astype(BF16), w_ref[0], preferred_element_type=F32)
    va_ref[...] = p[:, _C_AV:_C_BQ].astype(BF16)
    vb_ref[...] = p[:, _C_BV:_C_CZ].astype(BF16)
    va32_ref[...] = p[:, _C_AV:_C_BQ]
    vb32_ref[...] = p[:, _C_BV:_C_CZ]
    cz_ref[...] = p[:, _C_CZ:_C_CX]
    cx_ref[...] = p[:, _C_CX:_C_DU]
    du_ref[...] = p[:, _C_DU:_C_DT]
    dt_ref[...] = p[:, _C_DT:_C_END]

    def headnorm(t, gain):
        sq = t * t
        hi = sq.astype(BF16)
        lo = (sq - hi.astype(F32)).astype(BF16)
        ms = (jnp.dot(hi, ones_ref[...], preferred_element_type=F32)
              + jnp.dot(lo, ones_ref[...], preferred_element_type=F32))
        return (t * lax.rsqrt(ms + EPS)) * gain

    aq = p[:, _C_AQ:_C_AK] * (A_HALF ** -0.5 * LOG2E)
    ak = p[:, _C_AK:_C_AV]
    bq = headnorm(p[:, _C_BQ:_C_BK], qn_ref[0]) * (HEAD_DIM ** -0.5 * LOG2E)
    bk = headnorm(p[:, _C_BK:_C_BV], kn_ref[0])
    ka32_ref[...] = ak
    kb32_ref[...] = bk

    @pl.when(i < nct)
    def _():
        qa_ref[...] = aq.astype(BF16)
        ka_ref[...] = ak.astype(BF16)
        qb_ref[...] = bq.astype(BF16)
        kb_ref[...] = bk.astype(BF16)

    @pl.when(i >= nct)
    def _():
        rows = TM // GRID_W
        rt = rt_ref[0]
        lane = lax.broadcasted_iota(jnp.int32, (1, WG), 1)
        row_lane_a = (lane % (A_HALF // 2)) < A_HALF // 4
        row_lane_b = (lane % (HEAD_DIM // 2)) < HEAD_DIM // 4

        def table(k, row_lane):
            parts = [jnp.where(row_lane, rt[k, r:r + 1, :], ct_ref[k]) for r in range(rows)]
            return jnp.concatenate(parts, axis=0)

        def rope(t, base, row_lane, dist):
            return (t * table(base, row_lane)
                    + pltpu.roll(t, WG - dist, 1) * table(base + 1, row_lane)
                    + pltpu.roll(t, dist, 1) * table(base + 2, row_lane))

        qa_ref[...] = rope(aq, 0, row_lane_a, A_HALF // 2).astype(BF16)
        ka_ref[...] = rope(ak, 0, row_lane_a, A_HALF // 2).astype(BF16)
        qb_ref[...] = rope(bq, 3, row_lane_b, HEAD_DIM // 2).astype(BF16)
        kb_ref[...] = rope(bk, 3, row_lane_b, HEAD_DIM // 2).astype(BF16)


def _rope_tables(l2):
    rows = l2 // GRID_W
    lane = np.arange(WG)

    def one(block, n):
        freqs = (np.float32(ROPE_BASE) ** (-np.arange(n, dtype=np.float32) / np.float32(n))).astype(np.float32)
        p = lane % block
        idx = p % (block // 2)
        first = p < (block // 2)
        f = freqs[idx % n]

        def tabs(pos):
            ang = (pos[:, None].astype(np.float32) * f[None, :]).astype(np.float32)
            c, s = np.cos(ang).astype(np.float32), np.sin(ang).astype(np.float32)
            return [c, np.where(first[None, :], -s, 0.0).astype(np.float32),
                    np.where(first[None, :], 0.0, s).astype(np.float32)]

        return tabs(np.arange(rows)), tabs(np.arange(GRID_W))

    ra, ca = one(A_HALF, A_HALF // 4)
    rb, cb = one(HEAD_DIM, HEAD_DIM // 4)
    rt = np.stack(ra + rb, axis=0)
    ct = np.stack(ca + cb, axis=0)
    rpt = TM // GRID_W
    npt = l2 // TM
    rt = rt.reshape(6, npt, rpt, WG).transpose(1, 0, 2, 3)
    rt = np.concatenate([rt, np.zeros((npt, 6, SUBLANES - rpt, WG), np.float32)], axis=2)
    return jnp.asarray(rt), jnp.asarray(ct)


def _in_proj(x_all, mod_all, g_pre1, w_in_p, qn, kn, ones_blk, rt, ct, *, layer, nct, tpl):
    x_specs, x_args = _x_specs(x_all, nct)
    n = sum(a.shape[0] for a in x_args)
    d = x_args[0].shape[1]
    nt = n // TM
    row = lambda i: (i, 0)
    const2 = lambda i: (0, 0)
    lay3 = lambda i: (layer, 0, 0)
    mrow = _mod_row(nct, tpl)
    ctx_row = lambda i: (jnp.minimum(i, nct), 0)
    n_ctx = (nct + 1) * TM
    out_specs = ([pl.BlockSpec((TM, WG), row)] * 6 + [pl.BlockSpec((TM, WG), ctx_row)] * 4
                 + [pl.BlockSpec((TM, w), row) for w in (WG, 2 * WG, WG, 128)])
    out_shape = ([jax.ShapeDtypeStruct((n, WG), BF16)] * 6 + [jax.ShapeDtypeStruct((n_ctx, WG), F32)] * 4
                 + [jax.ShapeDtypeStruct((n, w), F32) for w in (WG, 2 * WG, WG, 128)])
    return pl.pallas_call(
        functools.partial(_inproj_kernel, nct=nct, n_x=len(x_args)),
        grid=(nt,),
        in_specs=x_specs + [
                  pl.BlockSpec((1, 1, 6, d), lambda i: (layer, mrow(i), 0, 0)),
                  pl.BlockSpec((1, 1, d), lay3),
                  pl.BlockSpec((1,) + w_in_p.shape[1:], lay3),
                  pl.BlockSpec((1, 1, WG), lay3),
                  pl.BlockSpec((1, 1, WG), lay3),
                  pl.BlockSpec((WG, WG), const2),
                  pl.BlockSpec((1, 6, SUBLANES, WG), lambda i: (jnp.maximum(i - nct, 0) % tpl, 0, 0, 0)),
                  pl.BlockSpec(ct.shape, lambda i: (0, 0, 0))],
        out_specs=out_specs,
        out_shape=out_shape,
        compiler_params=_cparams(("arbitrary",)),
        name="in_proj",
    )(*x_args, mod_all, g_pre1, w_in_p, qn, kn, ones_blk, rt, ct)


def _attn_kernel(*refs, diff, tq, lk_new, lc, ck, lam_init):
    it = iter(refs)
    q_ref, k_ref, v_ref = next(it), next(it), next(it)
    kc_ref = vc_ref = None
    if lc:
        kc_ref, vc_ref = next(it).at[0, 0], next(it).at[0, 0]
    lam_ref = sub_ref = None
    if diff:
        lam_ref, sub_ref = next(it), next(it)
    o_ref = next(it)
    s_ref = (next(it), next(it))
    m_ref = (next(it), next(it))
    oall_ref = next(it)

    blocks = [(k_ref, v_ref, r0, min(ck, lk_new - r0), r0) for r0 in range(0, lk_new, ck)]
    if lc:
        blocks.append((kc_ref, vc_ref, 0, lc, lk_new))

    n_streams = 2 * N_HEADS if diff else N_HEADS
    width = A_HALF if diff else HEAD_DIM
    q = q_ref[...]
    lane = lax.broadcasted_iota(jnp.int32, (1, WG), 1)

    def step(i, slot, scores, values):
        if scores:
            lo = i * width
            qm = jnp.where((lane >= lo) & (lane < lo + width), q, jnp.zeros_like(q))
            s_cur = s_ref[slot]
            mrun = jnp.full((tq, 128), -jnp.inf, F32)
        if values:
            s_prev = s_ref[1 - slot]
            m = jnp.max(m_ref[1 - slot][...], axis=-1, keepdims=True)
            lrun = jnp.zeros((tq, 128), F32)
            acc = jnp.zeros((tq, WG), F32)
        for kr, vr, r0, rows, c0 in blocks:
            if scores:
                s = lax.dot_general(qm, kr[r0:r0 + rows, :], (((1,), (1,)), ((), ())),
                                    preferred_element_type=F32)
                s_cur[:, c0:c0 + rows] = s
                for t in range(rows // 128):
                    mrun = jnp.maximum(mrun, s[:, t * 128:(t + 1) * 128])
            if values:
                p = jnp.exp2(s_prev[:, c0:c0 + rows] - m)
                for t in range(rows // 128):
                    lrun = lrun + p[:, t * 128:(t + 1) * 128]
                acc = acc + jnp.dot(p.astype(BF16), vr[r0:r0 + rows, :], preferred_element_type=F32)
        if scores:
            m_ref[slot][...] = mrun
        if values:
            oall_ref[i - 1] = acc * (1.0 / jnp.sum(lrun, axis=-1, keepdims=True))

    step(0, 0, True, False)

    def body(g, carry):
        step(2 * g + 1, 1, True, True)
        step(2 * g + 2, 0, True, True)
        return carry

    lax.fori_loop(0, (n_streams - 2) // 2, body, 0)
    step(n_streams - 1, 1, True, True)
    step(n_streams, 0, False, True)

    out = jnp.zeros((tq, WG), F32)
    if diff:
        lp = lam_ref[0]
        lam = (jnp.exp(jnp.sum(lp[0:1] * lp[1:2], axis=-1, keepdims=True))
               - jnp.exp(jnp.sum(lp[2:3] * lp[3:4], axis=-1, keepdims=True)) + lam_init)
    for h in range(N_HEADS):
        hm = _lane_mask(WG, h * HEAD_DIM, (h + 1) * HEAD_DIM)
        if diff:
            o = oall_ref[2 * h] - lam * oall_ref[2 * h + 1]
            ms = jnp.sum(jnp.where(hm, o * o, 0.0), axis=-1, keepdims=True) * (1.0 / HEAD_DIM)
            o = ((o * lax.rsqrt(ms + EPS)) * sub_ref[0]) * (1.0 - lam_init)
        else:
            o = oall_ref[h]
        out = jnp.where(hm, o, out)
    o_ref[...] = out


def _attention(q, k, v, *, diff, n_seq, seq_len, row_off, tq, ck, layer, ctx=None,
               lam=None, subln=None, lam_init=0.0):
    assert row_off % seq_len == 0 and seq_len % tq == 0
    qpb = seq_len // tq
    q_map = lambda s, j: (row_off // tq + s * qpb + j, 0)
    kv_map = lambda s, j: (row_off // seq_len + s, 0)
    in_specs = [pl.BlockSpec((tq, WG), q_map), pl.BlockSpec((seq_len, WG), kv_map),
                pl.BlockSpec((seq_len, WG), kv_map)]
    args = [q, k, v]
    lc = 0
    if ctx is not None:
        kc, vc = ctx
        lc = kc.shape[2]
        cmap = lambda s, j: (s, layer, 0, 0)
        in_specs += [pl.BlockSpec((1, 1, lc, WG), cmap), pl.BlockSpec((1, 1, lc, WG), cmap)]
        args += [kc, vc]
    if diff:
        lay3 = lambda s, j: (layer, 0, 0)
        in_specs += [pl.BlockSpec((1,) + lam.shape[1:], lay3), pl.BlockSpec((1, 1, WG), lay3)]
        args += [lam, subln]
    lk = seq_len + lc
    return pl.pallas_call(
        functools.partial(_attn_kernel, diff=diff, tq=tq, lk_new=seq_len, lc=lc, ck=min(ck, seq_len),
                          lam_init=lam_init),
        grid=(n_seq, qpb),
        in_specs=in_specs,
        out_specs=pl.BlockSpec((tq, WG), lambda s, j: (s * qpb + j, 0)),
        out_shape=jax.ShapeDtypeStruct((n_seq * seq_len, WG), F32),
        scratch_shapes=[pltpu.VMEM((tq, lk), F32), pltpu.VMEM((tq, lk), F32),
                        pltpu.VMEM((tq, 128), F32), pltpu.VMEM((tq, 128), F32),
                        pltpu.VMEM((2 * N_HEADS if diff else N_HEADS, tq, WG), F32)],
        compiler_params=_cparams(("parallel", "parallel")),
        name="attn_diff" if diff else "attn_gqa",
    )(*args)


SSD_PAIR = 2 * C_CHUNK


def _ssd_intra(xbc, dt_raw, dtb, av, *, reverse):
    q = C_CHUNK
    xs = xbc[:, 0:WG]
    bm = xbc[:, WG:WG + 128].astype(BF16)
    cm = xbc[:, WG + 128:WG + 256]
    raw = dt_raw + dtb
    dt = jnp.maximum(raw, 0.0) + jnp.log1p(jnp.exp(-jnp.abs(raw)))
    dta = dt * av
    li = lax.broadcasted_iota(jnp.int32, (q, q), 0)
    si = lax.broadcasted_iota(jnp.int32, (q, q), 1)
    causal = (si >= li) if reverse else (si <= li)
    cum = jnp.dot(causal.astype(F32), dta, preferred_element_type=F32, precision=lax.Precision.HIGHEST)
    cum_t = cum.T
    end = 0 if reverse else q - 1
    d0 = N_HEADS if reverse else 0
    rowh = lax.broadcasted_iota(jnp.int32, (WG, 1), 0) // HEAD_DIM

    y = jnp.zeros((q, WG), F32)
    xw = jnp.zeros((q, WG), F32)
    ecum = jnp.zeros((q, WG), F32)
    cdec = jnp.zeros((WG, 1), F32)
    gmat = None
    for h in range(N_HEADS):
        j = d0 + h
        g = h // (N_HEADS // C_GROUPS)
        if h % (N_HEADS // C_GROUPS) == 0:
            cg = jnp.where(_lane_mask(128, g * C_STATE, (g + 1) * C_STATE), cm, 0.0).astype(BF16)
            gmat = lax.dot_general(cg, bm, (((1,), (1,)), ((), ())), preferred_element_type=F32)
        col = cum[:, j:j + 1]
        seg = col - cum_t[j:j + 1, :]
        decay = jnp.where(causal, jnp.exp(jnp.where(causal, seg, 0.0)), 0.0)
        hm = _lane_mask(WG, h * HEAD_DIM, (h + 1) * HEAD_DIM)
        xdt = jnp.where(hm, xs * dt[:, j:j + 1], 0.0)
        y = y + jnp.dot((gmat * decay).astype(BF16), xdt.astype(BF16), preferred_element_type=F32)
        cend = cum[end:end + 1, j:j + 1]
        xw = xw + xdt * jnp.exp(cend - col)
        ecum = jnp.where(hm, jnp.exp(col), ecum)
        cdec = jnp.where(rowh == h, jnp.exp(cend), cdec)

    st = lax.dot_general(xw.astype(BF16), bm, (((0,), (0,)), ((), ())), preferred_element_type=F32)
    return y, xs, cm.astype(BF16), ecum, cdec, st


def _ssd_direction(x_ref, xp_ref, xn_ref, dt_ref, h0_ref, cw_ref, cb_ref, dtb_ref, av_ref, s_ref,
                   y_ref, st_ref, skip, p, *, reverse, npc, ppc, ppl):
    q = C_CHUNK
    is_lat = p >= npc
    pos = jnp.where(is_lat, jnp.maximum(p - npc, 0) % ppl, p % ppc)
    nper = jnp.where(is_lat, ppl, ppc)
    enter = (pos == nper - 1) if reverse else (pos == 0)

    @pl.when(enter)
    def _():
        s_ref[...] = h0_ref[0, 0]

    x = x_ref[...]
    rid = lax.broadcasted_iota(jnp.int32, (SSD_PAIR, 1), 0)
    prow = jnp.where(pos > 0, xp_ref[SUBLANES - 1:SUBLANES, :], 0.0)
    nrow = jnp.where(pos < nper - 1, xn_ref[0:1, :], 0.0)
    xm1 = jnp.where(rid == 0, prow, pltpu.roll(x, 1, 0))
    xp1 = jnp.where(rid == SSD_PAIR - 1, nrow, pltpu.roll(x, SSD_PAIR - 1, 0))
    cw = cw_ref[0]
    xbc = _silu(xm1 * cw[0:1] + x * cw[1:2] + xp1 * cw[2:3] + cb_ref[0])

    parts = [_ssd_intra(xbc[k * q:(k + 1) * q], dt_ref[k * q:(k + 1) * q, :], dtb_ref[...], av_ref[...],
                        reverse=reverse) for k in range(2)]
    rowh = lax.broadcasted_iota(jnp.int32, (WG, 1), 0) // HEAD_DIM
    colg = lax.broadcasted_iota(jnp.int32, (1, 128), 1) // C_STATE
    blk = (rowh // (N_HEADS // C_GROUPS)) == colg
    s = s_ref[...]
    for k in ((1, 0) if reverse else (0, 1)):
        y, xs, cm, ecum, cdec, st = parts[k]
        s_msk = jnp.where(blk, s, 0.0).astype(BF16)
        y = y + lax.dot_general(cm, s_msk, (((1,), (1,)), ((), ())), preferred_element_type=F32) * ecum
        if skip is not None:
            y = y + skip * xs
        y_ref[k * q:(k + 1) * q, :] = y
        s = s * cdec + st
    s_ref[...] = s
    st_ref[0] = s


def _ssd_kernel(xf_ref, xpf_ref, xnf_ref, dtf_ref, h0f_ref, xb_ref, xpb_ref, xnb_ref, dtr_ref, h0b_ref,
                cw_ref, cb_ref, dtb_ref, av_ref, dsk_ref,
                yf_ref, yb_ref, stf_ref, stb_ref, sf_ref, sb_ref, *, npc, ppc, ppl):
    t = pl.program_id(0)
    kw = dict(npc=npc, ppc=ppc, ppl=ppl)
    _ssd_direction(xf_ref, xpf_ref, xnf_ref, dtf_ref, h0f_ref, cw_ref, cb_ref, dtb_ref.at[0, 0], av_ref.at[0, 0],
                   sf_ref, yf_ref, stf_ref, dsk_ref[0], t, reverse=False, **kw)
    _ssd_direction(xb_ref, xpb_ref, xnb_ref, dtr_ref, h0b_ref, cw_ref, cb_ref, dtb_ref.at[0, 1], av_ref.at[0, 1],
                   sb_ref, yb_ref, stb_ref, None, pl.num_programs(0) - 1 - t, reverse=True, **kw)


def _ssd(cx, cdt, h0, conv_w, conv_b, dtb, av, dsk, *, layer, npc, ppc, ppl, n_ctx_seq):
    n = cx.shape[0]
    nc = n // SSD_PAIR
    n_seq = h0.shape[1]
    sub = SSD_PAIR // SUBLANES
    nb8 = n // SUBLANES
    lay3 = lambda t: (layer, 0, 0)

    def specs(cidx, d):
        def seq_of(t):
            c = cidx(t)
            return jnp.where(c < npc, c // ppc, n_ctx_seq + jnp.maximum(c - npc, 0) // ppl)

        row = lambda t: (cidx(t), 0)
        ins = [pl.BlockSpec((SSD_PAIR, 2 * WG), row),
               pl.BlockSpec((SUBLANES, 2 * WG), lambda t: (jnp.maximum(cidx(t) * sub - 1, 0), 0)),
               pl.BlockSpec((SUBLANES, 2 * WG), lambda t: (jnp.minimum((cidx(t) + 1) * sub, nb8 - 1), 0)),
               pl.BlockSpec((SSD_PAIR, 128), row),
               pl.BlockSpec((1, 1, WG, 128), lambda t: (d, seq_of(t), 0, 0))]
        outs = [pl.BlockSpec((SSD_PAIR, WG), row), pl.BlockSpec((1, WG, 128), lambda t: (seq_of(t), 0, 0))]
        return ins, outs

    ins_f, outs_f = specs(lambda t: t, 0)
    ins_b, outs_b = specs(lambda t: nc - 1 - t, 1)
    return pl.pallas_call(
        functools.partial(_ssd_kernel, npc=npc, ppc=ppc, ppl=ppl),
        grid=(nc,),
        in_specs=ins_f + ins_b + [pl.BlockSpec((1, 3, 2 * WG), lay3),
                                  pl.BlockSpec((1, 1, 2 * WG), lay3),
                                  pl.BlockSpec((1, 2, 1, 128), lambda t: (layer, 0, 0, 0)),
                                  pl.BlockSpec((1, 2, 1, 128), lambda t: (layer, 0, 0, 0)),
                                  pl.BlockSpec((1, 1, WG), lay3)],
        out_specs=[outs_f[0], outs_b[0], outs_f[1], outs_b[1]],
        out_shape=[jax.ShapeDtypeStruct((n, WG), F32), jax.ShapeDtypeStruct((n, WG), F32),
                   jax.ShapeDtypeStruct((n_seq, WG, 128), F32), jax.ShapeDtypeStruct((n_seq, WG, 128), F32)],
        scratch_shapes=[pltpu.VMEM((WG, 128), F32), pltpu.VMEM((WG, 128), F32)],
        compiler_params=_cparams(("arbitrary",)),
        name="ssd",
    )(cx, cx, cx, cdt, h0, cx, cx, cx, cdt, h0, conv_w, conv_b, dtb, av, dsk)


def _s5_kernel(*refs, tt, nl, nb_in, fold):
    u_refs = refs[:nb_in]
    (wb_ref, are_ref, aim_ref, wc_ref, h0r_ref, h0i_ref, y_ref, sr_ref, si_ref,
     ust_ref, ytm_ref, bu_ref, xs_ref, cr_ref, ci_ref) = refs[nb_in:]
    hl = nl // 2
    d = pl.program_id(0)

    @pl.when(pl.program_id(2) == 0)
    def _():
        cr_ref[...] = h0r_ref[0, 0]
        ci_ref[...] = h0i_ref[0, 0]

    lane = lax.broadcasted_iota(jnp.int32, (1, WG), 1)
    gw = WG // fold
    for qq in range(fold):
        for b in range(nb_in):
            c = qq * nb_in + b
            u = u_refs[b][...]
            if fold > 1:
                u = jnp.where((lane >= qq * gw) & (lane < (qq + 1) * gw), u, 0.0)
            for k in range(WG // 128):
                ust_ref[k, pl.ds(c, tt, stride=SUBLANES), :] = u[:, k * 128:(k + 1) * 128]
    ust = jnp.concatenate([ust_ref[k].astype(BF16) for k in range(WG // 128)], axis=1)
    bu_ref[...] = jnp.dot(ust, wb_ref[0, 0], preferred_element_type=F32)
    a_re = are_ref[0, 0]
    a_im = aim_ref[0, 0]

    def step(t, carry):
        xr, xi = carry
        te = t + d * (tt - 1 - 2 * t)
        r0 = pl.multiple_of(te * SUBLANES, SUBLANES)
        nr = a_re * xr - a_im * xi + bu_ref[pl.ds(r0, SUBLANES), 0:hl]
        ni = a_re * xi + a_im * xr + bu_ref[pl.ds(r0, SUBLANES), hl:nl]
        xs_ref[pl.ds(r0, SUBLANES), 0:hl] = nr
        xs_ref[pl.ds(r0, SUBLANES), hl:nl] = ni
        return nr, ni

    xr, xi = lax.fori_loop(0, tt, step, (cr_ref[...], ci_ref[...]), unroll=4)
    cr_ref[...] = xr
    ci_ref[...] = xi
    sr_ref[0, 0] = xr
    si_ref[0, 0] = xi
    y = jnp.dot(xs_ref[...].astype(BF16), wc_ref[0, 0], preferred_element_type=F32)
    for k in range(WG // 128):
        ytm_ref[k] = y[:, k * 128:(k + 1) * 128]

    def chain_rows(c):
        return jnp.concatenate([ytm_ref[k, pl.ds(c, tt, stride=SUBLANES), :] for k in range(WG // 128)], axis=1)

    for b in range(nb_in):
        if fold == 1:
            y_ref[0, b] = chain_rows(b)
        else:
            acc = jnp.zeros((tt, WG), F32)
            for qq in range(fold):
                acc = jnp.where((lane >= qq * gw) & (lane < (qq + 1) * gw), chain_rows(qq * nb_in + b), acc)
            y_ref[0, b] = acc


def _s5_scan(du, wb, a_re, a_im, wc, h0r, h0i, *, layer, n_b, seq_len, row_off, tt, fold):
    nb_in = SUBLANES // fold
    n_slab = n_b // nb_in
    nl = wb.shape[-1]
    hl = nl // 2
    nblk = seq_len // tt
    assert seq_len % tt == 0 and row_off % tt == 0
    tblk = lambda d, j: j + d * (nblk - 1 - 2 * j)
    u_specs = [pl.BlockSpec((tt, WG), functools.partial(
        lambda d, s, j, b: (row_off // tt + (s * nb_in + b) * nblk + tblk(d, j), 0), b=b)) for b in range(nb_in)]
    wmap = lambda d, s, j: (layer, d, 0, 0)
    smap = lambda d, s, j: (d, s, 0, 0)
    st_spec = pl.BlockSpec((1, 1, SUBLANES, hl), smap)
    rows = SUBLANES * tt
    return pl.pallas_call(
        functools.partial(_s5_kernel, tt=tt, nl=nl, nb_in=nb_in, fold=fold),
        grid=(2, n_slab, nblk),
        in_specs=u_specs + [pl.BlockSpec((1, 1, WG, nl), wmap),
                            pl.BlockSpec((1, 1, SUBLANES, hl), wmap),
                            pl.BlockSpec((1, 1, SUBLANES, hl), wmap),
                            pl.BlockSpec((1, 1, nl, WG), wmap),
                            st_spec, st_spec],
        out_specs=[pl.BlockSpec((1, nb_in, tt, WG), lambda d, s, j: (d, s, tblk(d, j), 0)), st_spec, st_spec],
        out_shape=[jax.ShapeDtypeStruct((2, n_b, seq_len, WG), F32),
                   jax.ShapeDtypeStruct((2, n_slab, SUBLANES, hl), F32),
                   jax.ShapeDtypeStruct((2, n_slab, SUBLANES, hl), F32)],
        scratch_shapes=[pltpu.VMEM((WG // 128, rows, 128), F32), pltpu.VMEM((WG // 128, rows, 128), F32),
                        pltpu.VMEM((rows, nl), F32), pltpu.VMEM((rows, nl), F32),
                        pltpu.VMEM((SUBLANES, hl), F32), pltpu.VMEM((SUBLANES, hl), F32)],
        compiler_params=_cparams(("parallel", "parallel", "arbitrary")),
        name="s5_scan",
    )(*([du] * nb_in), wb, a_re, a_im, wc, h0r, h0i)


def _s5_params(lam_re, lam_im, log_step, b_ri, c_ri):
    lam = lax.complex(lam_re, lam_im)
    a_bar = jnp.exp(lam * jnp.exp(log_step)[..., None])
    b_bar = ((a_bar - 1.0) / lam)[..., None] * lax.complex(b_ri[..., 0], b_ri[..., 1])
    return jnp.real(a_bar), jnp.imag(a_bar), jnp.real(b_bar), jnp.imag(b_bar), c_ri[..., 0], c_ri[..., 1]


def _s5_weights(a_re, a_im, bb_re, bb_im, c_re, c_im, fold, chain_rep):
    lead = a_re.shape[:2]
    m = lead[0] * lead[1]
    gpr = D_GROUPS // fold
    eye = jnp.eye(gpr, dtype=F32)

    def wb_part(bb):
        t = bb.reshape(m, fold, gpr, D_STATE, D_GROUP)
        src = jnp.transpose(t, (0, 1, 2, 4, 3))
        w = jnp.where(eye[None, None, :, None, :, None] > 0,
                      jnp.broadcast_to(src[:, :, :, :, None, :], (m, fold, gpr, D_GROUP, gpr, D_STATE)), 0.0)
        return w.reshape(m, WG, gpr * D_STATE)

    wb = jnp.concatenate([wb_part(bb_re), wb_part(bb_im)], axis=-1)

    def wc_part(cc):
        t = cc.reshape(m, fold, gpr, D_GROUP, D_STATE)
        src = jnp.transpose(t, (0, 2, 4, 1, 3))
        w = jnp.where(eye[None, :, None, None, :, None] > 0,
                      jnp.broadcast_to(src[:, :, :, :, None, :], (m, gpr, D_STATE, fold, gpr, D_GROUP)), 0.0)
        return w.reshape(m, gpr * D_STATE, WG)

    wc = jnp.concatenate([wc_part(c_re), -wc_part(c_im)], axis=1)

    def a_rows(a):
        t = a.reshape(m, fold, 1, gpr * D_STATE)
        return jnp.broadcast_to(t, (m, fold, chain_rep, gpr * D_STATE)).reshape(lead + (SUBLANES, gpr * D_STATE))

    nl = 2 * gpr * D_STATE
    return (wb.astype(BF16).reshape(lead + (WG, nl)), wc.astype(BF16).reshape(lead + (nl, WG)),
            a_rows(a_re), a_rows(a_im))


def _outproj_kernel(*refs, nct, n_x):
    x_refs = refs[:n_x]
    (mod_ref, yac_ref, yal_ref, ybc_ref, ybl_ref, ycf_ref, ycb_ref, cz_ref, cn_ref,
     ydc0_ref, ydc1_ref, ydl0_ref, ydl1_ref, du_ref, dd_ref, wglu_ref, wout_ref, g_ref, o_ref) = refs[n_x:]
    mod = mod_ref[0, 0]
    yc = _rms((ycf_ref[...] + ycb_ref[...]) * _silu(cz_ref[...]), cn_ref[0])
    is_ctx = pl.program_id(0) < nct
    ya = jnp.where(is_ctx, yac_ref[...], yal_ref[...])
    yb = jnp.where(is_ctx, ybc_ref[...], ybl_ref[...])
    yd = jnp.where(is_ctx, ydc0_ref[0] + ydc1_ref[0], ydl0_ref[0] + ydl1_ref[0]) + dd_ref[0] * du_ref[...]
    yd = yd * (0.5 * (1.0 + jnp.tanh(math.sqrt(2.0 / math.pi) * (yd + 0.044715 * (yd * yd * yd)))))
    gl = jnp.dot(yd.astype(BF16), wglu_ref[0], preferred_element_type=F32)
    yd = gl[:, 0:WG] * _sigmoid(gl[:, WG:2 * WG])
    cat = jnp.concatenate([ya.astype(BF16), yb.astype(BF16), yc.astype(BF16), yd.astype(BF16)], axis=-1)
    y = jnp.dot(cat, wout_ref[0], preferred_element_type=F32)
    o_ref[...] = _load_x(x_refs, nct) + mod[2:3] * _rms(y, g_ref[0])


def _out_proj(x_all, mod_all, ya, yb, ycf, ycb, cz, cnorm, ydc, ydl, du, dd, wglu, wout, g_post1,
              *, layer, nct, tpl):
    x_specs, x_args = _x_specs(x_all, nct)
    n = sum(a.shape[0] for a in x_args)
    d = x_args[0].shape[1]
    row = lambda i: (i, 0)
    lay3 = lambda i: (layer, 0, 0)
    mrow = _mod_row(nct, tpl)
    sm = pl.BlockSpec((TM, WG), row)
    smc = pl.BlockSpec((TM, WG), lambda i: (jnp.minimum(i, nct - 1), 0))
    sml = pl.BlockSpec((TM, WG), lambda i: (jnp.maximum(i - nct, 0), 0))
    ydc_spec = lambda dd_: pl.BlockSpec((1, TM, WG), lambda i: (dd_, jnp.minimum(i, nct - 1), 0))
    ydl_spec = lambda dd_: pl.BlockSpec((1, TM, WG), lambda i: (dd_, jnp.maximum(i - nct, 0), 0))
    return pl.pallas_call(
        functools.partial(_outproj_kernel, nct=nct, n_x=len(x_args)),
        grid=(n // TM,),
        in_specs=x_specs + [pl.BlockSpec((1, 1, 6, d), lambda i: (layer, mrow(i), 0, 0)),
                  smc, sml, smc, sml, sm, sm, sm, pl.BlockSpec((1, 1, WG), lay3),
                  ydc_spec(0), ydc_spec(1), ydl_spec(0), ydl_spec(1), sm,
                  pl.BlockSpec((1, 1, WG), lay3), pl.BlockSpec((1,) + wglu.shape[1:], lay3),
                  pl.BlockSpec((1,) + wout.shape[1:], lay3), pl.BlockSpec((1, 1, d), lay3)],
        out_specs=pl.BlockSpec((TM, d), row),
        out_shape=jax.ShapeDtypeStruct((n, d), F32),
        compiler_params=_cparams(("parallel",)),
        name="out_proj",
    )(*x_args, mod_all, ya[0], ya[1], yb[0], yb[1], ycf, ycb, cz, cnorm, ydc, ydc, ydl, ydl, du, dd,
      wglu, wout, g_post1)


def _ffn_kernel(x_ref, xp_ref, xn_ref, mod_ref, g_ref, wup_ref, cw_ref, cb_ref, wdn_ref, gp_ref,
                *rest, f, fc, nct, tpc, tpl, split):
    if split:
        oc_ref, ol_ref, hext_ref, hbf_ref, act_ref = rest
    else:
        o_ref, hext_ref, hbf_ref, act_ref = rest
    i = pl.program_id(0)
    is_lat = i >= nct
    pos = jnp.where(is_lat, jnp.maximum(i - nct, 0) % tpl, i % tpc)
    nper = jnp.where(is_lat, tpl, tpc)
    mod = mod_ref[0, 0]

    def hfun(x):
        return _rms(x, g_ref[0]) * (1.0 + mod[4:5]) + mod[3:4]

    x = x_ref[...]
    hext_ref[0:SUBLANES, :] = jnp.where(pos > 0, hfun(xp_ref[...]), 0.0)
    hext_ref[SUBLANES:SUBLANES + TM, :] = hfun(x)
    hext_ref[SUBLANES + TM:2 * SUBLANES + TM, :] = jnp.where(pos < nper - 1, hfun(xn_ref[...]), 0.0)
    hbf_ref[...] = hext_ref[...].astype(BF16)
    rows = TM + 2 * SUBLANES

    def conv(u, c0):
        cw = cw_ref[0, :, c0:c0 + fc]
        um1 = pltpu.roll(u, 1, 0)[SUBLANES:SUBLANES + TM]
        up1 = pltpu.roll(u, rows - 1, 0)[SUBLANES:SUBLANES + TM]
        return (um1 * cw[0:1] + u[SUBLANES:SUBLANES + TM] * cw[1:2] + up1 * cw[2:3]
                + cb_ref[0, :, c0:c0 + fc])

    for j in range(f // fc):
        ug = jnp.dot(hbf_ref[...], wup_ref[0, :, j * fc:(j + 1) * fc], preferred_element_type=F32)
        uv = jnp.dot(hbf_ref[...], wup_ref[0, :, f + j * fc:f + (j + 1) * fc], preferred_element_type=F32)
        act_ref[:, j * fc:(j + 1) * fc] = (_silu(conv(ug, j * fc)) * conv(uv, f + j * fc)).astype(BF16)
    acc = jnp.dot(act_ref[...], wdn_ref[0], preferred_element_type=F32)
    out = x + mod[5:6] * _rms(acc, gp_ref[0])
    if split:
        @pl.when(i < nct)
        def _():
            oc_ref[...] = out

        @pl.when(i >= nct)
        def _():
            ol_ref[...] = out
    else:
        o_ref[...] = out


def _ffn(x_all, mod_all, g_pre2, wup, cw, cb, wdn, g_post2, *, layer, nct, tpc, tpl, split):
    n, d = x_all.shape
    f = wdn.shape[1]
    fc = 256
    assert f % fc == 0
    sub = TM // SUBLANES
    nb8 = n // SUBLANES
    row = lambda i: (i, 0)
    lay3 = lambda i: (layer, 0, 0)
    mrow = _mod_row(nct, tpl)
    single = dict(pipeline_mode=pl.Buffered(1))
    if split:
        out_specs = [pl.BlockSpec((TM, d), lambda i: (jnp.minimum(i, nct - 1), 0)),
                     pl.BlockSpec((TM, d), lambda i: (jnp.maximum(i - nct, 0), 0))]
        out_shape = [jax.ShapeDtypeStruct((nct * TM, d), F32), jax.ShapeDtypeStruct((n - nct * TM, d), F32)]
    else:
        out_specs = pl.BlockSpec((TM, d), row)
        out_shape = jax.ShapeDtypeStruct((n, d), F32)
    return pl.pallas_call(
        functools.partial(_ffn_kernel, f=f, fc=fc, nct=nct, tpc=tpc, tpl=tpl, split=split),
        grid=(n // TM,),
        in_specs=[pl.BlockSpec((TM, d), row),
                  pl.BlockSpec((SUBLANES, d), lambda i: (jnp.maximum(i * sub - 1, 0), 0)),
                  pl.BlockSpec((SUBLANES, d), lambda i: (jnp.minimum((i + 1) * sub, nb8 - 1), 0)),
                  pl.BlockSpec((1, 1, 6, d), lambda i: (layer, mrow(i), 0, 0)),
                  pl.BlockSpec((1, 1, d), lay3),
                  pl.BlockSpec((1,) + wup.shape[1:], lay3, **single),
                  pl.BlockSpec((1,) + cw.shape[1:], lay3),
                  pl.BlockSpec((1,) + cb.shape[1:], lay3),
                  pl.BlockSpec((1,) + wdn.shape[1:], lay3, **single),
                  pl.BlockSpec((1, 1, d), lay3)],
        out_specs=out_specs,
        out_shape=out_shape,
        scratch_shapes=[pltpu.VMEM((TM + 2 * SUBLANES, d), F32), pltpu.VMEM((TM + 2 * SUBLANES, d), BF16),
                        pltpu.VMEM((TM, f), BF16)],
        compiler_params=_cparams(("arbitrary",)),
        name="conv_ffn",
    )(x_all, x_all, x_all, mod_all, g_pre2, wup, cw, cb, wdn, g_post2)


def _ssd_expand(st):
    z = jnp.zeros_like(st)
    hpg = N_HEADS // C_GROUPS
    left = jnp.concatenate([st[..., :hpg, :, :], z[..., hpg:, :, :]], axis=-3)
    right = jnp.concatenate([z[..., :hpg, :, :], st[..., hpg:, :, :]], axis=-3)
    return jnp.concatenate([left, right], axis=-1).reshape(st.shape[:-3] + (WG, 2 * C_STATE))


def _ssd_extract(s):
    s = s.reshape(s.shape[:-2] + (N_HEADS, HEAD_DIM, C_GROUPS, C_STATE))
    hpg = N_HEADS // C_GROUPS
    return jnp.stack([s[..., h, :, h // hpg, :] for h in range(N_HEADS)], axis=-3)


def kernel(x_prompt, x_sample, cache_a_k, cache_a_v, cache_b_k, cache_b_v, state_ssd, state_s5,
           c, c_ctx, w_mod, b_mod, g_pre1, g_post1, g_pre2, g_post2, w_in, a_lam, a_subln,
           b_qnorm, b_knorm, c_conv_w, c_conv_b, c_dt_bias, c_a_log, c_d, c_norm,
           d_lam_re, d_lam_im, d_log_step, d_b, d_c, d_d, d_glu, w_out, w_up,
           ffn_conv_w, ffn_conv_b, w_down):
    b1, l1, d = x_prompt.shape
    b2, l2, _ = x_sample.shape
    depth = w_mod.shape[0]
    past = cache_a_k.shape[2]
    n1, n2 = b1 * l1, b2 * l2
    assert d == 4 * WG and l1 % TM == 0 and l2 % TM == 0 and n1 % l2 == 0
    assert b1 % SUBLANES == 0 and SUBLANES % b2 == 0 and 1 + b2 <= SUBLANES
    nct, tpc, tpl = n1 // TM, l1 // TM, l2 // TM
    fold = SUBLANES // b2

    x_all = (x_prompt.reshape(n1, d), x_sample.reshape(n2, d))
    cond8 = jnp.concatenate([c_ctx[None, :], c, jnp.zeros((SUBLANES - 1 - b2, d), F32)], axis=0)
    mod_all = _modulation(cond8, w_mod, b_mod).reshape(depth, SUBLANES, 6, d)

    vec = lambda t: t.reshape(depth, 1, t.shape[-1])
    tile_h = lambda g: jnp.tile(g, (1, N_HEADS)).reshape(depth, 1, WG)
    hd = HEAD_DIM
    wcol = lambda a, b: w_in[:, :, a:b].astype(BF16)
    kv_dup = lambda o: [wcol(o, o + hd), wcol(o, o + hd), wcol(o + hd, o + 2 * hd), wcol(o + hd, o + 2 * hd)]
    w_in_p = jnp.concatenate([wcol(0, 1024)] + kv_dup(1024) + kv_dup(1152)
                             + [wcol(1280, 2048), wcol(2056, 2312), wcol(2048, 2056),
                                jnp.zeros((depth, d, 120), BF16)], axis=2)
    w_up16, w_dn16, w_out16, w_glu16 = (t.astype(BF16) for t in (w_up, w_down, w_out, d_glu))
    rt, ct = _rope_tables(l2)
    ones_blk = jnp.asarray(np.kron(np.eye(N_HEADS, dtype=np.float32),
                                   np.full((HEAD_DIM, HEAD_DIM), 1.0 / HEAD_DIM, np.float32))).astype(BF16)
    rep_kv = lambda t: jnp.repeat(t, 2, axis=-2).reshape(*t.shape[:-2], WG)
    cak = cache_a_k.reshape(b2, depth, past, WG).astype(BF16)
    cav = cache_a_v.reshape(b2, depth, past, WG).astype(BF16)
    cbk = rep_kv(cache_b_k).astype(BF16)
    cbv = rep_kv(cache_b_v).astype(BF16)
    qn, kn, subln = tile_h(b_qnorm), tile_h(b_knorm), tile_h(a_subln)
    h0_lat = jnp.transpose(_ssd_expand(state_ssd), (1, 2, 0, 3, 4))
    h0_ssd = jnp.concatenate([jnp.zeros((depth, 2, b1, WG, 2 * C_STATE), F32), h0_lat], axis=2)
    dir_lanes = lambda t: jnp.concatenate(
        [jnp.stack([t[:, 0], jnp.zeros_like(t[:, 0])], axis=1), jnp.stack([jnp.zeros_like(t[:, 1]), t[:, 1]], axis=1),
         jnp.zeros((depth, 2, 128 - 2 * N_HEADS), F32)], axis=-1)[:, :, None, :]
    dtb = dir_lanes(c_dt_bias)
    av = dir_lanes(-jnp.exp(c_a_log))
    dsk = jnp.repeat(c_d, HEAD_DIM, axis=-1).reshape(depth, 1, WG)
    pr = _s5_params(d_lam_re, d_lam_im, d_log_step, d_b, d_c)
    wb1, wc1, ar1, ai1 = _s5_weights(*pr, 1, SUBLANES)
    wb2, wc2, ar2, ai2 = _s5_weights(*pr, fold, b2)
    gpr = D_GROUPS // fold
    zero1 = jnp.zeros((2, b1 // SUBLANES, SUBLANES, S5_LANES // 2), F32)
    h0s = state_s5.reshape(b2, depth, 2, fold, gpr * D_STATE, 2)
    h0s = jnp.transpose(h0s, (1, 2, 3, 0, 4, 5)).reshape(depth, 2, 1, SUBLANES, gpr * D_STATE, 2)

    new_ak, new_av, new_bk, new_bv, new_ssd, new_s5 = [], [], [], [], [], []
    y_prompt = y_sample = None
    for l in range(depth):
        lam_init = 0.8 - 0.6 * math.exp(-0.3 * l)
        qa, ka, va, qb, kb, vb, ka32, va32, kb32, vb32, cz, cx, du, cdt = _in_proj(
            x_all, mod_all, vec(g_pre1), w_in_p, qn, kn, ones_blk, rt, ct, layer=l, nct=nct, tpl=tpl)

        attn_c = functools.partial(_attention, n_seq=b1, seq_len=l1, row_off=0, tq=TM, ck=1024, layer=l)
        attn_l = functools.partial(_attention, n_seq=b2, seq_len=l2, row_off=n1, tq=TQ_LATENT, ck=1024, layer=l)
        diff_kw = dict(diff=True, lam=a_lam, subln=subln, lam_init=lam_init)
        ya = (attn_c(qa, ka, va, **diff_kw), attn_l(qa, ka, va, ctx=(cak, cav), **diff_kw))
        yb = (attn_c(qb, kb, vb, diff=False), attn_l(qb, kb, vb, diff=False, ctx=(cbk, cbv)))

        ycf, ycb, st_f, st_b = _ssd(cx, cdt, h0_ssd[l], c_conv_w, vec(c_conv_b), dtb, av, dsk, layer=l,
                                    npc=n1 // SSD_PAIR, ppc=l1 // SSD_PAIR, ppl=l2 // SSD_PAIR, n_ctx_seq=b1)
        new_ssd.append(jnp.stack([_ssd_extract(st_f[:b1]), _ssd_extract(st_b[:b1])], axis=1))

        ydc, s1r, s1i = _s5_scan(du, wb1, ar1, ai1, wc1, zero1, zero1, layer=l, n_b=b1, seq_len=l1,
                                 row_off=0, tt=min(l1, 128), fold=1)
        ydl, _, _ = _s5_scan(du, wb2, ar2, ai2, wc2, h0s[l, ..., 0], h0s[l, ..., 1], layer=l, n_b=b2,
                             seq_len=l2, row_off=n1, tt=min(l2, 512), fold=fold)
        st1 = jnp.stack([s1r, s1i], axis=-1).reshape(2, b1, D_GROUPS, D_STATE, 2)
        new_s5.append(jnp.transpose(st1, (1, 0, 2, 3, 4)))

        x_all = _out_proj(x_all, mod_all, ya, yb, ycf, ycb, cz, vec(c_norm), ydc.reshape(2, n1, WG),
                          ydl.reshape(2, n2, WG), du, vec(d_d), w_glu16, w_out16, vec(g_post1),
                          layer=l, nct=nct, tpl=tpl)
        last = l == depth - 1
        res = _ffn(x_all, mod_all, vec(g_pre2), w_up16, ffn_conv_w, vec(ffn_conv_b), w_dn16, vec(g_post2),
                   layer=l, nct=nct, tpc=tpc, tpl=tpl, split=last)
        if last:
            y_prompt, y_sample = res[0].reshape(b1, l1, d), res[1].reshape(b2, l2, d)
        else:
            x_all = res

        new_ak.append(ka32[:n1].reshape(b1, l1, N_HEADS, HEAD_DIM))
        new_av.append(va32[:n1].reshape(b1, l1, N_HEADS, HEAD_DIM))
        pick = lambda t: t[:n1].reshape(b1, l1, N_HEADS, HEAD_DIM)[:, :, ::2, :]
        new_bk.append(pick(kb32))
        new_bv.append(pick(vb32))

    st = lambda xs: jnp.stack(xs, axis=1)
    return (y_prompt, y_sample, st(new_ak), st(new_av), st(new_bk), st(new_bv), st(new_ssd), st(new_s5))
```

```python
import functools
import math

import numpy as np
import jax
import jax.numpy as jnp
from jax import lax
from jax.experimental import pallas as pl
from jax.experimental.pallas import tpu as pltpu

F32 = jnp.float32
BF16 = jnp.bfloat16

EPS = 1e-6
LOG2E = math.log2(math.e)
GRID_W = 64
ROPE_BASE = 10000.0
HEAD_DIM = 64
A_HALF = HEAD_DIM // 2
N_HEADS = 4
WG = N_HEADS * HEAD_DIM
C_GROUPS = 2
C_STATE = 64
C_CHUNK = 128
D_GROUP = 16
D_GROUPS = WG // D_GROUP
D_STATE = 64
S5_LANES = D_GROUPS * D_STATE * 2

TM = 256
TQ_LATENT = 512
SUBLANES = 8
VMEM_LIMIT = 56 * 1024 * 1024

_C_AQ, _C_AK, _C_AV, _C_BQ, _C_BK, _C_BV, _C_CZ, _C_CX, _C_DU, _C_DT, _C_END = (
    0, 256, 512, 768, 1024, 1280, 1536, 1792, 2304, 2560, 2688)


def _sigmoid(x):
    return 1.0 / (1.0 + jnp.exp(-x))


def _silu(x):
    return x * _sigmoid(x)


def _cparams(sem):
    return pltpu.CompilerParams(dimension_semantics=sem, vmem_limit_bytes=VMEM_LIMIT)


def _lane_mask(width, lo, hi):
    lane = lax.broadcasted_iota(jnp.int32, (1, width), 1)
    return (lane >= lo) & (lane < hi)


def _rms(x, g):
    return (x * lax.rsqrt(jnp.mean(x * x, axis=-1, keepdims=True) + EPS)) * g


def _mod_row(nct, tpl):
    return lambda i: jnp.where(i < nct, 0, 1 + jnp.maximum(i - nct, 0) // tpl)


def _mod_kernel(c_ref, w_ref, b_ref, o_ref):
    s = _silu(c_ref[...])
    o_ref[0] = jnp.dot(s.astype(BF16), w_ref[0].astype(BF16), preferred_element_type=F32) + b_ref[0]


def _modulation(cond8, w_mod, b_mod):
    depth, d, n = w_mod.shape
    tn = 1536
    return pl.pallas_call(
        _mod_kernel,
        grid=(depth, n // tn),
        in_specs=[pl.BlockSpec((SUBLANES, d), lambda l, j: (0, 0)),
                  pl.BlockSpec((1, d, tn), lambda l, j: (l, 0, j)),
                  pl.BlockSpec((1, 1, tn), lambda l, j: (l, 0, j))],
        out_specs=pl.BlockSpec((1, SUBLANES, tn), lambda l, j: (l, 0, j)),
        out_shape=jax.ShapeDtypeStruct((depth, SUBLANES, n), F32),
        compiler_params=_cparams(("parallel", "parallel")),
        name="modulation",
    )(cond8, w_mod, b_mod.reshape(depth, 1, n))


def _x_specs(x, nct):
    if isinstance(x, tuple):
        d = x[0].shape[1]
        return [pl.BlockSpec((TM, d), lambda i: (jnp.minimum(i, nct - 1), 0)),
                pl.BlockSpec((TM, d), lambda i: (jnp.maximum(i - nct, 0), 0))], list(x)
    return [pl.BlockSpec((TM, x.shape[1]), lambda i: (i, 0))], [x]


def _load_x(x_refs, nct):
    if len(x_refs) == 2:
        return jnp.where(pl.program_id(0) < nct, x_refs[0][...], x_refs[1][...])
    return x_refs[0][...]


def _inproj_kernel(*refs, nct, n_x):
    x_refs = refs[:n_x]
    (mod_ref, g_ref, w_ref, qn_ref, kn_ref, ones_ref, rt_ref, ct_ref,
     qa_ref, ka_ref, va_ref, qb_ref, kb_ref, vb_ref,
     ka32_ref, va32_ref, kb32_ref, vb32_ref, cz_ref, cx_ref, du_ref, dt_ref) = refs[n_x:]
    i = pl.program_id(0)
    mod = mod_ref[0, 0]
    h = _rms(_load_x(x_refs, nct), g_ref[0]) * (1.0 + mod[1:2]) + mod[0:1]
    p = jnp.dot(h.astype(BF16), w_ref[0], preferred_element_type=F32)
    va_ref[...] = p[:, _C_AV:_C_BQ].astype(BF16)
    vb_ref[...] = p[:, _C_BV:_C_CZ].astype(BF16)
    kv_rows = lambda t: jnp.concatenate([t[0:HEAD_DIM], t[2 * HEAD_DIM:3 * HEAD_DIM]], axis=0)
    va32_ref[0] = p[:, _C_AV:_C_BQ].T
    vb32_ref[0] = kv_rows(p[:, _C_BV:_C_CZ].T)
    cz_ref[...] = p[:, _C_CZ:_C_CX]
    cx_ref[...] = p[:, _C_CX:_C_DU]
    du_ref[...] = p[:, _C_DU:_C_DT]
    dt_ref[...] = p[:, _C_DT:_C_END]

    def headnorm(t, gain):
        sq = t * t
        hi = sq.astype(BF16)
        lo = (sq - hi.astype(F32)).astype(BF16)
        ms = (jnp.dot(hi, ones_ref[...], preferred_element_type=F32)
              + jnp.dot(lo, ones_ref[...], preferred_element_type=F32))
        return (t * lax.rsqrt(ms + EPS)) * gain

    aq = p[:, _C_AQ:_C_AK] * (A_HALF ** -0.5 * LOG2E)
    ak = p[:, _C_AK:_C_AV]
    bq = headnorm(p[:, _C_BQ:_C_BK], qn_ref[0]) * (HEAD_DIM ** -0.5 * LOG2E)
    bk = headnorm(p[:, _C_BK:_C_BV], kn_ref[0])
    ka32_ref[0] = ak.T
    kb32_ref[0] = kv_rows(bk.T)

    rows = TM // GRID_W
    is_lat = i >= nct
    rt = rt_ref[0]
    lane = lax.broadcasted_iota(jnp.int32, (1, WG), 1)
    row_lane_a = (lane % (A_HALF // 2)) < A_HALF // 4
    row_lane_b = (lane % (HEAD_DIM // 2)) < HEAD_DIM // 4

    def table(k, row_lane):
        ctk = jnp.where(is_lat, ct_ref[k], 1.0 if k % 3 == 0 else 0.0)
        parts = [jnp.where(row_lane, rt[k, r:r + 1, :], ctk) for r in range(rows)]
        return jnp.concatenate(parts, axis=0)

    def rope(t, base, row_lane, dist):
        return (t * table(base, row_lane)
                + pltpu.roll(t, WG - dist, 1) * table(base + 1, row_lane)
                + pltpu.roll(t, dist, 1) * table(base + 2, row_lane))

    qa_ref[...] = rope(aq, 0, row_lane_a, A_HALF // 2).astype(BF16)
    ka_ref[...] = rope(ak, 0, row_lane_a, A_HALF // 2).astype(BF16)
    qb_ref[...] = rope(bq, 3, row_lane_b, HEAD_DIM // 2).astype(BF16)
    kb_ref[...] = rope(bk, 3, row_lane_b, HEAD_DIM // 2).astype(BF16)


def _rope_tables(l2):
    rows = l2 // GRID_W
    lane = np.arange(WG)

    def one(block, n):
        freqs = (np.float32(ROPE_BASE) ** (-np.arange(n, dtype=np.float32) / np.float32(n))).astype(np.float32)
        p = lane % block
        idx = p % (block // 2)
        first = p < (block // 2)
        f = freqs[idx % n]

        def tabs(pos):
            ang = (pos[:, None].astype(np.float32) * f[None, :]).astype(np.float32)
            c, s = np.cos(ang).astype(np.float32), np.sin(ang).astype(np.float32)
            return [c, np.where(first[None, :], -s, 0.0).astype(np.float32),
                    np.where(first[None, :], 0.0, s).astype(np.float32)]

        return tabs(np.arange(rows)), tabs(np.arange(GRID_W))

    ra, ca = one(A_HALF, A_HALF // 4)
    rb, cb = one(HEAD_DIM, HEAD_DIM // 4)
    rt = np.stack(ra + rb, axis=0)
    ct = np.stack(ca + cb, axis=0)
    rpt = TM // GRID_W
    npt = l2 // TM
    rt = rt.reshape(6, npt, rpt, WG).transpose(1, 0, 2, 3)
    rt = np.concatenate([rt, np.zeros((npt, 6, SUBLANES - rpt, WG), np.float32)], axis=2)
    ident = np.zeros((1, 6, SUBLANES, WG), np.float32)
    ident[0, 0::3] = 1.0
    return jnp.asarray(np.concatenate([rt, ident], axis=0)), jnp.asarray(ct)


def _in_proj(x_all, mod_all, g_pre1, w_in_p, qn, kn, ones_blk, rt, ct, *, layer, nct, tpl):
    x_specs, x_args = _x_specs(x_all, nct)
    n = sum(a.shape[0] for a in x_args)
    d = x_args[0].shape[1]
    nt = n // TM
    row = lambda i: (i, 0)
    const2 = lambda i: (0, 0)
    lay3 = lambda i: (layer, 0, 0)
    mrow = _mod_row(nct, tpl)
    ctx_blk = lambda i: (jnp.minimum(i, nct), 0, 0)
    kvw = WG // 2
    out_specs = ([pl.BlockSpec((TM, WG), row)] * 6
                 + [pl.BlockSpec((1, w, TM), ctx_blk) for w in (WG, WG, kvw, kvw)]
                 + [pl.BlockSpec((TM, w), row) for w in (WG, 2 * WG, WG, 128)])
    out_shape = ([jax.ShapeDtypeStruct((n, WG), BF16)] * 6
                 + [jax.ShapeDtypeStruct((nct + 1, w, TM), F32) for w in (WG, WG, kvw, kvw)]
                 + [jax.ShapeDtypeStruct((n, w), F32) for w in (WG, 2 * WG, WG, 128)])
    return pl.pallas_call(
        functools.partial(_inproj_kernel, nct=nct, n_x=len(x_args)),
        grid=(nt,),
        in_specs=x_specs + [
                  pl.BlockSpec((1, 1, 6, d), lambda i: (layer, mrow(i), 0, 0)),
                  pl.BlockSpec((1, 1, d), lay3),
                  pl.BlockSpec((1,) + w_in_p.shape[1:], lay3),
                  pl.BlockSpec((1, 1, WG), lay3),
                  pl.BlockSpec((1, 1, WG), lay3),
                  pl.BlockSpec((WG, WG), const2),
                  pl.BlockSpec((1, 6, SUBLANES, WG),
                               lambda i: (jnp.where(i < nct, tpl, jnp.maximum(i - nct, 0) % tpl), 0, 0, 0)),
                  pl.BlockSpec(ct.shape, lambda i: (0, 0, 0))],
        out_specs=out_specs,
        out_shape=out_shape,
        compiler_params=_cparams(("arbitrary",)),
        name="in_proj",
    )(*x_args, mod_all, g_pre1, w_in_p, qn, kn, ones_blk, rt, ct)


def _attn_kernel(*refs, diff, tq, lk_new, lc, ck, lam_init):
    it = iter(refs)
    q_ref, k_ref, v_ref = next(it), next(it), next(it)
    kc_ref = vc_ref = None
    if lc:
        kc_ref, vc_ref = next(it).at[0, 0], next(it).at[0, 0]
    lam_ref = sub_ref = None
    if diff:
        lam_ref, sub_ref = next(it), next(it)
    o_ref = next(it)
    s_ref = (next(it), next(it))
    m_ref = (next(it), next(it))
    oall_ref = next(it)

    blocks = [(k_ref, v_ref, r0, min(ck, lk_new - r0), r0) for r0 in range(0, lk_new, ck)]
    if lc:
        blocks.append((kc_ref, vc_ref, 0, lc, lk_new))

    n_streams = 2 * N_HEADS if diff else N_HEADS
    width = A_HALF if diff else HEAD_DIM
    q = q_ref[...]
    lane = lax.broadcasted_iota(jnp.int32, (1, WG), 1)

    def step(i, slot, scores, values):
        if scores:
            lo = i * width
            qm = jnp.where((lane >= lo) & (lane < lo + width), q, jnp.zeros_like(q))
            s_cur = s_ref[slot]
            mrun = jnp.full((tq, 128), -jnp.inf, F32)
        if values:
            s_prev = s_ref[1 - slot]
            m = jnp.max(m_ref[1 - slot][...], axis=-1, keepdims=True)
            lrun = jnp.zeros((tq, 128), F32)
            acc = jnp.zeros((tq, WG), F32)
        for kr, vr, r0, rows, c0 in blocks:
            if scores:
                s = lax.dot_general(qm, kr[r0:r0 + rows, :], (((1,), (1,)), ((), ())),
                                    preferred_element_type=F32)
                s_cur[:, c0:c0 + rows] = s
                for t in range(rows // 128):
                    mrun = jnp.maximum(mrun, s[:, t * 128:(t + 1) * 128])
            if values:
                p = jnp.exp2(s_prev[:, c0:c0 + rows] - m)
                for t in range(rows // 128):
                    lrun = lrun + p[:, t * 128:(t + 1) * 128]
                acc = acc + jnp.dot(p.astype(BF16), vr[r0:r0 + rows, :], preferred_element_type=F32)
        if scores:
            m_ref[slot][...] = mrun
        if values:
            oall_ref[i - 1] = acc * (1.0 / jnp.sum(lrun, axis=-1, keepdims=True))

    step(0, 0, True, False)

    def body(g, carry):
        step(2 * g + 1, 1, True, True)
        step(2 * g + 2, 0, True, True)
        return carry

    lax.fori_loop(0, (n_streams - 2) // 2, body, 0)
    step(n_streams - 1, 1, True, True)
    step(n_streams, 0, False, True)

    out = jnp.zeros((tq, WG), F32)
    if diff:
        lp = lam_ref[0]
        lam = (jnp.exp(jnp.sum(lp[0:1] * lp[1:2], axis=-1, keepdims=True))
               - jnp.exp(jnp.sum(lp[2:3] * lp[3:4], axis=-1, keepdims=True)) + lam_init)
    for h in range(N_HEADS):
        hm = _lane_mask(WG, h * HEAD_DIM, (h + 1) * HEAD_DIM)
        if diff:
            o = oall_ref[2 * h] - lam * oall_ref[2 * h + 1]
            ms = jnp.sum(jnp.where(hm, o * o, 0.0), axis=-1, keepdims=True) * (1.0 / HEAD_DIM)
            o = ((o * lax.rsqrt(ms + EPS)) * sub_ref[0]) * (1.0 - lam_init)
        else:
            o = oall_ref[h]
        out = jnp.where(hm, o, out)
    o_ref[...] = out


def _attention(q, k, v, *, diff, n_seq, seq_len, row_off, tq, ck, layer, ctx=None,
               lam=None, subln=None, lam_init=0.0):
    assert row_off % seq_len == 0 and seq_len % tq == 0
    qpb = seq_len // tq
    q_map = lambda s, j: (row_off // tq + s * qpb + j, 0)
    kv_map = lambda s, j: (row_off // seq_len + s, 0)
    in_specs = [pl.BlockSpec((tq, WG), q_map), pl.BlockSpec((seq_len, WG), kv_map),
                pl.BlockSpec((seq_len, WG), kv_map)]
    args = [q, k, v]
    lc = 0
    if ctx is not None:
        kc, vc = ctx
        lc = kc.shape[2]
        cmap = lambda s, j: (s, layer, 0, 0)
        in_specs += [pl.BlockSpec((1, 1, lc, WG), cmap), pl.BlockSpec((1, 1, lc, WG), cmap)]
        args += [kc, vc]
    if diff:
        lay3 = lambda s, j: (layer, 0, 0)
        in_specs += [pl.BlockSpec((1,) + lam.shape[1:], lay3), pl.BlockSpec((1, 1, WG), lay3)]
        args += [lam, subln]
    lk = seq_len + lc
    return pl.pallas_call(
        functools.partial(_attn_kernel, diff=diff, tq=tq, lk_new=seq_len, lc=lc, ck=min(ck, seq_len),
                          lam_init=lam_init),
        grid=(n_seq, qpb),
        in_specs=in_specs,
        out_specs=pl.BlockSpec((tq, WG), lambda s, j: (s * qpb + j, 0)),
        out_shape=jax.ShapeDtypeStruct((n_seq * seq_len, WG), F32),
        scratch_shapes=[pltpu.VMEM((tq, lk), F32), pltpu.VMEM((tq, lk), F32),
                        pltpu.VMEM((tq, 128), F32), pltpu.VMEM((tq, 128), F32),
                        pltpu.VMEM((2 * N_HEADS if diff else N_HEADS, tq, WG), F32)],
        compiler_params=_cparams(("parallel", "parallel")),
        name="attn_diff" if diff else "attn_gqa",
    )(*args)


SSD_PAIR = 2 * C_CHUNK


def _ssd_intra(xbc, dt_raw, dtb, av, *, reverse):
    q = C_CHUNK
    xs = xbc[:, 0:WG]
    bm = xbc[:, WG:WG + 128].astype(BF16)
    cm = xbc[:, WG + 128:WG + 256]
    raw = dt_raw + dtb
    dt = jnp.maximum(raw, 0.0) + jnp.log1p(jnp.exp(-jnp.abs(raw)))
    dta = dt * av
    li = lax.broadcasted_iota(jnp.int32, (q, q), 0)
    si = lax.broadcasted_iota(jnp.int32, (q, q), 1)
    causal = (si >= li) if reverse else (si <= li)
    cum = jnp.dot(causal.astype(F32), dta, preferred_element_type=F32, precision=lax.Precision.HIGHEST)
    cum_t = cum.T
    end = 0 if reverse else q - 1
    d0 = N_HEADS if reverse else 0
    rowh = lax.broadcasted_iota(jnp.int32, (WG, 1), 0) // HEAD_DIM

    y = jnp.zeros((q, WG), F32)
    xw = jnp.zeros((q, WG), F32)
    ecum = jnp.zeros((q, WG), F32)
    cdec = jnp.zeros((WG, 1), F32)
    gmat = None
    for h in range(N_HEADS):
        j = d0 + h
        g = h // (N_HEADS // C_GROUPS)
        if h % (N_HEADS // C_GROUPS) == 0:
            cg = jnp.where(_lane_mask(128, g * C_STATE, (g + 1) * C_STATE), cm, 0.0).astype(BF16)
            gmat = lax.dot_general(cg, bm, (((1,), (1,)), ((), ())), preferred_element_type=F32)
        col = jnp.broadcast_to(cum[:, j:j + 1], (q, 128))
        dtc = jnp.broadcast_to(dt[:, j:j + 1], (q, 128))
        wide = lambda a: jnp.concatenate([a, a], axis=1)
        seg = col - cum_t[j:j + 1, :]
        decay = jnp.where(causal, jnp.exp(jnp.where(causal, seg, 0.0)), 0.0)
        hm = _lane_mask(WG, h * HEAD_DIM, (h + 1) * HEAD_DIM)
        xdt = jnp.where(hm, xs * wide(dtc), 0.0)
        y = y + jnp.dot((gmat * decay).astype(BF16), xdt.astype(BF16), preferred_element_type=F32)
        cend = cum[end:end + 1, j:j + 1]
        xw = xw + xdt * wide(jnp.exp(cend - col))
        ecum = jnp.where(hm, wide(jnp.exp(col)), ecum)
        cdec = jnp.where(rowh == h, jnp.exp(cend), cdec)

    st = lax.dot_general(xw.astype(BF16), bm, (((0,), (0,)), ((), ())), preferred_element_type=F32)
    return y, xs, cm.astype(BF16), ecum, cdec, st


def _ssd_direction(x_ref, xp_ref, xn_ref, dt_ref, h0_ref, cw_ref, cb_ref, dtb_ref, av_ref, s_ref,
                   y_ref, st_ref, skip, p, *, reverse, npc, ppc, ppl):
    q = C_CHUNK
    is_lat = p >= npc
    pos = jnp.where(is_lat, jnp.maximum(p - npc, 0) % ppl, p % ppc)
    nper = jnp.where(is_lat, ppl, ppc)
    enter = (pos == nper - 1) if reverse else (pos == 0)

    @pl.when(enter)
    def _():
        s_ref[...] = h0_ref[0, 0]

    x = x_ref[...]
    rid = lax.broadcasted_iota(jnp.int32, (SSD_PAIR, 1), 0)
    prow = jnp.where(pos > 0, xp_ref[SUBLANES - 1:SUBLANES, :], 0.0)
    nrow = jnp.where(pos < nper - 1, xn_ref[0:1, :], 0.0)
    xm1 = jnp.where(rid == 0, prow, pltpu.roll(x, 1, 0))
    xp1 = jnp.where(rid == SSD_PAIR - 1, nrow, pltpu.roll(x, SSD_PAIR - 1, 0))
    cw = cw_ref[0]
    xbc = _silu(xm1 * cw[0:1] + x * cw[1:2] + xp1 * cw[2:3] + cb_ref[0])

    parts = [_ssd_intra(xbc[k * q:(k + 1) * q], dt_ref[k * q:(k + 1) * q, :], dtb_ref[...], av_ref[...],
                        reverse=reverse) for k in range(2)]
    rowh = lax.broadcasted_iota(jnp.int32, (WG, 1), 0) // HEAD_DIM
    colg = lax.broadcasted_iota(jnp.int32, (1, 128), 1) // C_STATE
    blk = (rowh // (N_HEADS // C_GROUPS)) == colg
    s = s_ref[...]
    for k in ((1, 0) if reverse else (0, 1)):
        y, xs, cm, ecum, cdec, st = parts[k]
        s_msk = jnp.where(blk, s, 0.0).astype(BF16)
        y = y + lax.dot_general(cm, s_msk, (((1,), (1,)), ((), ())), preferred_element_type=F32) * ecum
        if skip is not None:
            y = y + skip * xs
        y_ref[k * q:(k + 1) * q, :] = y
        s = s * cdec + st
    s_ref[...] = s
    st_ref[0] = s


def _ssd_kernel(xf_ref, xpf_ref, xnf_ref, dtf_ref, h0f_ref, xb_ref, xpb_ref, xnb_ref, dtr_ref, h0b_ref,
                cw_ref, cb_ref, dtb_ref, av_ref, dsk_ref,
                yf_ref, yb_ref, stf_ref, stb_ref, sf_ref, sb_ref, *, npc, ppc, ppl):
    t = pl.program_id(0)
    kw = dict(npc=npc, ppc=ppc, ppl=ppl)
    _ssd_direction(xf_ref, xpf_ref, xnf_ref, dtf_ref, h0f_ref, cw_ref, cb_ref, dtb_ref.at[0, 0], av_ref.at[0, 0],
                   sf_ref, yf_ref, stf_ref, dsk_ref[0], t, reverse=False, **kw)
    _ssd_direction(xb_ref, xpb_ref, xnb_ref, dtr_ref, h0b_ref, cw_ref, cb_ref, dtb_ref.at[0, 1], av_ref.at[0, 1],
                   sb_ref, yb_ref, stb_ref, None, pl.num_programs(0) - 1 - t, reverse=True, **kw)


def _ssd(cx, cdt, h0, conv_w, conv_b, dtb, av, dsk, *, layer, npc, ppc, ppl, n_ctx_seq):
    n = cx.shape[0]
    nc = n // SSD_PAIR
    n_seq = h0.shape[1]
    sub = SSD_PAIR // SUBLANES
    nb8 = n // SUBLANES
    lay3 = lambda t: (layer, 0, 0)

    def specs(cidx, d):
        def seq_of(t):
            c = cidx(t)
            return jnp.where(c < npc, c // ppc, n_ctx_seq + jnp.maximum(c - npc, 0) // ppl)

        row = lambda t: (cidx(t), 0)
        ins = [pl.BlockSpec((SSD_PAIR, 2 * WG), row),
               pl.BlockSpec((SUBLANES, 2 * WG), lambda t: (jnp.maximum(cidx(t) * sub - 1, 0), 0)),
               pl.BlockSpec((SUBLANES, 2 * WG), lambda t: (jnp.minimum((cidx(t) + 1) * sub, nb8 - 1), 0)),
               pl.BlockSpec((SSD_PAIR, 128), row),
               pl.BlockSpec((1, 1, WG, 128), lambda t: (d, seq_of(t), 0, 0))]
        outs = [pl.BlockSpec((SSD_PAIR, WG), row), pl.BlockSpec((1, WG, 128), lambda t: (seq_of(t), 0, 0))]
        return ins, outs

    ins_f, outs_f = specs(lambda t: t, 0)
    ins_b, outs_b = specs(lambda t: nc - 1 - t, 1)
    return pl.pallas_call(
        functools.partial(_ssd_kernel, npc=npc, ppc=ppc, ppl=ppl),
        grid=(nc,),
        in_specs=ins_f + ins_b + [pl.BlockSpec((1, 3, 2 * WG), lay3),
                                  pl.BlockSpec((1, 1, 2 * WG), lay3),
                                  pl.BlockSpec((1, 2, 1, 128), lambda t: (layer, 0, 0, 0)),
                                  pl.BlockSpec((1, 2, 1, 128), lambda t: (layer, 0, 0, 0)),
                                  pl.BlockSpec((1, 1, WG), lay3)],
        out_specs=[outs_f[0], outs_b[0], outs_f[1], outs_b[1]],
        out_shape=[jax.ShapeDtypeStruct((n, WG), F32), jax.ShapeDtypeStruct((n, WG), F32),
                   jax.ShapeDtypeStruct((n_seq, WG, 128), F32), jax.ShapeDtypeStruct((n_seq, WG, 128), F32)],
        scratch_shapes=[pltpu.VMEM((WG, 128), F32), pltpu.VMEM((WG, 128), F32)],
        compiler_params=_cparams(("arbitrary",)),
        name="ssd",
    )(cx, cx, cx, cdt, h0, cx, cx, cx, cdt, h0, conv_w, conv_b, dtb, av, dsk)


def _s5_kernel(*refs, tt, nl, nb_in, fold):
    u_refs = refs[:nb_in]
    (wb_ref, are_ref, aim_ref, wc_ref, h0r_ref, h0i_ref, y_ref, sr_ref, si_ref,
     ust_ref, ytm_ref, bu_ref, xs_ref, cr_ref, ci_ref) = refs[nb_in:]
    hl = nl // 2
    d = pl.program_id(0)

    @pl.when(pl.program_id(2) == 0)
    def _():
        cr_ref[...] = h0r_ref[0, 0]
        ci_ref[...] = h0i_ref[0, 0]

    lane = lax.broadcasted_iota(jnp.int32, (1, WG), 1)
    gw = WG // fold
    for qq in range(fold):
        for b in range(nb_in):
            c = qq * nb_in + b
            u = u_refs[b][...]
            if fold > 1:
                u = jnp.where((lane >= qq * gw) & (lane < (qq + 1) * gw), u, 0.0)
            for k in range(WG // 128):
                ust_ref[k, pl.ds(c, tt, stride=SUBLANES), :] = u[:, k * 128:(k + 1) * 128]
    ust = jnp.concatenate([ust_ref[k].astype(BF16) for k in range(WG // 128)], axis=1)
    bu_ref[...] = jnp.dot(ust, wb_ref[0, 0], preferred_element_type=F32)
    a_re = are_ref[0, 0]
    a_im = aim_ref[0, 0]

    def step(t, carry):
        xr, xi = carry
        te = t + d * (tt - 1 - 2 * t)
        r0 = pl.multiple_of(te * SUBLANES, SUBLANES)
        nr = a_re * xr - a_im * xi + bu_ref[pl.ds(r0, SUBLANES), 0:hl]
        ni = a_re * xi + a_im * xr + bu_ref[pl.ds(r0, SUBLANES), hl:nl]
        xs_ref[pl.ds(r0, SUBLANES), 0:hl] = nr
        xs_ref[pl.ds(r0, SUBLANES), hl:nl] = ni
        return nr, ni

    xr, xi = lax.fori_loop(0, tt, step, (cr_ref[...], ci_ref[...]), unroll=4)
    cr_ref[...] = xr
    ci_ref[...] = xi
    sr_ref[0, 0] = xr
    si_ref[0, 0] = xi
    y = jnp.dot(xs_ref[...].astype(BF16), wc_ref[0, 0], preferred_element_type=F32)
    for k in range(WG // 128):
        ytm_ref[k] = y[:, k * 128:(k + 1) * 128]

    def chain_rows(c):
        return jnp.concatenate([ytm_ref[k, pl.ds(c, tt, stride=SUBLANES), :] for k in range(WG // 128)], axis=1)

    for b in range(nb_in):
        if fold == 1:
            y_ref[0, b] = chain_rows(b)
        else:
            acc = jnp.zeros((tt, WG), F32)
            for qq in range(fold):
                acc = jnp.where((lane >= qq * gw) & (lane < (qq + 1) * gw), chain_rows(qq * nb_in + b), acc)
            y_ref[0, b] = acc


def _s5_scan(du, wb, a_re, a_im, wc, h0r, h0i, *, layer, n_b, seq_len, row_off, tt, fold):
    nb_in = SUBLANES // fold
    n_slab = n_b // nb_in
    nl = wb.shape[-1]
    hl = nl // 2
    nblk = seq_len // tt
    assert seq_len % tt == 0 and row_off % tt == 0
    tblk = lambda d, j: j + d * (nblk - 1 - 2 * j)
    u_specs = [pl.BlockSpec((tt, WG), functools.partial(
        lambda d, s, j, b: (row_off // tt + (s * nb_in + b) * nblk + tblk(d, j), 0), b=b)) for b in range(nb_in)]
    wmap = lambda d, s, j: (layer, d, 0, 0)
    smap = lambda d, s, j: (d, s, 0, 0)
    st_spec = pl.BlockSpec((1, 1, SUBLANES, hl), smap)
    rows = SUBLANES * tt
    return pl.pallas_call(
        functools.partial(_s5_kernel, tt=tt, nl=nl, nb_in=nb_in, fold=fold),
        grid=(2, n_slab, nblk),
        in_specs=u_specs + [pl.BlockSpec((1, 1, WG, nl), wmap),
                            pl.BlockSpec((1, 1, SUBLANES, hl), wmap),
                            pl.BlockSpec((1, 1, SUBLANES, hl), wmap),
                            pl.BlockSpec((1, 1, nl, WG), wmap),
                            st_spec, st_spec],
        out_specs=[pl.BlockSpec((1, nb_in, tt, WG), lambda d, s, j: (d, s, tblk(d, j), 0)), st_spec, st_spec],
        out_shape=[jax.ShapeDtypeStruct((2, n_b, seq_len, WG), F32),
                   jax.ShapeDtypeStruct((2, n_slab, SUBLANES, hl), F32),
                   jax.ShapeDtypeStruct((2, n_slab, SUBLANES, hl), F32)],
        scratch_shapes=[pltpu.VMEM((WG // 128, rows, 128), F32), pltpu.VMEM((WG // 128, rows, 128), F32),
                        pltpu.VMEM((rows, nl), F32), pltpu.VMEM((rows, nl), F32),
                        pltpu.VMEM((SUBLANES, hl), F32), pltpu.VMEM((SUBLANES, hl), F32)],
        compiler_params=_cparams(("parallel", "parallel", "arbitrary")),
        name="s5_scan",
    )(*([du] * nb_in), wb, a_re, a_im, wc, h0r, h0i)


def _s5_params(lam_re, lam_im, log_step, b_ri, c_ri):
    lam = lax.complex(lam_re, lam_im)
    a_bar = jnp.exp(lam * jnp.exp(log_step)[..., None])
    b_bar = ((a_bar - 1.0) / lam)[..., None] * lax.complex(b_ri[..., 0], b_ri[..., 1])
    return jnp.real(a_bar), jnp.imag(a_bar), jnp.real(b_bar), jnp.imag(b_bar), c_ri[..., 0], c_ri[..., 1]


def _s5_weights(a_re, a_im, bb_re, bb_im, c_re, c_im, fold, chain_rep):
    lead = a_re.shape[:2]
    m = lead[0] * lead[1]
    gpr = D_GROUPS // fold
    eye = jnp.eye(gpr, dtype=F32)

    def wb_part(bb):
        t = bb.reshape(m, fold, gpr, D_STATE, D_GROUP)
        src = jnp.transpose(t, (0, 1, 2, 4, 3))
        w = jnp.where(eye[None, None, :, None, :, None] > 0,
                      jnp.broadcast_to(src[:, :, :, :, None, :], (m, fold, gpr, D_GROUP, gpr, D_STATE)), 0.0)
        return w.reshape(m, WG, gpr * D_STATE)

    wb = jnp.concatenate([wb_part(bb_re), wb_part(bb_im)], axis=-1)

    def wc_part(cc):
        t = cc.reshape(m, fold, gpr, D_GROUP, D_STATE)
        src = jnp.transpose(t, (0, 2, 4, 1, 3))
        w = jnp.where(eye[None, :, None, None, :, None] > 0,
                      jnp.broadcast_to(src[:, :, :, :, None, :], (m, gpr, D_STATE, fold, gpr, D_GROUP)), 0.0)
        return w.reshape(m, gpr * D_STATE, WG)

    wc = jnp.concatenate([wc_part(c_re), -wc_part(c_im)], axis=1)

    def a_rows(a):
        t = a.reshape(m, fold, 1, gpr * D_STATE)
        return jnp.broadcast_to(t, (m, fold, chain_rep, gpr * D_STATE)).reshape(lead + (SUBLANES, gpr * D_STATE))

    nl = 2 * gpr * D_STATE
    return (wb.astype(BF16).reshape(lead + (WG, nl)), wc.astype(BF16).reshape(lead + (nl, WG)),
            a_rows(a_re), a_rows(a_im))


def _outproj_kernel(*refs, nct, n_x):
    x_refs = refs[:n_x]
    (mod_ref, yac_ref, yal_ref, ybc_ref, ybl_ref, ycf_ref, ycb_ref, cz_ref, cn_ref,
     ydc0_ref, ydc1_ref, ydl0_ref, ydl1_ref, du_ref, dd_ref, wglu_ref, wout_ref, g_ref, o_ref) = refs[n_x:]
    mod = mod_ref[0, 0]
    yc = _rms((ycf_ref[...] + ycb_ref[...]) * _silu(cz_ref[...]), cn_ref[0])
    is_ctx = pl.program_id(0) < nct
    ya = jnp.where(is_ctx, yac_ref[...], yal_ref[...])
    yb = jnp.where(is_ctx, ybc_ref[...], ybl_ref[...])
    yd = jnp.where(is_ctx, ydc0_ref[0] + ydc1_ref[0], ydl0_ref[0] + ydl1_ref[0]) + dd_ref[0] * du_ref[...]
    yd = yd * (0.5 * (1.0 + jnp.tanh(math.sqrt(2.0 / math.pi) * (yd + 0.044715 * (yd * yd * yd)))))
    gl = jnp.dot(yd.astype(BF16), wglu_ref[0], preferred_element_type=F32)
    yd = gl[:, 0:WG] * _sigmoid(gl[:, WG:2 * WG])
    cat = jnp.concatenate([ya.astype(BF16), yb.astype(BF16), yc.astype(BF16), yd.astype(BF16)], axis=-1)
    y = jnp.dot(cat, wout_ref[0], preferred_element_type=F32)
    o_ref[...] = _load_x(x_refs, nct) + mod[2:3] * _rms(y, g_ref[0])


def _out_proj(x_all, mod_all, ya, yb, ycf, ycb, cz, cnorm, ydc, ydl, du, dd, wglu, wout, g_post1,
              *, layer, nct, tpl):
    x_specs, x_args = _x_specs(x_all, nct)
    n = sum(a.shape[0] for a in x_args)
    d = x_args[0].shape[1]
    row = lambda i: (i, 0)
    lay3 = lambda i: (layer, 0, 0)
    mrow = _mod_row(nct, tpl)
    sm = pl.BlockSpec((TM, WG), row)
    smc = pl.BlockSpec((TM, WG), lambda i: (jnp.minimum(i, nct - 1), 0))
    sml = pl.BlockSpec((TM, WG), lambda i: (jnp.maximum(i - nct, 0), 0))
    ydc_spec = lambda dd_: pl.BlockSpec((1, TM, WG), lambda i: (dd_, jnp.minimum(i, nct - 1), 0))
    ydl_spec = lambda dd_: pl.BlockSpec((1, TM, WG), lambda i: (dd_, jnp.maximum(i - nct, 0), 0))
    return pl.pallas_call(
        functools.partial(_outproj_kernel, nct=nct, n_x=len(x_args)),
        grid=(n // TM,),
        in_specs=x_specs + [pl.BlockSpec((1, 1, 6, d), lambda i: (layer, mrow(i), 0, 0)),
                  smc, sml, smc, sml, sm, sm, sm, pl.BlockSpec((1, 1, WG), lay3),
                  ydc_spec(0), ydc_spec(1), ydl_spec(0), ydl_spec(1), sm,
                  pl.BlockSpec((1, 1, WG), lay3), pl.BlockSpec((1,) + wglu.shape[1:], lay3),
                  pl.BlockSpec((1,) + wout.shape[1:], lay3), pl.BlockSpec((1, 1, d), lay3)],
        out_specs=pl.BlockSpec((TM, d), row),
        out_shape=jax.ShapeDtypeStruct((n, d), F32),
        compiler_params=_cparams(("parallel",)),
        name="out_proj",
    )(*x_args, mod_all, ya[0], ya[1], yb[0], yb[1], ycf, ycb, cz, cnorm, ydc, ydc, ydl, ydl, du, dd,
      wglu, wout, g_post1)


def _ffn_kernel(x_ref, xp_ref, xn_ref, mod_ref, g_ref, wup_ref, cw_ref, cb_ref, wdn_ref, gp_ref,
                *rest, f, fc, nct, tpc, tpl, split):
    if split:
        oc_ref, ol_ref, hext_ref, hbf_ref, act_ref = rest
    else:
        o_ref, hext_ref, hbf_ref, act_ref = rest
    i = pl.program_id(0)
    is_lat = i >= nct
    pos = jnp.where(is_lat, jnp.maximum(i - nct, 0) % tpl, i % tpc)
    nper = jnp.where(is_lat, tpl, tpc)
    mod = mod_ref[0, 0]

    def hfun(x):
        return _rms(x, g_ref[0]) * (1.0 + mod[4:5]) + mod[3:4]

    x = x_ref[...]
    hext_ref[0:SUBLANES, :] = jnp.where(pos > 0, hfun(xp_ref[...]), 0.0)
    hext_ref[SUBLANES:SUBLANES + TM, :] = hfun(x)
    hext_ref[SUBLANES + TM:2 * SUBLANES + TM, :] = jnp.where(pos < nper - 1, hfun(xn_ref[...]), 0.0)
    hbf_ref[...] = hext_ref[...].astype(BF16)
    rows = TM + 2 * SUBLANES

    def conv(u, c0):
        cw = cw_ref[0, :, c0:c0 + fc]
        um1 = pltpu.roll(u, 1, 0)[SUBLANES:SUBLANES + TM]
        up1 = pltpu.roll(u, rows - 1, 0)[SUBLANES:SUBLANES + TM]
        return (um1 * cw[0:1] + u[SUBLANES:SUBLANES + TM] * cw[1:2] + up1 * cw[2:3]
                + cb_ref[0, :, c0:c0 + fc])

    for j in range(f // fc):
        ug = jnp.dot(hbf_ref[...], wup_ref[0, :, j * fc:(j + 1) * fc], preferred_element_type=F32)
        uv = jnp.dot(hbf_ref[...], wup_ref[0, :, f + j * fc:f + (j + 1) * fc], preferred_element_type=F32)
        act_ref[:, j * fc:(j + 1) * fc] = (_silu(conv(ug, j * fc)) * conv(uv, f + j * fc)).astype(BF16)
    acc = jnp.dot(act_ref[...], wdn_ref[0], preferred_element_type=F32)
    out = x + mod[5:6] * _rms(acc, gp_ref[0])
    if split:
        @pl.when(i < nct)
        def _():
            oc_ref[...] = out

        @pl.when(i >= nct)
        def _():
            ol_ref[...] = out
    else:
        o_ref[...] = out


def _ffn(x_all, mod_all, g_pre2, wup, cw, cb, wdn, g_post2, *, layer, nct, tpc, tpl, split):
    n, d = x_all.shape
    f = wdn.shape[1]
    fc = 256
    assert f % fc == 0
    sub = TM // SUBLANES
    nb8 = n // SUBLANES
    row = lambda i: (i, 0)
    lay3 = lambda i: (layer, 0, 0)
    mrow = _mod_row(nct, tpl)
    single = dict(pipeline_mode=pl.Buffered(1))
    if split:
        out_specs = [pl.BlockSpec((TM, d), lambda i: (jnp.minimum(i, nct - 1), 0)),
                     pl.BlockSpec((TM, d), lambda i: (jnp.maximum(i - nct, 0), 0))]
        out_shape = [jax.ShapeDtypeStruct((nct * TM, d), F32), jax.ShapeDtypeStruct((n - nct * TM, d), F32)]
    else:
        out_specs = pl.BlockSpec((TM, d), row)
        out_shape = jax.ShapeDtypeStruct((n, d), F32)
    return pl.pallas_call(
        functools.partial(_ffn_kernel, f=f, fc=fc, nct=nct, tpc=tpc, tpl=tpl, split=split),
        grid=(n // TM,),
        in_specs=[pl.BlockSpec((TM, d), row),
                  pl.BlockSpec((SUBLANES, d), lambda i: (jnp.maximum(i * sub - 1, 0), 0)),
                  pl.BlockSpec((SUBLANES, d), lambda i: (jnp.minimum((i + 1) * sub, nb8 - 1), 0)),
                  pl.BlockSpec((1, 1, 6, d), lambda i: (layer, mrow(i), 0, 0)),
                  pl.BlockSpec((1, 1, d), lay3),
                  pl.BlockSpec((1,) + wup.shape[1:], lay3, **single),
                  pl.BlockSpec((1,) + cw.shape[1:], lay3),
                  pl.BlockSpec((1,) + cb.shape[1:], lay3),
                  pl.BlockSpec((1,) + wdn.shape[1:], lay3, **single),
                  pl.BlockSpec((1, 1, d), lay3)],
        out_specs=out_specs,
        out_shape=out_shape,
        scratch_shapes=[pltpu.VMEM((TM + 2 * SUBLANES, d), F32), pltpu.VMEM((TM + 2 * SUBLANES, d), BF16),
                        pltpu.VMEM((TM, f), BF16)],
        compiler_params=_cparams(("arbitrary",)),
        name="conv_ffn",
    )(x_all, x_all, x_all, mod_all, g_pre2, wup, cw, cb, wdn, g_post2)


def _ssd_expand(st):
    z = jnp.zeros_like(st)
    hpg = N_HEADS // C_GROUPS
    left = jnp.concatenate([st[..., :hpg, :, :], z[..., hpg:, :, :]], axis=-3)
    right = jnp.concatenate([z[..., :hpg, :, :], st[..., hpg:, :, :]], axis=-3)
    return jnp.concatenate([left, right], axis=-1).reshape(st.shape[:-3] + (WG, 2 * C_STATE))


def _ssd_extract(s):
    s = s.reshape(s.shape[:-2] + (N_HEADS, HEAD_DIM, C_GROUPS, C_STATE))
    hpg = N_HEADS // C_GROUPS
    return jnp.stack([s[..., h, :, h // hpg, :] for h in range(N_HEADS)], axis=-3)


def kernel(x_prompt, x_sample, cache_a_k, cache_a_v, cache_b_k, cache_b_v, state_ssd, state_s5,
           c, c_ctx, w_mod, b_mod, g_pre1, g_post1, g_pre2, g_post2, w_in, a_lam, a_subln,
           b_qnorm, b_knorm, c_conv_w, c_conv_b, c_dt_bias, c_a_log, c_d, c_norm,
           d_lam_re, d_lam_im, d_log_step, d_b, d_c, d_d, d_glu, w_out, w_up,
           ffn_conv_w, ffn_conv_b, w_down):
    b1, l1, d = x_prompt.shape
    b2, l2, _ = x_sample.shape
    depth = w_mod.shape[0]
    past = cache_a_k.shape[2]
    n1, n2 = b1 * l1, b2 * l2
    assert d == 4 * WG and l1 % TM == 0 and l2 % TM == 0 and n1 % l2 == 0
    assert b1 % SUBLANES == 0 and SUBLANES % b2 == 0 and 1 + b2 <= SUBLANES
    nct, tpc, tpl = n1 // TM, l1 // TM, l2 // TM
    fold = SUBLANES // b2

    x_all = (x_prompt.reshape(n1, d), x_sample.reshape(n2, d))
    cond8 = jnp.concatenate([c_ctx[None, :], c, jnp.zeros((SUBLANES - 1 - b2, d), F32)], axis=0)
    mod_all = _modulation(cond8, w_mod, b_mod).reshape(depth, SUBLANES, 6, d)

    vec = lambda t: t.reshape(depth, 1, t.shape[-1])
    tile_h = lambda g: jnp.tile(g, (1, N_HEADS)).reshape(depth, 1, WG)
    hd = HEAD_DIM
    wcol = lambda a, b: w_in[:, :, a:b]
    kv_dup = lambda o: [wcol(o, o + hd), wcol(o, o + hd), wcol(o + hd, o + 2 * hd), wcol(o + hd, o + 2 * hd)]
    w_in_p = jnp.concatenate([wcol(0, 1024)] + kv_dup(1024) + kv_dup(1152)
                             + [wcol(1280, 2048), wcol(2056, 2312), wcol(2048, 2056),
                                jnp.zeros((depth, d, 120), F32)], axis=2).astype(BF16)
    w_up16, w_dn16, w_out16, w_glu16 = (t.astype(BF16) for t in (w_up, w_down, w_out, d_glu))
    rt, ct = _rope_tables(l2)
    ones_blk = jnp.asarray(np.kron(np.eye(N_HEADS, dtype=np.float32),
                                   np.full((HEAD_DIM, HEAD_DIM), 1.0 / HEAD_DIM, np.float32))).astype(BF16)
    rep_kv = lambda t: jnp.repeat(t, 2, axis=-2).reshape(*t.shape[:-2], WG)
    cak = cache_a_k.reshape(b2, depth, past, WG).astype(BF16)
    cav = cache_a_v.reshape(b2, depth, past, WG).astype(BF16)
    cbk = rep_kv(cache_b_k).astype(BF16)
    cbv = rep_kv(cache_b_v).astype(BF16)
    qn, kn, subln = tile_h(b_qnorm), tile_h(b_knorm), tile_h(a_subln)
    h0_lat = jnp.transpose(_ssd_expand(state_ssd), (1, 2, 0, 3, 4))
    h0_ssd = jnp.concatenate([jnp.zeros((depth, 2, b1, WG, 2 * C_STATE), F32), h0_lat], axis=2)
    dir_lanes = lambda t: jnp.concatenate(
        [jnp.stack([t[:, 0], jnp.zeros_like(t[:, 0])], axis=1), jnp.stack([jnp.zeros_like(t[:, 1]), t[:, 1]], axis=1),
         jnp.zeros((depth, 2, 128 - 2 * N_HEADS), F32)], axis=-1)[:, :, None, :]
    dtb = dir_lanes(c_dt_bias)
    av = dir_lanes(-jnp.exp(c_a_log))
    dsk = jnp.repeat(c_d, HEAD_DIM, axis=-1).reshape(depth, 1, WG)
    pr = _s5_params(d_lam_re, d_lam_im, d_log_step, d_b, d_c)
    wb1, wc1, ar1, ai1 = _s5_weights(*pr, 1, SUBLANES)
    wb2, wc2, ar2, ai2 = _s5_weights(*pr, fold, b2)
    gpr = D_GROUPS // fold
    zero1 = jnp.zeros((2, b1 // SUBLANES, SUBLANES, S5_LANES // 2), F32)
    h0s = state_s5.reshape(b2, depth, 2, fold, gpr * D_STATE, 2)
    h0s = jnp.transpose(h0s, (1, 2, 3, 0, 4, 5)).reshape(depth, 2, 1, SUBLANES, gpr * D_STATE, 2)

    new_ak, new_av, new_bk, new_bv, new_ssd, new_s5 = [], [], [], [], [], []
    y_prompt = y_sample = None
    for l in range(depth):
        lam_init = 0.8 - 0.6 * math.exp(-0.3 * l)
        qa, ka, va, qb, kb, vb, ka32, va32, kb32, vb32, cz, cx, du, cdt = _in_proj(
            x_all, mod_all, vec(g_pre1), w_in_p, qn, kn, ones_blk, rt, ct, layer=l, nct=nct, tpl=tpl)

        attn_c = functools.partial(_attention, n_seq=b1, seq_len=l1, row_off=0, tq=TM, ck=1024, layer=l)
        attn_l = functools.partial(_attention, n_seq=b2, seq_len=l2, row_off=n1, tq=TQ_LATENT, ck=1024, layer=l)
        diff_kw = dict(diff=True, lam=a_lam, subln=subln, lam_init=lam_init)
        ya = (attn_c(qa, ka, va, **diff_kw), attn_l(qa, ka, va, ctx=(cak, cav), **diff_kw))
        yb = (attn_c(qb, kb, vb, diff=False), attn_l(qb, kb, vb, diff=False, ctx=(cbk, cbv)))

        ycf, ycb, st_f, st_b = _ssd(cx, cdt, h0_ssd[l], c_conv_w, vec(c_conv_b), dtb, av, dsk, layer=l,
                                    npc=n1 // SSD_PAIR, ppc=l1 // SSD_PAIR, ppl=l2 // SSD_PAIR, n_ctx_seq=b1)
        new_ssd.append(jnp.stack([_ssd_extract(st_f[:b1]), _ssd_extract(st_b[:b1])], axis=1))

        ydc, s1r, s1i = _s5_scan(du, wb1, ar1, ai1, wc1, zero1, zero1, layer=l, n_b=b1, seq_len=l1,
                                 row_off=0, tt=min(l1, 128), fold=1)
        ydl, _, _ = _s5_scan(du, wb2, ar2, ai2, wc2, h0s[l, ..., 0], h0s[l, ..., 1], layer=l, n_b=b2,
                             seq_len=l2, row_off=n1, tt=min(l2, 512), fold=fold)
        st1 = jnp.stack([s1r, s1i], axis=-1).reshape(2, b1, D_GROUPS, D_STATE, 2)
        new_s5.append(jnp.transpose(st1, (1, 0, 2, 3, 4)))

        x_all = _out_proj(x_all, mod_all, ya, yb, ycf, ycb, cz, vec(c_norm), ydc.reshape(2, n1, WG),
                          ydl.reshape(2, n2, WG), du, vec(d_d), w_glu16, w_out16, vec(g_post1),
                          layer=l, nct=nct, tpl=tpl)
        last = l == depth - 1
        res = _ffn(x_all, mod_all, vec(g_pre2), w_up16, ffn_conv_w, vec(ffn_conv_b), w_dn16, vec(g_post2),
                   layer=l, nct=nct, tpc=tpc, tpl=tpl, split=last)
        if last:
            y_prompt, y_sample = res[0].reshape(b1, l1, d), res[1].reshape(b2, l2, d)
        else:
            x_all = res

        cache = lambda t, nh: jnp.transpose(
            t[:nct].reshape(b1, tpc, nh, HEAD_DIM, TM), (0, 1, 4, 2, 3)).reshape(b1, l1, nh, HEAD_DIM)
        new_ak.append(cache(ka32, N_HEADS))
        new_av.append(cache(va32, N_HEADS))
        new_bk.append(cache(kb32, N_HEADS // 2))
        new_bv.append(cache(vb32, N_HEADS // 2))

    st = lambda xs: jnp.stack(xs, axis=1)
    return (y_prompt, y_sample, st(new_ak), st(new_av), st(new_bk), st(new_bv), st(new_ssd), st(new_s5))
```

```python
import functools
import math

import numpy as np
import jax
import jax.numpy as jnp
from jax import lax
from jax.experimental import pallas as pl
from jax.experimental.pallas import tpu as pltpu

F32 = jnp.float32
BF16 = jnp.bfloat16

EPS = 1e-6
LOG2E = math.log2(math.e)
GRID_W = 64
ROPE_BASE = 10000.0
HEAD_DIM = 64
A_HALF = HEAD_DIM // 2
N_HEADS = 4
WG = N_HEADS * HEAD_DIM
C_GROUPS = 2
C_STATE = 64
C_CHUNK = 128
D_GROUP = 16
D_GROUPS = WG // D_GROUP
D_STATE = 64
S5_LANES = D_GROUPS * D_STATE * 2

TM = 256
TQ_LATENT = 512
SUBLANES = 8
VMEM_LIMIT = 56 * 1024 * 1024

_C_AQ, _C_AK, _C_AV, _C_BQ, _C_BK, _C_BV, _C_CZ, _C_CX, _C_DU, _C_DT, _C_END = (
    0, 256, 512, 768, 1024, 1280, 1536, 1792, 2304, 2560, 2688)


def _sigmoid(x):
    return 1.0 / (1.0 + jnp.exp(-x))


def _silu(x):
    return x * _sigmoid(x)


def _cparams(sem):
    return pltpu.CompilerParams(dimension_semantics=sem, vmem_limit_bytes=VMEM_LIMIT)


def _lane_mask(width, lo, hi):
    lane = lax.broadcasted_iota(jnp.int32, (1, width), 1)
    return (lane >= lo) & (lane < hi)


def _rms(x, g):
    return (x * lax.rsqrt(jnp.mean(x * x, axis=-1, keepdims=True) + EPS)) * g


def _mod_row(nct, tpl):
    return lambda i: jnp.where(i < nct, 0, 1 + jnp.maximum(i - nct, 0) // tpl)


def _mod_kernel(c_ref, w_ref, b_ref, o_ref):
    s = _silu(c_ref[...])
    o_ref[0] = jnp.dot(s.astype(BF16), w_ref[0].astype(BF16), preferred_element_type=F32) + b_ref[0]


def _modulation(cond8, w_mod, b_mod):
    depth, d, n = w_mod.shape
    tn = 1536
    return pl.pallas_call(
        _mod_kernel,
        grid=(depth, n // tn),
        in_specs=[pl.BlockSpec((SUBLANES, d), lambda l, j: (0, 0)),
                  pl.BlockSpec((1, d, tn), lambda l, j: (l, 0, j)),
                  pl.BlockSpec((1, 1, tn), lambda l, j: (l, 0, j))],
        out_specs=pl.BlockSpec((1, SUBLANES, tn), lambda l, j: (l, 0, j)),
        out_shape=jax.ShapeDtypeStruct((depth, SUBLANES, n), F32),
        compiler_params=_cparams(("parallel", "parallel")),
        name="modulation",
    )(cond8, w_mod, b_mod.reshape(depth, 1, n))


def _x_specs(x, nct):
    if isinstance(x, tuple):
        d = x[0].shape[1]
        return [pl.BlockSpec((TM, d), lambda i: (jnp.minimum(i, nct - 1), 0)),
                pl.BlockSpec((TM, d), lambda i: (jnp.maximum(i - nct, 0), 0))], list(x)
    return [pl.BlockSpec((TM, x.shape[1]), lambda i: (i, 0))], [x]


def _load_x(x_refs, nct):
    if len(x_refs) == 2:
        return jnp.where(pl.program_id(0) < nct, x_refs[0][...], x_refs[1][...])
    return x_refs[0][...]


def _inproj_kernel(*refs, nct, n_x):
    x_refs = refs[:n_x]
    (mod_ref, g_ref, w_ref, qn_ref, kn_ref, ones_ref, rt_ref, ct_ref,
     qa_ref, ka_ref, va_ref, qb_ref, kb_ref, vb_ref,
     ka32_ref, va32_ref, kb32_ref, vb32_ref, cz_ref, cx_ref, du_ref, dt_ref) = refs[n_x:]
    i = pl.program_id(0)
    mod = mod_ref[0, 0]
    h = _rms(_load_x(x_refs, nct), g_ref[0]) * (1.0 + mod[1:2]) + mod[0:1]
    p = jnp.dot(h.astype(BF16), w_ref[0], preferred_element_type=F32)
    va_ref[...] = p[:, _C_AV:_C_BQ].astype(BF16)
    vb_ref[...] = p[:, _C_BV:_C_CZ].astype(BF16)
    kv_rows = lambda t: jnp.concatenate([t[0:HEAD_DIM], t[2 * HEAD_DIM:3 * HEAD_DIM]], axis=0)
    va32_ref[0] = p[:, _C_AV:_C_BQ].T
    vb32_ref[0] = kv_rows(p[:, _C_BV:_C_CZ].T)
    cz_ref[...] = p[:, _C_CZ:_C_CX]
    cx_ref[...] = p[:, _C_CX:_C_DU]
    du_ref[...] = p[:, _C_DU:_C_DT]
    dt_ref[...] = p[:, _C_DT:_C_END]

    def headnorm(t, gain):
        sq = t * t
        hi = sq.astype(BF16)
        lo = (sq - hi.astype(F32)).astype(BF16)
        ms = (jnp.dot(hi, ones_ref[...], preferred_element_type=F32)
              + jnp.dot(lo, ones_ref[...], preferred_element_type=F32))
        return (t * lax.rsqrt(ms + EPS)) * gain

    aq = p[:, _C_AQ:_C_AK] * (A_HALF ** -0.5 * LOG2E)
    ak = p[:, _C_AK:_C_AV]
    bq = headnorm(p[:, _C_BQ:_C_BK], qn_ref[0]) * (HEAD_DIM ** -0.5 * LOG2E)
    bk = headnorm(p[:, _C_BK:_C_BV], kn_ref[0])
    ka32_ref[0] = ak.T
    kb32_ref[0] = kv_rows(bk.T)

    rows = TM // GRID_W
    is_lat = i >= nct
    rt = rt_ref[0]
    lane = lax.broadcasted_iota(jnp.int32, (1, WG), 1)
    row_lane_a = (lane % (A_HALF // 2)) < A_HALF // 4
    row_lane_b = (lane % (HEAD_DIM // 2)) < HEAD_DIM // 4

    def table(k, row_lane):
        ctk = jnp.where(is_lat, ct_ref[k], 1.0 if k % 3 == 0 else 0.0)
        parts = [jnp.where(row_lane, rt[k, r:r + 1, :], ctk) for r in range(rows)]
        return jnp.concatenate(parts, axis=0)

    def rope(t, base, row_lane, dist):
        return (t * table(base, row_lane)
                + pltpu.roll(t, WG - dist, 1) * table(base + 1, row_lane)
                + pltpu.roll(t, dist, 1) * table(base + 2, row_lane))

    qa_ref[...] = rope(aq, 0, row_lane_a, A_HALF // 2).astype(BF16)
    ka_ref[...] = rope(ak, 0, row_lane_a, A_HALF // 2).astype(BF16)
    qb_ref[...] = rope(bq, 3, row_lane_b, HEAD_DIM // 2).astype(BF16)
    kb_ref[...] = rope(bk, 3, row_lane_b, HEAD_DIM // 2).astype(BF16)


def _rope_tables(l2):
    rows = l2 // GRID_W
    lane = np.arange(WG)

    def one(block, n):
        freqs = (np.float32(ROPE_BASE) ** (-np.arange(n, dtype=np.float32) / np.float32(n))).astype(np.float32)
        p = lane % block
        idx = p % (block // 2)
        first = p < (block // 2)
        f = freqs[idx % n]

        def tabs(pos):
            ang = (pos[:, None].astype(np.float32) * f[None, :]).astype(np.float32)
            c, s = np.cos(ang).astype(np.float32), np.sin(ang).astype(np.float32)
            return [c, np.where(first[None, :], -s, 0.0).astype(np.float32),
                    np.where(first[None, :], 0.0, s).astype(np.float32)]

        return tabs(np.arange(rows)), tabs(np.arange(GRID_W))

    ra, ca = one(A_HALF, A_HALF // 4)
    rb, cb = one(HEAD_DIM, HEAD_DIM // 4)
    rt = np.stack(ra + rb, axis=0)
    ct = np.stack(ca + cb, axis=0)
    rpt = TM // GRID_W
    npt = l2 // TM
    rt = rt.reshape(6, npt, rpt, WG).transpose(1, 0, 2, 3)
    rt = np.concatenate([rt, np.zeros((npt, 6, SUBLANES - rpt, WG), np.float32)], axis=2)
    ident = np.zeros((1, 6, SUBLANES, WG), np.float32)
    ident[0, 0::3] = 1.0
    return jnp.asarray(np.concatenate([rt, ident], axis=0)), jnp.asarray(ct)


def _in_proj(x_all, mod_all, g_pre1, w_in_p, qn, kn, ones_blk, rt, ct, *, layer, nct, tpl):
    x_specs, x_args = _x_specs(x_all, nct)
    n = sum(a.shape[0] for a in x_args)
    d = x_args[0].shape[1]
    nt = n // TM
    row = lambda i: (i, 0)
    const2 = lambda i: (0, 0)
    lay3 = lambda i: (layer, 0, 0)
    mrow = _mod_row(nct, tpl)
    ctx_blk = lambda i: (jnp.minimum(i, nct), 0, 0)
    kvw = WG // 2
    out_specs = ([pl.BlockSpec((TM, WG), row)] * 6
                 + [pl.BlockSpec((1, w, TM), ctx_blk) for w in (WG, WG, kvw, kvw)]
                 + [pl.BlockSpec((TM, w), row) for w in (WG, 2 * WG, WG, 128)])
    out_shape = ([jax.ShapeDtypeStruct((n, WG), BF16)] * 6
                 + [jax.ShapeDtypeStruct((nct + 1, w, TM), F32) for w in (WG, WG, kvw, kvw)]
                 + [jax.ShapeDtypeStruct((n, w), F32) for w in (WG, 2 * WG, WG, 128)])
    return pl.pallas_call(
        functools.partial(_inproj_kernel, nct=nct, n_x=len(x_args)),
        grid=(nt,),
        in_specs=x_specs + [
                  pl.BlockSpec((1, 1, 6, d), lambda i: (layer, mrow(i), 0, 0)),
                  pl.BlockSpec((1, 1, d), lay3),
                  pl.BlockSpec((1,) + w_in_p.shape[1:], lay3),
                  pl.BlockSpec((1, 1, WG), lay3),
                  pl.BlockSpec((1, 1, WG), lay3),
                  pl.BlockSpec((WG, WG), const2),
                  pl.BlockSpec((1, 6, SUBLANES, WG),
                               lambda i: (jnp.where(i < nct, tpl, jnp.maximum(i - nct, 0) % tpl), 0, 0, 0)),
                  pl.BlockSpec(ct.shape, lambda i: (0, 0, 0))],
        out_specs=out_specs,
        out_shape=out_shape,
        compiler_params=_cparams(("arbitrary",)),
        name="in_proj",
    )(*x_args, mod_all, g_pre1, w_in_p, qn, kn, ones_blk, rt, ct)


def _attn_kernel(*refs, diff, tq, lk_new, lc, ck, lam_init):
    it = iter(refs)
    q_ref, k_ref, v_ref = next(it), next(it), next(it)
    kc_ref = vc_ref = None
    if lc:
        kc_ref, vc_ref = next(it).at[0, 0], next(it).at[0, 0]
    lam_ref = sub_ref = None
    if diff:
        lam_ref, sub_ref = next(it), next(it)
    o_ref = next(it)
    nsm = 2 if diff else 1
    s_ref = tuple(tuple(next(it) for _ in range(nsm)) for _ in range(2))
    m_ref = tuple(tuple(next(it) for _ in range(nsm)) for _ in range(2))
    oall_ref = next(it)

    blocks = [(k_ref, v_ref, r0, min(ck, lk_new - r0), r0) for r0 in range(0, lk_new, ck)]
    if lc:
        blocks.append((kc_ref, vc_ref, 0, lc, lk_new))

    width = HEAD_DIM // nsm
    q = q_ref[...]
    lane = lax.broadcasted_iota(jnp.int32, (1, WG), 1)
    if diff:
        lp = lam_ref[0]
        lam = (jnp.exp(jnp.sum(lp[0:1] * lp[1:2], axis=-1, keepdims=True))
               - jnp.exp(jnp.sum(lp[2:3] * lp[3:4], axis=-1, keepdims=True)) + lam_init)

    def lane_tiles(x):
        return [x[:, t * 128:(t + 1) * 128] for t in range(x.shape[1] // 128)]

    def step(h, slot, scores, values):
        if scores:
            qms = [jnp.where((lane >= h * HEAD_DIM + e * width) & (lane < h * HEAD_DIM + (e + 1) * width),
                             q, jnp.zeros_like(q)) for e in range(nsm)]
            mrun = [jnp.full((tq, 128), -jnp.inf, F32) for _ in range(nsm)]
        if values:
            prev = s_ref[1 - slot]
            m = [jnp.max(m_ref[1 - slot][e][...], axis=-1, keepdims=True) for e in range(nsm)]
            lrun = [jnp.zeros((tq, 128), F32) for _ in range(nsm)]
            acc = jnp.zeros((tq, WG), F32)
        for kr, vr, r0, rows, c0 in blocks:
            for e in range(nsm):
                if scores:
                    s = lax.dot_general(qms[e], kr[r0:r0 + rows, :], (((1,), (1,)), ((), ())),
                                        preferred_element_type=F32)
                    s_ref[slot][e][:, c0:c0 + rows] = s
                    for t in lane_tiles(s):
                        mrun[e] = jnp.maximum(mrun[e], t)
                if values:
                    p = jnp.exp2(prev[e][:, c0:c0 + rows] - m[e])
                    for t in lane_tiles(p):
                        lrun[e] = lrun[e] + t
                    if diff:
                        prev[e][:, c0:c0 + rows] = p
                    else:
                        acc = acc + jnp.dot(p.astype(BF16), vr[r0:r0 + rows, :], preferred_element_type=F32)
        if scores:
            for e in range(nsm):
                m_ref[slot][e][...] = mrun[e]
        if values:
            inv = [1.0 / jnp.sum(lrun[e], axis=-1, keepdims=True) for e in range(nsm)]
            if diff:
                for _, vr, r0, rows, c0 in blocks:
                    p = prev[0][:, c0:c0 + rows] * inv[0] - prev[1][:, c0:c0 + rows] * (lam * inv[1])
                    acc = acc + jnp.dot(p.astype(BF16), vr[r0:r0 + rows, :], preferred_element_type=F32)
            else:
                acc = acc * inv[0]
            oall_ref[h - 1] = acc

    step(0, 0, True, False)

    def body(g, carry):
        step(2 * g + 1, 1, True, True)
        step(2 * g + 2, 0, True, True)
        return carry

    lax.fori_loop(0, (N_HEADS - 2) // 2, body, 0)
    step(N_HEADS - 1, 1, True, True)
    step(N_HEADS, 0, False, True)

    out = jnp.zeros((tq, WG), F32)
    for h in range(N_HEADS):
        hm = _lane_mask(WG, h * HEAD_DIM, (h + 1) * HEAD_DIM)
        o = oall_ref[h]
        if diff:
            ms = jnp.sum(jnp.where(hm, o * o, 0.0), axis=-1, keepdims=True) * (1.0 / HEAD_DIM)
            o = ((o * lax.rsqrt(ms + EPS)) * sub_ref[0]) * (1.0 - lam_init)
        out = jnp.where(hm, o, out)
    o_ref[...] = out


def _attention(q, k, v, *, diff, n_seq, seq_len, row_off, tq, ck, layer, ctx=None,
               lam=None, subln=None, lam_init=0.0):
    assert row_off % seq_len == 0 and seq_len % tq == 0
    qpb = seq_len // tq
    q_map = lambda s, j: (row_off // tq + s * qpb + j, 0)
    kv_map = lambda s, j: (row_off // seq_len + s, 0)
    in_specs = [pl.BlockSpec((tq, WG), q_map), pl.BlockSpec((seq_len, WG), kv_map),
                pl.BlockSpec((seq_len, WG), kv_map)]
    args = [q, k, v]
    lc = 0
    if ctx is not None:
        kc, vc = ctx
        lc = kc.shape[2]
        cmap = lambda s, j: (s, layer, 0, 0)
        in_specs += [pl.BlockSpec((1, 1, lc, WG), cmap), pl.BlockSpec((1, 1, lc, WG), cmap)]
        args += [kc, vc]
    if diff:
        lay3 = lambda s, j: (layer, 0, 0)
        in_specs += [pl.BlockSpec((1,) + lam.shape[1:], lay3), pl.BlockSpec((1, 1, WG), lay3)]
        args += [lam, subln]
    lk = seq_len + lc
    return pl.pallas_call(
        functools.partial(_attn_kernel, diff=diff, tq=tq, lk_new=seq_len, lc=lc, ck=min(ck, seq_len),
                          lam_init=lam_init),
        grid=(n_seq, qpb),
        in_specs=in_specs,
        out_specs=pl.BlockSpec((tq, WG), lambda s, j: (s * qpb + j, 0)),
        out_shape=jax.ShapeDtypeStruct((n_seq * seq_len, WG), F32),
        scratch_shapes=([pltpu.VMEM((tq, lk), F32)] * (4 if diff else 2)
                        + [pltpu.VMEM((tq, 128), F32)] * (4 if diff else 2)
                        + [pltpu.VMEM((N_HEADS, tq, WG), F32)]),
        compiler_params=_cparams(("parallel", "parallel")),
        name="attn_diff" if diff else "attn_gqa",
    )(*args)


SSD_PAIR = 2 * C_CHUNK


def _ssd_intra(xbc, dt_raw, dtb, av, *, reverse):
    q = C_CHUNK
    xs = xbc[:, 0:WG]
    bm = xbc[:, WG:WG + 128].astype(BF16)
    cm = xbc[:, WG + 128:WG + 256]
    raw = dt_raw + dtb
    dt = jnp.maximum(raw, 0.0) + jnp.log1p(jnp.exp(-jnp.abs(raw)))
    dta = dt * av
    li = lax.broadcasted_iota(jnp.int32, (q, q), 0)
    si = lax.broadcasted_iota(jnp.int32, (q, q), 1)
    causal = (si >= li) if reverse else (si <= li)
    cum = jnp.dot(causal.astype(F32), dta, preferred_element_type=F32, precision=lax.Precision.HIGHEST)
    cum_t = cum.T
    end = 0 if reverse else q - 1
    d0 = N_HEADS if reverse else 0
    rowh = lax.broadcasted_iota(jnp.int32, (WG, 1), 0) // HEAD_DIM

    y = jnp.zeros((q, WG), F32)
    xw = jnp.zeros((q, WG), F32)
    ecum = jnp.zeros((q, WG), F32)
    cdec = jnp.zeros((WG, 1), F32)
    gmat = None
    for h in range(N_HEADS):
        j = d0 + h
        g = h // (N_HEADS // C_GROUPS)
        if h % (N_HEADS // C_GROUPS) == 0:
            cg = jnp.where(_lane_mask(128, g * C_STATE, (g + 1) * C_STATE), cm, 0.0).astype(BF16)
            gmat = lax.dot_general(cg, bm, (((1,), (1,)), ((), ())), preferred_element_type=F32)
        col = jnp.broadcast_to(cum[:, j:j + 1], (q, 128))
        dtc = jnp.broadcast_to(dt[:, j:j + 1], (q, 128))
        wide = lambda a: jnp.concatenate([a, a], axis=1)
        seg = col - cum_t[j:j + 1, :]
        decay = jnp.where(causal, jnp.exp(jnp.where(causal, seg, 0.0)), 0.0)
        hm = _lane_mask(WG, h * HEAD_DIM, (h + 1) * HEAD_DIM)
        xdt = jnp.where(hm, xs * wide(dtc), 0.0)
        y = y + jnp.dot((gmat * decay).astype(BF16), xdt.astype(BF16), preferred_element_type=F32)
        cend = cum[end:end + 1, j:j + 1]
        xw = xw + xdt * wide(jnp.exp(cend - col))
        ecum = jnp.where(hm, wide(jnp.exp(col)), ecum)
        cdec = jnp.where(rowh == h, jnp.exp(cend), cdec)

    st = lax.dot_general(xw.astype(BF16), bm, (((0,), (0,)), ((), ())), preferred_element_type=F32)
    return y, xs, cm.astype(BF16), ecum, cdec, st


def _ssd_direction(x_ref, xp_ref, xn_ref, dt_ref, h0_ref, cw_ref, cb_ref, dtb_ref, av_ref, s_ref,
                   y_ref, st_ref, skip, p, *, reverse, npc, ppc, ppl):
    q = C_CHUNK
    is_lat = p >= npc
    pos = jnp.where(is_lat, jnp.maximum(p - npc, 0) % ppl, p % ppc)
    nper = jnp.where(is_lat, ppl, ppc)
    enter = (pos == nper - 1) if reverse else (pos == 0)

    @pl.when(enter)
    def _():
        s_ref[...] = h0_ref[0, 0]

    x = x_ref[...]
    rid = lax.broadcasted_iota(jnp.int32, (SSD_PAIR, 1), 0)
    prow = jnp.where(pos > 0, xp_ref[SUBLANES - 1:SUBLANES, :], 0.0)
    nrow = jnp.where(pos < nper - 1, xn_ref[0:1, :], 0.0)
    xm1 = jnp.where(rid == 0, prow, pltpu.roll(x, 1, 0))
    xp1 = jnp.where(rid == SSD_PAIR - 1, nrow, pltpu.roll(x, SSD_PAIR - 1, 0))
    cw = cw_ref[0]
    xbc = _silu(xm1 * cw[0:1] + x * cw[1:2] + xp1 * cw[2:3] + cb_ref[0])

    parts = [_ssd_intra(xbc[k * q:(k + 1) * q], dt_ref[k * q:(k + 1) * q, :], dtb_ref[...], av_ref[...],
                        reverse=reverse) for k in range(2)]
    rowh = lax.broadcasted_iota(jnp.int32, (WG, 1), 0) // HEAD_DIM
    colg = lax.broadcasted_iota(jnp.int32, (1, 128), 1) // C_STATE
    blk = (rowh // (N_HEADS // C_GROUPS)) == colg
    s = s_ref[...]
    for k in ((1, 0) if reverse else (0, 1)):
        y, xs, cm, ecum, cdec, st = parts[k]
        s_msk = jnp.where(blk, s, 0.0).astype(BF16)
        y = y + lax.dot_general(cm, s_msk, (((1,), (1,)), ((), ())), preferred_element_type=F32) * ecum
        if skip is not None:
            y = y + skip * xs
        y_ref[k * q:(k + 1) * q, :] = y
        s = s * cdec + st
    s_ref[...] = s
    st_ref[0] = s


def _ssd_kernel(xf_ref, xpf_ref, xnf_ref, dtf_ref, h0f_ref, xb_ref, xpb_ref, xnb_ref, dtr_ref, h0b_ref,
                cw_ref, cb_ref, dtb_ref, av_ref, dsk_ref,
                yf_ref, yb_ref, stf_ref, stb_ref, sf_ref, sb_ref, *, npc, ppc, ppl):
    t = pl.program_id(0)
    kw = dict(npc=npc, ppc=ppc, ppl=ppl)
    _ssd_direction(xf_ref, xpf_ref, xnf_ref, dtf_ref, h0f_ref, cw_ref, cb_ref, dtb_ref.at[0, 0], av_ref.at[0, 0],
                   sf_ref, yf_ref, stf_ref, dsk_ref[0], t, reverse=False, **kw)
    _ssd_direction(xb_ref, xpb_ref, xnb_ref, dtr_ref, h0b_ref, cw_ref, cb_ref, dtb_ref.at[0, 1], av_ref.at[0, 1],
                   sb_ref, yb_ref, stb_ref, None, pl.num_programs(0) - 1 - t, reverse=True, **kw)


def _ssd(cx, cdt, h0, conv_w, conv_b, dtb, av, dsk, *, layer, npc, ppc, ppl, n_ctx_seq):
    n = cx.shape[0]
    nc = n // SSD_PAIR
    n_seq = h0.shape[1]
    sub = SSD_PAIR // SUBLANES
    nb8 = n // SUBLANES
    lay3 = lambda t: (layer, 0, 0)

    def specs(cidx, d):
        def seq_of(t):
            c = cidx(t)
            return jnp.where(c < npc, c // ppc, n_ctx_seq + jnp.maximum(c - npc, 0) // ppl)

        row = lambda t: (cidx(t), 0)
        ins = [pl.BlockSpec((SSD_PAIR, 2 * WG), row),
               pl.BlockSpec((SUBLANES, 2 * WG), lambda t: (jnp.maximum(cidx(t) * sub - 1, 0), 0)),
               pl.BlockSpec((SUBLANES, 2 * WG), lambda t: (jnp.minimum((cidx(t) + 1) * sub, nb8 - 1), 0)),
               pl.BlockSpec((SSD_PAIR, 128), row),
               pl.BlockSpec((1, 1, WG, 128), lambda t: (d, seq_of(t), 0, 0))]
        outs = [pl.BlockSpec((SSD_PAIR, WG), row), pl.BlockSpec((1, WG, 128), lambda t: (seq_of(t), 0, 0))]
        return ins, outs

    ins_f, outs_f = specs(lambda t: t, 0)
    ins_b, outs_b = specs(lambda t: nc - 1 - t, 1)
    return pl.pallas_call(
        functools.partial(_ssd_kernel, npc=npc, ppc=ppc, ppl=ppl),
        grid=(nc,),
        in_specs=ins_f + ins_b + [pl.BlockSpec((1, 3, 2 * WG), lay3),
                                  pl.BlockSpec((1, 1, 2 * WG), lay3),
                                  pl.BlockSpec((1, 2, 1, 128), lambda t: (layer, 0, 0, 0)),
                                  pl.BlockSpec((1, 2, 1, 128), lambda t: (layer, 0, 0, 0)),
                                  pl.BlockSpec((1, 1, WG), lay3)],
        out_specs=[outs_f[0], outs_b[0], outs_f[1], outs_b[1]],
        out_shape=[jax.ShapeDtypeStruct((n, WG), F32), jax.ShapeDtypeStruct((n, WG), F32),
                   jax.ShapeDtypeStruct((n_seq, WG, 128), F32), jax.ShapeDtypeStruct((n_seq, WG, 128), F32)],
        scratch_shapes=[pltpu.VMEM((WG, 128), F32), pltpu.VMEM((WG, 128), F32)],
        compiler_params=_cparams(("arbitrary",)),
        name="ssd",
    )(cx, cx, cx, cdt, h0, cx, cx, cx, cdt, h0, conv_w, conv_b, dtb, av, dsk)


def _s5_kernel(*refs, tt, nl, nb_in, fold):
    u_refs = refs[:nb_in]
    (wb_ref, are_ref, aim_ref, wc_ref, h0r_ref, h0i_ref, y_ref, sr_ref, si_ref,
     ust_ref, ytm_ref, bu_ref, xs_ref, cr_ref, ci_ref) = refs[nb_in:]
    hl = nl // 2
    d = pl.program_id(0)

    @pl.when(pl.program_id(2) == 0)
    def _():
        cr_ref[...] = h0r_ref[0, 0]
        ci_ref[...] = h0i_ref[0, 0]

    lane = lax.broadcasted_iota(jnp.int32, (1, WG), 1)
    gw = WG // fold
    for qq in range(fold):
        for b in range(nb_in):
            c = qq * nb_in + b
            u = u_refs[b][...]
            if fold > 1:
                u = jnp.where((lane >= qq * gw) & (lane < (qq + 1) * gw), u, 0.0)
            for k in range(WG // 128):
                ust_ref[k, pl.ds(c, tt, stride=SUBLANES), :] = u[:, k * 128:(k + 1) * 128]
    ust = jnp.concatenate([ust_ref[k].astype(BF16) for k in range(WG // 128)], axis=1)
    bu_ref[...] = jnp.dot(ust, wb_ref[0, 0], preferred_element_type=F32)
    a_re = are_ref[0, 0]
    a_im = aim_ref[0, 0]

    def step(t, carry):
        xr, xi = carry
        te = t + d * (tt - 1 - 2 * t)
        r0 = pl.multiple_of(te * SUBLANES, SUBLANES)
        nr = a_re * xr - a_im * xi + bu_ref[pl.ds(r0, SUBLANES), 0:hl]
        ni = a_re * xi + a_im * xr + bu_ref[pl.ds(r0, SUBLANES), hl:nl]
        xs_ref[pl.ds(r0, SUBLANES), 0:hl] = nr
        xs_ref[pl.ds(r0, SUBLANES), hl:nl] = ni
        return nr, ni

    xr, xi = lax.fori_loop(0, tt, step, (cr_ref[...], ci_ref[...]), unroll=4)
    cr_ref[...] = xr
    ci_ref[...] = xi
    sr_ref[0, 0] = xr
    si_ref[0, 0] = xi
    y = jnp.dot(xs_ref[...].astype(BF16), wc_ref[0, 0], preferred_element_type=F32)
    for k in range(WG // 128):
        ytm_ref[k] = y[:, k * 128:(k + 1) * 128]

    def chain_rows(c):
        return jnp.concatenate([ytm_ref[k, pl.ds(c, tt, stride=SUBLANES), :] for k in range(WG // 128)], axis=1)

    for b in range(nb_in):
        if fold == 1:
            y_ref[0, b] = chain_rows(b)
        else:
            acc = jnp.zeros((tt, WG), F32)
            for qq in range(fold):
                acc = jnp.where((lane >= qq * gw) & (lane < (qq + 1) * gw), chain_rows(qq * nb_in + b), acc)
            y_ref[0, b] = acc


def _s5_scan(du, wb, a_re, a_im, wc, h0r, h0i, *, layer, n_b, seq_len, row_off, tt, fold):
    nb_in = SUBLANES // fold
    n_slab = n_b // nb_in
    nl = wb.shape[-1]
    hl = nl // 2
    nblk = seq_len // tt
    assert seq_len % tt == 0 and row_off % tt == 0
    tblk = lambda d, j: j + d * (nblk - 1 - 2 * j)
    u_specs = [pl.BlockSpec((tt, WG), functools.partial(
        lambda d, s, j, b: (row_off // tt + (s * nb_in + b) * nblk + tblk(d, j), 0), b=b)) for b in range(nb_in)]
    wmap = lambda d, s, j: (layer, d, 0, 0)
    smap = lambda d, s, j: (d, s, 0, 0)
    st_spec = pl.BlockSpec((1, 1, SUBLANES, hl), smap)
    rows = SUBLANES * tt
    return pl.pallas_call(
        functools.partial(_s5_kernel, tt=tt, nl=nl, nb_in=nb_in, fold=fold),
        grid=(2, n_slab, nblk),
        in_specs=u_specs + [pl.BlockSpec((1, 1, WG, nl), wmap),
                            pl.BlockSpec((1, 1, SUBLANES, hl), wmap),
                            pl.BlockSpec((1, 1, SUBLANES, hl), wmap),
                            pl.BlockSpec((1, 1, nl, WG), wmap),
                            st_spec, st_spec],
        out_specs=[pl.BlockSpec((1, nb_in, tt, WG), lambda d, s, j: (d, s, tblk(d, j), 0)), st_spec, st_spec],
        out_shape=[jax.ShapeDtypeStruct((2, n_b, seq_len, WG), F32),
                   jax.ShapeDtypeStruct((2, n_slab, SUBLANES, hl), F32),
                   jax.ShapeDtypeStruct((2, n_slab, SUBLANES, hl), F32)],
        scratch_shapes=[pltpu.VMEM((WG // 128, rows, 128), F32), pltpu.VMEM((WG // 128, rows, 128), F32),
                        pltpu.VMEM((rows, nl), F32), pltpu.VMEM((rows, nl), F32),
                        pltpu.VMEM((SUBLANES, hl), F32), pltpu.VMEM((SUBLANES, hl), F32)],
        compiler_params=_cparams(("parallel", "parallel", "arbitrary")),
        name="s5_scan",
    )(*([du] * nb_in), wb, a_re, a_im, wc, h0r, h0i)


def _s5_params(lam_re, lam_im, log_step, b_ri, c_ri):
    lam = lax.complex(lam_re, lam_im)
    a_bar = jnp.exp(lam * jnp.exp(log_step)[..., None])
    b_bar = ((a_bar - 1.0) / lam)[..., None] * lax.complex(b_ri[..., 0], b_ri[..., 1])
    return jnp.real(a_bar), jnp.imag(a_bar), jnp.real(b_bar), jnp.imag(b_bar), c_ri[..., 0], c_ri[..., 1]


def _s5_weights(a_re, a_im, bb_re, bb_im, c_re, c_im, fold, chain_rep):
    lead = a_re.shape[:2]
    m = lead[0] * lead[1]
    gpr = D_GROUPS // fold
    eye = jnp.eye(gpr, dtype=F32)

    def wb_part(bb):
        t = bb.reshape(m, fold, gpr, D_STATE, D_GROUP)
        src = jnp.transpose(t, (0, 1, 2, 4, 3))
        w = jnp.where(eye[None, None, :, None, :, None] > 0,
                      jnp.broadcast_to(src[:, :, :, :, None, :], (m, fold, gpr, D_GROUP, gpr, D_STATE)), 0.0)
        return w.reshape(m, WG, gpr * D_STATE)

    wb = jnp.concatenate([wb_part(bb_re), wb_part(bb_im)], axis=-1)

    def wc_part(cc):
        t = cc.reshape(m, fold, gpr, D_GROUP, D_STATE)
        src = jnp.transpose(t, (0, 2, 4, 1, 3))
        w = jnp.where(eye[None, :, None, None, :, None] > 0,
                      jnp.broadcast_to(src[:, :, :, :, None, :], (m, gpr, D_STATE, fold, gpr, D_GROUP)), 0.0)
        return w.reshape(m, gpr * D_STATE, WG)

    wc = jnp.concatenate([wc_part(c_re), -wc_part(c_im)], axis=1)

    def a_rows(a):
        t = a.reshape(m, fold, 1, gpr * D_STATE)
        return jnp.broadcast_to(t, (m, fold, chain_rep, gpr * D_STATE)).reshape(lead + (SUBLANES, gpr * D_STATE))

    nl = 2 * gpr * D_STATE
    return (wb.astype(BF16).reshape(lead + (WG, nl)), wc.astype(BF16).reshape(lead + (nl, WG)),
            a_rows(a_re), a_rows(a_im))


def _outproj_kernel(*refs, nct, n_x):
    x_refs = refs[:n_x]
    (mod_ref, yac_ref, yal_ref, ybc_ref, ybl_ref, ycf_ref, ycb_ref, cz_ref, cn_ref,
     ydc0_ref, ydc1_ref, ydl0_ref, ydl1_ref, du_ref, dd_ref, wglu_ref, wout_ref, g_ref, o_ref) = refs[n_x:]
    mod = mod_ref[0, 0]
    yc = _rms((ycf_ref[...] + ycb_ref[...]) * _silu(cz_ref[...]), cn_ref[0])
    is_ctx = pl.program_id(0) < nct
    ya = jnp.where(is_ctx, yac_ref[...], yal_ref[...])
    yb = jnp.where(is_ctx, ybc_ref[...], ybl_ref[...])
    yd = jnp.where(is_ctx, ydc0_ref[0] + ydc1_ref[0], ydl0_ref[0] + ydl1_ref[0]) + dd_ref[0] * du_ref[...]
    yd = yd * (0.5 * (1.0 + jnp.tanh(math.sqrt(2.0 / math.pi) * (yd + 0.044715 * (yd * yd * yd)))))
    gl = jnp.dot(yd.astype(BF16), wglu_ref[0], preferred_element_type=F32)
    yd = gl[:, 0:WG] * _sigmoid(gl[:, WG:2 * WG])
    cat = jnp.concatenate([ya.astype(BF16), yb.astype(BF16), yc.astype(BF16), yd.astype(BF16)], axis=-1)
    y = jnp.dot(cat, wout_ref[0], preferred_element_type=F32)
    o_ref[...] = _load_x(x_refs, nct) + mod[2:3] * _rms(y, g_ref[0])


def _out_proj(x_all, mod_all, ya, yb, ycf, ycb, cz, cnorm, ydc, ydl, du, dd, wglu, wout, g_post1,
              *, layer, nct, tpl):
    x_specs, x_args = _x_specs(x_all, nct)
    n = sum(a.shape[0] for a in x_args)
    d = x_args[0].shape[1]
    row = lambda i: (i, 0)
    lay3 = lambda i: (layer, 0, 0)
    mrow = _mod_row(nct, tpl)
    sm = pl.BlockSpec((TM, WG), row)
    smc = pl.BlockSpec((TM, WG), lambda i: (jnp.minimum(i, nct - 1), 0))
    sml = pl.BlockSpec((TM, WG), lambda i: (jnp.maximum(i - nct, 0), 0))
    ydc_spec = lambda dd_: pl.BlockSpec((1, TM, WG), lambda i: (dd_, jnp.minimum(i, nct - 1), 0))
    ydl_spec = lambda dd_: pl.BlockSpec((1, TM, WG), lambda i: (dd_, jnp.maximum(i - nct, 0), 0))
    return pl.pallas_call(
        functools.partial(_outproj_kernel, nct=nct, n_x=len(x_args)),
        grid=(n // TM,),
        in_specs=x_specs + [pl.BlockSpec((1, 1, 6, d), lambda i: (layer, mrow(i), 0, 0)),
                  smc, sml, smc, sml, sm, sm, sm, pl.BlockSpec((1, 1, WG), lay3),
                  ydc_spec(0), ydc_spec(1), ydl_spec(0), ydl_spec(1), sm,
                  pl.BlockSpec((1, 1, WG), lay3), pl.BlockSpec((1,) + wglu.shape[1:], lay3),
                  pl.BlockSpec((1,) + wout.shape[1:], lay3), pl.BlockSpec((1, 1, d), lay3)],
        out_specs=pl.BlockSpec((TM, d), row),
        out_shape=jax.ShapeDtypeStruct((n, d), F32),
        compiler_params=_cparams(("parallel",)),
        name="out_proj",
    )(*x_args, mod_all, ya[0], ya[1], yb[0], yb[1], ycf, ycb, cz, cnorm, ydc, ydc, ydl, ydl, du, dd,
      wglu, wout, g_post1)


def _ffn_kernel(x_ref, xp_ref, xn_ref, mod_ref, g_ref, wup_ref, cw_ref, cb_ref, wdn_ref, gp_ref,
                *rest, f, fc, nct, tpc, tpl, split):
    if split:
        oc_ref, ol_ref, hext_ref, hbf_ref, act_ref = rest
    else:
        o_ref, hext_ref, hbf_ref, act_ref = rest
    i = pl.program_id(0)
    is_lat = i >= nct
    pos = jnp.where(is_lat, jnp.maximum(i - nct, 0) % tpl, i % tpc)
    nper = jnp.where(is_lat, tpl, tpc)
    mod = mod_ref[0, 0]

    def hfun(x):
        return _rms(x, g_ref[0]) * (1.0 + mod[4:5]) + mod[3:4]

    x = x_ref[...]
    hext_ref[0:SUBLANES, :] = jnp.where(pos > 0, hfun(xp_ref[...]), 0.0)
    hext_ref[SUBLANES:SUBLANES + TM, :] = hfun(x)
    hext_ref[SUBLANES + TM:2 * SUBLANES + TM, :] = jnp.where(pos < nper - 1, hfun(xn_ref[...]), 0.0)
    hbf_ref[...] = hext_ref[...].astype(BF16)
    rows = TM + 2 * SUBLANES

    def conv(u, c0):
        cw = cw_ref[0, :, c0:c0 + fc]
        um1 = pltpu.roll(u, 1, 0)[SUBLANES:SUBLANES + TM]
        up1 = pltpu.roll(u, rows - 1, 0)[SUBLANES:SUBLANES + TM]
        return (um1 * cw[0:1] + u[SUBLANES:SUBLANES + TM] * cw[1:2] + up1 * cw[2:3]
                + cb_ref[0, :, c0:c0 + fc])

    for j in range(f // fc):
        ug = jnp.dot(hbf_ref[...], wup_ref[0, :, j * fc:(j + 1) * fc], preferred_element_type=F32)
        uv = jnp.dot(hbf_ref[...], wup_ref[0, :, f + j * fc:f + (j + 1) * fc], preferred_element_type=F32)
        act_ref[:, j * fc:(j + 1) * fc] = (_silu(conv(ug, j * fc)) * conv(uv, f + j * fc)).astype(BF16)
    acc = jnp.dot(act_ref[...], wdn_ref[0], preferred_element_type=F32)
    out = x + mod[5:6] * _rms(acc, gp_ref[0])
    if split:
        @pl.when(i < nct)
        def _():
            oc_ref[...] = out

        @pl.when(i >= nct)
        def _():
            ol_ref[...] = out
    else:
        o_ref[...] = out


def _ffn(x_all, mod_all, g_pre2, wup, cw, cb, wdn, g_post2, *, layer, nct, tpc, tpl, split):
    n, d = x_all.shape
    f = wdn.shape[1]
    fc = 256
    assert f % fc == 0
    sub = TM // SUBLANES
    nb8 = n // SUBLANES
    row = lambda i: (i, 0)
    lay3 = lambda i: (layer, 0, 0)
    mrow = _mod_row(nct, tpl)
    single = dict(pipeline_mode=pl.Buffered(1))
    if split:
        out_specs = [pl.BlockSpec((TM, d), lambda i: (jnp.minimum(i, nct - 1), 0)),
                     pl.BlockSpec((TM, d), lambda i: (jnp.maximum(i - nct, 0), 0))]
        out_shape = [jax.ShapeDtypeStruct((nct * TM, d), F32), jax.ShapeDtypeStruct((n - nct * TM, d), F32)]
    else:
        out_specs = pl.BlockSpec((TM, d), row)
        out_shape = jax.ShapeDtypeStruct((n, d), F32)
    return pl.pallas_call(
        functools.partial(_ffn_kernel, f=f, fc=fc, nct=nct, tpc=tpc, tpl=tpl, split=split),
        grid=(n // TM,),
        in_specs=[pl.BlockSpec((TM, d), row),
                  pl.BlockSpec((SUBLANES, d), lambda i: (jnp.maximum(i * sub - 1, 0), 0)),
                  pl.BlockSpec((SUBLANES, d), lambda i: (jnp.minimum((i + 1) * sub, nb8 - 1), 0)),
                  pl.BlockSpec((1, 1, 6, d), lambda i: (layer, mrow(i), 0, 0)),
                  pl.BlockSpec((1, 1, d), lay3),
                  pl.BlockSpec((1,) + wup.shape[1:], lay3, **single),
                  pl.BlockSpec((1,) + cw.shape[1:], lay3),
                  pl.BlockSpec((1,) + cb.shape[1:], lay3),
                  pl.BlockSpec((1,) + wdn.shape[1:], lay3, **single),
                  pl.BlockSpec((1, 1, d), lay3)],
        out_specs=out_specs,
        out_shape=out_shape,
        scratch_shapes=[pltpu.VMEM((TM + 2 * SUBLANES, d), F32), pltpu.VMEM((TM + 2 * SUBLANES, d), BF16),
                        pltpu.VMEM((TM, f), BF16)],
        compiler_params=_cparams(("arbitrary",)),
        name="conv_ffn",
    )(x_all, x_all, x_all, mod_all, g_pre2, wup, cw, cb, wdn, g_post2)


def _ssd_expand(st):
    z = jnp.zeros_like(st)
    hpg = N_HEADS // C_GROUPS
    left = jnp.concatenate([st[..., :hpg, :, :], z[..., hpg:, :, :]], axis=-3)
    right = jnp.concatenate([z[..., :hpg, :, :], st[..., hpg:, :, :]], axis=-3)
    return jnp.concatenate([left, right], axis=-1).reshape(st.shape[:-3] + (WG, 2 * C_STATE))


def _ssd_extract(s):
    s = s.reshape(s.shape[:-2] + (N_HEADS, HEAD_DIM, C_GROUPS, C_STATE))
    hpg = N_HEADS // C_GROUPS
    return jnp.stack([s[..., h, :, h // hpg, :] for h in range(N_HEADS)], axis=-3)


def kernel(x_prompt, x_sample, cache_a_k, cache_a_v, cache_b_k, cache_b_v, state_ssd, state_s5,
           c, c_ctx, w_mod, b_mod, g_pre1, g_post1, g_pre2, g_post2, w_in, a_lam, a_subln,
           b_qnorm, b_knorm, c_conv_w, c_conv_b, c_dt_bias, c_a_log, c_d, c_norm,
           d_lam_re, d_lam_im, d_log_step, d_b, d_c, d_d, d_glu, w_out, w_up,
           ffn_conv_w, ffn_conv_b, w_down):
    b1, l1, d = x_prompt.shape
    b2, l2, _ = x_sample.shape
    depth = w_mod.shape[0]
    past = cache_a_k.shape[2]
    n1, n2 = b1 * l1, b2 * l2
    assert d == 4 * WG and l1 % TM == 0 and l2 % TM == 0 and n1 % l2 == 0
    assert b1 % SUBLANES == 0 and SUBLANES % b2 == 0 and 1 + b2 <= SUBLANES
    nct, tpc, tpl = n1 // TM, l1 // TM, l2 // TM
    fold = SUBLANES // b2

    x_all = (x_prompt.reshape(n1, d), x_sample.reshape(n2, d))
    cond8 = jnp.concatenate([c_ctx[None, :], c, jnp.zeros((SUBLANES - 1 - b2, d), F32)], axis=0)
    mod_all = _modulation(cond8, w_mod, b_mod).reshape(depth, SUBLANES, 6, d)

    vec = lambda t: t.reshape(depth, 1, t.shape[-1])
    tile_h = lambda g: jnp.tile(g, (1, N_HEADS)).reshape(depth, 1, WG)
    hd = HEAD_DIM
    wcol = lambda a, b: w_in[:, :, a:b]
    kv_dup = lambda o: [wcol(o, o + hd), wcol(o, o + hd), wcol(o + hd, o + 2 * hd), wcol(o + hd, o + 2 * hd)]
    w_in_p = jnp.concatenate([wcol(0, 1024)] + kv_dup(1024) + kv_dup(1152)
                             + [wcol(1280, 2048), wcol(2056, 2312), wcol(2048, 2056),
                                jnp.zeros((depth, d, 120), F32)], axis=2).astype(BF16)
    w_up16, w_dn16, w_out16, w_glu16 = (t.astype(BF16) for t in (w_up, w_down, w_out, d_glu))
    rt, ct = _rope_tables(l2)
    ones_blk = jnp.asarray(np.kron(np.eye(N_HEADS, dtype=np.float32),
                                   np.full((HEAD_DIM, HEAD_DIM), 1.0 / HEAD_DIM, np.float32))).astype(BF16)
    rep_kv = lambda t: jnp.repeat(t, 2, axis=-2).reshape(*t.shape[:-2], WG)
    cak = cache_a_k.reshape(b2, depth, past, WG).astype(BF16)
    cav = cache_a_v.reshape(b2, depth, past, WG).astype(BF16)
    cbk = rep_kv(cache_b_k).astype(BF16)
    cbv = rep_kv(cache_b_v).astype(BF16)
    qn, kn, subln = tile_h(b_qnorm), tile_h(b_knorm), tile_h(a_subln)
    h0_lat = jnp.transpose(_ssd_expand(state_ssd), (1, 2, 0, 3, 4))
    h0_ssd = jnp.concatenate([jnp.zeros((depth, 2, b1, WG, 2 * C_STATE), F32), h0_lat], axis=2)
    dir_lanes = lambda t: jnp.concatenate(
        [jnp.stack([t[:, 0], jnp.zeros_like(t[:, 0])], axis=1), jnp.stack([jnp.zeros_like(t[:, 1]), t[:, 1]], axis=1),
         jnp.zeros((depth, 2, 128 - 2 * N_HEADS), F32)], axis=-1)[:, :, None, :]
    dtb = dir_lanes(c_dt_bias)
    av = dir_lanes(-jnp.exp(c_a_log))
    dsk = jnp.repeat(c_d, HEAD_DIM, axis=-1).reshape(depth, 1, WG)
    pr = _s5_params(d_lam_re, d_lam_im, d_log_step, d_b, d_c)
    wb1, wc1, ar1, ai1 = _s5_weights(*pr, 1, SUBLANES)
    wb2, wc2, ar2, ai2 = _s5_weights(*pr, fold, b2)
    gpr = D_GROUPS // fold
    zero1 = jnp.zeros((2, b1 // SUBLANES, SUBLANES, S5_LANES // 2), F32)
    h0s = state_s5.reshape(b2, depth, 2, fold, gpr * D_STATE, 2)
    h0s = jnp.transpose(h0s, (1, 2, 3, 0, 4, 5)).reshape(depth, 2, 1, SUBLANES, gpr * D_STATE, 2)

    new_ak, new_av, new_bk, new_bv, new_ssd, new_s5 = [], [], [], [], [], []
    y_prompt = y_sample = None
    for l in range(depth):
        lam_init = 0.8 - 0.6 * math.exp(-0.3 * l)
        qa, ka, va, qb, kb, vb, ka32, va32, kb32, vb32, cz, cx, du, cdt = _in_proj(
            x_all, mod_all, vec(g_pre1), w_in_p, qn, kn, ones_blk, rt, ct, layer=l, nct=nct, tpl=tpl)

        attn_c = functools.partial(_attention, n_seq=b1, seq_len=l1, row_off=0, tq=TM, ck=1024, layer=l)
        attn_l = functools.partial(_attention, n_seq=b2, seq_len=l2, row_off=n1, tq=TQ_LATENT, ck=1024, layer=l)
        diff_kw = dict(diff=True, lam=a_lam, subln=subln, lam_init=lam_init)
        ya = (attn_c(qa, ka, va, **diff_kw), attn_l(qa, ka, va, ctx=(cak, cav), **diff_kw))
        yb = (attn_c(qb, kb, vb, diff=False), attn_l(qb, kb, vb, diff=False, ctx=(cbk, cbv)))

        ycf, ycb, st_f, st_b = _ssd(cx, cdt, h0_ssd[l], c_conv_w, vec(c_conv_b), dtb, av, dsk, layer=l,
                                    npc=n1 // SSD_PAIR, ppc=l1 // SSD_PAIR, ppl=l2 // SSD_PAIR, n_ctx_seq=b1)
        new_ssd.append(jnp.stack([_ssd_extract(st_f[:b1]), _ssd_extract(st_b[:b1])], axis=1))

        ydc, s1r, s1i = _s5_scan(du, wb1, ar1, ai1, wc1, zero1, zero1, layer=l, n_b=b1, seq_len=l1,
                                 row_off=0, tt=min(l1, 128), fold=1)
        ydl, _, _ = _s5_scan(du, wb2, ar2, ai2, wc2, h0s[l, ..., 0], h0s[l, ..., 1], layer=l, n_b=b2,
                             seq_len=l2, row_off=n1, tt=min(l2, 512), fold=fold)
        st1 = jnp.stack([s1r, s1i], axis=-1).reshape(2, b1, D_GROUPS, D_STATE, 2)
        new_s5.append(jnp.transpose(st1, (1, 0, 2, 3, 4)))

        x_all = _out_proj(x_all, mod_all, ya, yb, ycf, ycb, cz, vec(c_norm), ydc.reshape(2, n1, WG),
                          ydl.reshape(2, n2, WG), du, vec(d_d), w_glu16, w_out16, vec(g_post1),
                          layer=l, nct=nct, tpl=tpl)
        last = l == depth - 1
        res = _ffn(x_all, mod_all, vec(g_pre2), w_up16, ffn_conv_w, vec(ffn_conv_b), w_dn16, vec(g_post2),
                   layer=l, nct=nct, tpc=tpc, tpl=tpl, split=last)
        if last:
            y_prompt, y_sample = res[0].reshape(b1, l1, d), res[1].reshape(b2, l2, d)
        else:
            x_all = res

        cache = lambda t, nh: jnp.transpose(
            t[:nct].reshape(b1, tpc, nh, HEAD_DIM, TM), (0, 1, 4, 2, 3)).reshape(b1, l1, nh, HEAD_DIM)
        new_ak.append(cache(ka32, N_HEADS))
        new_av.append(cache(va32, N_HEADS))
        new_bk.append(cache(kb32, N_HEADS // 2))
        new_bv.append(cache(vb32, N_HEADS // 2))

    st = lambda xs: jnp.stack(xs, axis=1)
    return (y_prompt, y_sample, st(new_ak), st(new_av), st(new_bk), st(new_bv), st(new_ssd), st(new_s5))
```

```python
import functools
import math

import numpy as np
import jax
import jax.numpy as jnp
from jax import lax
from jax.experimental import pallas as pl
from jax.experimental.pallas import tpu as pltpu

F32 = jnp.float32
BF16 = jnp.bfloat16

EPS = 1e-6
LOG2E = math.log2(math.e)
GRID_W = 64
ROPE_BASE = 10000.0
HEAD_DIM = 64
A_HALF = HEAD_DIM // 2
N_HEADS = 4
WG = N_HEADS * HEAD_DIM
C_GROUPS = 2
C_STATE = 64
C_CHUNK = 128
D_GROUP = 16
D_GROUPS = WG // D_GROUP
D_STATE = 64
S5_LANES = D_GROUPS * D_STATE * 2

TM = 256
TQ_LATENT = 512
SUBLANES = 8
VMEM_LIMIT = 56 * 1024 * 1024

_C_AQ, _C_AK, _C_AV, _C_BQ, _C_BK, _C_BV, _C_CZ, _C_CX, _C_DU, _C_DT, _C_END = (
    0, 256, 512, 768, 1024, 1280, 1536, 1792, 2304, 2560, 2688)


def _sigmoid(x):
    return 1.0 / (1.0 + jnp.exp(-x))


def _silu(x):
    return x * _sigmoid(x)


def _cparams(sem):
    return pltpu.CompilerParams(dimension_semantics=sem, vmem_limit_bytes=VMEM_LIMIT)


def _lane_mask(width, lo, hi):
    lane = lax.broadcasted_iota(jnp.int32, (1, width), 1)
    return (lane >= lo) & (lane < hi)


def _rms(x, g):
    return (x * lax.rsqrt(jnp.mean(x * x, axis=-1, keepdims=True) + EPS)) * g


def _mod_row(nct, tpl):
    return lambda i: jnp.where(i < nct, 0, 1 + jnp.maximum(i - nct, 0) // tpl)


def _mod_kernel(c_ref, w_ref, b_ref, o_ref):
    s = _silu(c_ref[...])
    o_ref[0] = jnp.dot(s.astype(BF16), w_ref[0].astype(BF16), preferred_element_type=F32) + b_ref[0]


def _modulation(cond8, w_mod, b_mod):
    depth, d, n = w_mod.shape
    tn = 1536
    return pl.pallas_call(
        _mod_kernel,
        grid=(depth, n // tn),
        in_specs=[pl.BlockSpec((SUBLANES, d), lambda l, j: (0, 0)),
                  pl.BlockSpec((1, d, tn), lambda l, j: (l, 0, j)),
                  pl.BlockSpec((1, 1, tn), lambda l, j: (l, 0, j))],
        out_specs=pl.BlockSpec((1, SUBLANES, tn), lambda l, j: (l, 0, j)),
        out_shape=jax.ShapeDtypeStruct((depth, SUBLANES, n), F32),
        compiler_params=_cparams(("parallel", "parallel")),
        name="modulation",
    )(cond8, w_mod, b_mod.reshape(depth, 1, n))


def _x_specs(x, nct):
    if isinstance(x, tuple):
        d = x[0].shape[1]
        return [pl.BlockSpec((TM, d), lambda i: (jnp.minimum(i, nct - 1), 0)),
                pl.BlockSpec((TM, d), lambda i: (jnp.maximum(i - nct, 0), 0))], list(x)
    return [pl.BlockSpec((TM, x.shape[1]), lambda i: (i, 0))], [x]


def _load_x(x_refs, nct):
    if len(x_refs) == 2:
        return jnp.where(pl.program_id(0) < nct, x_refs[0][...], x_refs[1][...])
    return x_refs[0][...]


def _inproj_kernel(*refs, nct, n_x):
    x_refs = refs[:n_x]
    (mod_ref, g_ref, w_ref, qn_ref, kn_ref, ones_ref, rt_ref, ct_ref,
     qa_ref, ka_ref, va_ref, qb_ref, kb_ref, vb_ref,
     ka32_ref, va32_ref, kb32_ref, vb32_ref, cz_ref, cx_ref, du_ref, dt_ref) = refs[n_x:]
    i = pl.program_id(0)
    mod = mod_ref[0, 0]
    h = _rms(_load_x(x_refs, nct), g_ref[0]) * (1.0 + mod[1:2]) + mod[0:1]
    p = jnp.dot(h.astype(BF16), w_ref[0], preferred_element_type=F32)
    va_ref[...] = p[:, _C_AV:_C_BQ].astype(BF16)
    vb_ref[...] = p[:, _C_BV:_C_CZ].astype(BF16)
    kv_rows = lambda t: jnp.concatenate([t[0:HEAD_DIM], t[2 * HEAD_DIM:3 * HEAD_DIM]], axis=0)
    va32_ref[0] = p[:, _C_AV:_C_BQ].T
    vb32_ref[0] = kv_rows(p[:, _C_BV:_C_CZ].T)
    cz_ref[...] = p[:, _C_CZ:_C_CX]
    cx_ref[...] = p[:, _C_CX:_C_DU]
    du_ref[...] = p[:, _C_DU:_C_DT]
    dt_ref[...] = p[:, _C_DT:_C_END]

    def headnorm(t, gain):
        sq = t * t
        hi = sq.astype(BF16)
        lo = (sq - hi.astype(F32)).astype(BF16)
        ms = (jnp.dot(hi, ones_ref[...], preferred_element_type=F32)
              + jnp.dot(lo, ones_ref[...], preferred_element_type=F32))
        return (t * lax.rsqrt(ms + EPS)) * gain

    aq = p[:, _C_AQ:_C_AK] * (A_HALF ** -0.5 * LOG2E)
    ak = p[:, _C_AK:_C_AV]
    bq = headnorm(p[:, _C_BQ:_C_BK], qn_ref[0]) * (HEAD_DIM ** -0.5 * LOG2E)
    bk = headnorm(p[:, _C_BK:_C_BV], kn_ref[0])
    ka32_ref[0] = ak.T
    kb32_ref[0] = kv_rows(bk.T)

    rows = TM // GRID_W
    is_lat = i >= nct
    rt = rt_ref[0]
    lane = lax.broadcasted_iota(jnp.int32, (1, WG), 1)
    row_lane_a = (lane % (A_HALF // 2)) < A_HALF // 4
    row_lane_b = (lane % (HEAD_DIM // 2)) < HEAD_DIM // 4

    def table(k, row_lane):
        ctk = jnp.where(is_lat, ct_ref[k], 1.0 if k % 3 == 0 else 0.0)
        parts = [jnp.where(row_lane, rt[k, r:r + 1, :], ctk) for r in range(rows)]
        return jnp.concatenate(parts, axis=0)

    def rope(t, base, row_lane, dist):
        return (t * table(base, row_lane)
                + pltpu.roll(t, WG - dist, 1) * table(base + 1, row_lane)
                + pltpu.roll(t, dist, 1) * table(base + 2, row_lane))

    qa_ref[...] = rope(aq, 0, row_lane_a, A_HALF // 2).astype(BF16)
    ka_ref[...] = rope(ak, 0, row_lane_a, A_HALF // 2).astype(BF16)
    qb_ref[...] = rope(bq, 3, row_lane_b, HEAD_DIM // 2).astype(BF16)
    kb_ref[...] = rope(bk, 3, row_lane_b, HEAD_DIM // 2).astype(BF16)


def _rope_tables(l2):
    rows = l2 // GRID_W
    lane = np.arange(WG)

    def one(block, n):
        freqs = (np.float32(ROPE_BASE) ** (-np.arange(n, dtype=np.float32) / np.float32(n))).astype(np.float32)
        p = lane % block
        idx = p % (block // 2)
        first = p < (block // 2)
        f = freqs[idx % n]

        def tabs(pos):
            ang = (pos[:, None].astype(np.float32) * f[None, :]).astype(np.float32)
            c, s = np.cos(ang).astype(np.float32), np.sin(ang).astype(np.float32)
            return [c, np.where(first[None, :], -s, 0.0).astype(np.float32),
                    np.where(first[None, :], 0.0, s).astype(np.float32)]

        return tabs(np.arange(rows)), tabs(np.arange(GRID_W))

    ra, ca = one(A_HALF, A_HALF // 4)
    rb, cb = one(HEAD_DIM, HEAD_DIM // 4)
    rt = np.stack(ra + rb, axis=0)
    ct = np.stack(ca + cb, axis=0)
    rpt = TM // GRID_W
    npt = l2 // TM
    rt = rt.reshape(6, npt, rpt, WG).transpose(1, 0, 2, 3)
    rt = np.concatenate([rt, np.zeros((npt, 6, SUBLANES - rpt, WG), np.float32)], axis=2)
    ident = np.zeros((1, 6, SUBLANES, WG), np.float32)
    ident[0, 0::3] = 1.0
    return jnp.asarray(np.concatenate([rt, ident], axis=0)), jnp.asarray(ct)


def _in_proj(x_all, mod_all, g_pre1, w_in_p, qn, kn, ones_blk, rt, ct, *, layer, nct, tpl):
    x_specs, x_args = _x_specs(x_all, nct)
    n = sum(a.shape[0] for a in x_args)
    d = x_args[0].shape[1]
    nt = n // TM
    row = lambda i: (i, 0)
    const2 = lambda i: (0, 0)
    lay3 = lambda i: (layer, 0, 0)
    mrow = _mod_row(nct, tpl)
    ctx_blk = lambda i: (jnp.minimum(i, nct), 0, 0)
    kvw = WG // 2
    out_specs = ([pl.BlockSpec((TM, WG), row)] * 6
                 + [pl.BlockSpec((1, w, TM), ctx_blk) for w in (WG, WG, kvw, kvw)]
                 + [pl.BlockSpec((TM, w), row) for w in (WG, 2 * WG, WG, 128)])
    out_shape = ([jax.ShapeDtypeStruct((n, WG), BF16)] * 6
                 + [jax.ShapeDtypeStruct((nct + 1, w, TM), F32) for w in (WG, WG, kvw, kvw)]
                 + [jax.ShapeDtypeStruct((n, w), F32) for w in (WG, 2 * WG, WG, 128)])
    return pl.pallas_call(
        functools.partial(_inproj_kernel, nct=nct, n_x=len(x_args)),
        grid=(nt,),
        in_specs=x_specs + [
                  pl.BlockSpec((1, 1, 6, d), lambda i: (layer, mrow(i), 0, 0)),
                  pl.BlockSpec((1, 1, d), lay3),
                  pl.BlockSpec((1,) + w_in_p.shape[1:], lay3),
                  pl.BlockSpec((1, 1, WG), lay3),
                  pl.BlockSpec((1, 1, WG), lay3),
                  pl.BlockSpec((WG, WG), const2),
                  pl.BlockSpec((1, 6, SUBLANES, WG),
                               lambda i: (jnp.where(i < nct, tpl, jnp.maximum(i - nct, 0) % tpl), 0, 0, 0)),
                  pl.BlockSpec(ct.shape, lambda i: (0, 0, 0))],
        out_specs=out_specs,
        out_shape=out_shape,
        compiler_params=_cparams(("arbitrary",)),
        name="in_proj",
    )(*x_args, mod_all, g_pre1, w_in_p, qn, kn, ones_blk, rt, ct)


def _attn_kernel(*refs, diff, tq, lk_new, lc, ck, lam_init):
    it = iter(refs)
    q_ref, k_ref, v_ref = next(it), next(it), next(it)
    kc_ref = vc_ref = None
    if lc:
        kc_ref, vc_ref = next(it).at[0, 0], next(it).at[0, 0]
    lam_ref = sub_ref = None
    if diff:
        lam_ref, sub_ref = next(it), next(it)
    o_ref = next(it)
    nsm = 2 if diff else 1
    s_ref = tuple(tuple(next(it) for _ in range(nsm)) for _ in range(2))
    m_ref = tuple(tuple(next(it) for _ in range(nsm)) for _ in range(2))
    oall_ref = next(it)

    blocks = [(k_ref, v_ref, r0, min(ck, lk_new - r0), r0) for r0 in range(0, lk_new, ck)]
    if lc:
        blocks.append((kc_ref, vc_ref, 0, lc, lk_new))

    width = HEAD_DIM // nsm
    q = q_ref[...]
    lane = lax.broadcasted_iota(jnp.int32, (1, WG), 1)
    if diff:
        lp = lam_ref[0]
        lam = (jnp.exp(jnp.sum(lp[0:1] * lp[1:2], axis=-1, keepdims=True))
               - jnp.exp(jnp.sum(lp[2:3] * lp[3:4], axis=-1, keepdims=True)) + lam_init)

    def lane_tiles(x):
        return [x[:, t * 128:(t + 1) * 128] for t in range(x.shape[1] // 128)]

    def step(h, slot, scores, values):
        if scores:
            qms = [jnp.where((lane >= h * HEAD_DIM + e * width) & (lane < h * HEAD_DIM + (e + 1) * width),
                             q, jnp.zeros_like(q)) for e in range(nsm)]
            mrun = [jnp.full((tq, 128), -jnp.inf, F32) for _ in range(nsm)]
        if values:
            prev = s_ref[1 - slot]
            m = [jnp.max(m_ref[1 - slot][e][...], axis=-1, keepdims=True) for e in range(nsm)]
            lrun = [jnp.zeros((tq, 128), F32) for _ in range(nsm)]
            acc = jnp.zeros((tq, WG), F32)
        for kr, vr, r0, rows, c0 in blocks:
            for e in range(nsm):
                if scores:
                    s = lax.dot_general(qms[e], kr[r0:r0 + rows, :], (((1,), (1,)), ((), ())),
                                        preferred_element_type=F32)
                    s_ref[slot][e][:, c0:c0 + rows] = s
                    for t in lane_tiles(s):
                        mrun[e] = jnp.maximum(mrun[e], t)
                if values:
                    p = jnp.exp2(prev[e][:, c0:c0 + rows] - m[e])
                    for t in lane_tiles(p):
                        lrun[e] = lrun[e] + t
                    if diff:
                        prev[e][:, c0:c0 + rows] = p
                    else:
                        acc = acc + jnp.dot(p.astype(BF16), vr[r0:r0 + rows, :], preferred_element_type=F32)
        if scores:
            for e in range(nsm):
                m_ref[slot][e][...] = mrun[e]
        if values:
            inv = [1.0 / jnp.sum(lrun[e], axis=-1, keepdims=True) for e in range(nsm)]
            if diff:
                for _, vr, r0, rows, c0 in blocks:
                    p = prev[0][:, c0:c0 + rows] * inv[0] - prev[1][:, c0:c0 + rows] * (lam * inv[1])
                    acc = acc + jnp.dot(p.astype(BF16), vr[r0:r0 + rows, :], preferred_element_type=F32)
            else:
                acc = acc * inv[0]
            oall_ref[h - 1] = acc

    step(0, 0, True, False)

    def body(g, carry):
        step(2 * g + 1, 1, True, True)
        step(2 * g + 2, 0, True, True)
        return carry

    lax.fori_loop(0, (N_HEADS - 2) // 2, body, 0)
    step(N_HEADS - 1, 1, True, True)
    step(N_HEADS, 0, False, True)

    out = jnp.zeros((tq, WG), F32)
    for h in range(N_HEADS):
        hm = _lane_mask(WG, h * HEAD_DIM, (h + 1) * HEAD_DIM)
        o = oall_ref[h]
        if diff:
            ms = jnp.sum(jnp.where(hm, o * o, 0.0), axis=-1, keepdims=True) * (1.0 / HEAD_DIM)
            o = ((o * lax.rsqrt(ms + EPS)) * sub_ref[0]) * (1.0 - lam_init)
        out = jnp.where(hm, o, out)
    o_ref[...] = out


def _attention(q, k, v, *, diff, n_seq, seq_len, row_off, tq, ck, layer, ctx=None,
               lam=None, subln=None, lam_init=0.0):
    assert row_off % seq_len == 0 and seq_len % tq == 0
    qpb = seq_len // tq
    q_map = lambda s, j: (row_off // tq + s * qpb + j, 0)
    kv_map = lambda s, j: (row_off // seq_len + s, 0)
    in_specs = [pl.BlockSpec((tq, WG), q_map), pl.BlockSpec((seq_len, WG), kv_map),
                pl.BlockSpec((seq_len, WG), kv_map)]
    args = [q, k, v]
    lc = 0
    if ctx is not None:
        kc, vc = ctx
        lc = kc.shape[2]
        cmap = lambda s, j: (s, layer, 0, 0)
        in_specs += [pl.BlockSpec((1, 1, lc, WG), cmap), pl.BlockSpec((1, 1, lc, WG), cmap)]
        args += [kc, vc]
    if diff:
        lay3 = lambda s, j: (layer, 0, 0)
        in_specs += [pl.BlockSpec((1,) + lam.shape[1:], lay3), pl.BlockSpec((1, 1, WG), lay3)]
        args += [lam, subln]
    lk = seq_len + lc
    return pl.pallas_call(
        functools.partial(_attn_kernel, diff=diff, tq=tq, lk_new=seq_len, lc=lc, ck=min(ck, seq_len),
                          lam_init=lam_init),
        grid=(n_seq, qpb),
        in_specs=in_specs,
        out_specs=pl.BlockSpec((tq, WG), lambda s, j: (s * qpb + j, 0)),
        out_shape=jax.ShapeDtypeStruct((n_seq * seq_len, WG), F32),
        scratch_shapes=([pltpu.VMEM((tq, lk), F32)] * (4 if diff else 2)
                        + [pltpu.VMEM((tq, 128), F32)] * (4 if diff else 2)
                        + [pltpu.VMEM((N_HEADS, tq, WG), F32)]),
        compiler_params=_cparams(("parallel", "parallel")),
        name="attn_diff" if diff else "attn_gqa",
    )(*args)


SSD_PAIR = 2 * C_CHUNK


def _ssd_intra(xbc, dt_raw, dtb, av, *, reverse):
    q = C_CHUNK
    xs = xbc[:, 0:WG]
    bm = xbc[:, WG:WG + 128].astype(BF16)
    cm = xbc[:, WG + 128:WG + 256]
    raw = dt_raw + dtb
    dt = jnp.maximum(raw, 0.0) + jnp.log1p(jnp.exp(-jnp.abs(raw)))
    dta = dt * av
    li = lax.broadcasted_iota(jnp.int32, (q, q), 0)
    si = lax.broadcasted_iota(jnp.int32, (q, q), 1)
    causal = (si >= li) if reverse else (si <= li)
    cum = jnp.dot(causal.astype(F32), dta, preferred_element_type=F32, precision=lax.Precision.HIGHEST)
    cum_t = cum.T
    end = 0 if reverse else q - 1
    d0 = N_HEADS if reverse else 0
    rowh = lax.broadcasted_iota(jnp.int32, (WG, 1), 0) // HEAD_DIM

    y = jnp.zeros((q, WG), F32)
    xw = jnp.zeros((q, WG), F32)
    ecum = jnp.zeros((q, WG), F32)
    cdec = jnp.zeros((WG, 1), F32)
    gmat = None
    for h in range(N_HEADS):
        j = d0 + h
        g = h // (N_HEADS // C_GROUPS)
        if h % (N_HEADS // C_GROUPS) == 0:
            cg = jnp.where(_lane_mask(128, g * C_STATE, (g + 1) * C_STATE), cm, 0.0).astype(BF16)
            gmat = lax.dot_general(cg, bm, (((1,), (1,)), ((), ())), preferred_element_type=F32)
        col = jnp.broadcast_to(cum[:, j:j + 1], (q, 128))
        dtc = jnp.broadcast_to(dt[:, j:j + 1], (q, 128))
        wide = lambda a: jnp.concatenate([a, a], axis=1)
        seg = col - cum_t[j:j + 1, :]
        decay = jnp.where(causal, jnp.exp(jnp.where(causal, seg, 0.0)), 0.0)
        hm = _lane_mask(WG, h * HEAD_DIM, (h + 1) * HEAD_DIM)
        xdt = jnp.where(hm, xs * wide(dtc), 0.0)
        y = y + jnp.dot((gmat * decay).astype(BF16), xdt.astype(BF16), preferred_element_type=F32)
        cend = cum[end:end + 1, j:j + 1]
        xw = xw + xdt * wide(jnp.exp(cend - col))
        ecum = jnp.where(hm, wide(jnp.exp(col)), ecum)
        cdec = jnp.where(rowh == h, jnp.exp(cend), cdec)

    st = lax.dot_general(xw.astype(BF16), bm, (((0,), (0,)), ((), ())), preferred_element_type=F32)
    return y, xs, cm.astype(BF16), ecum, cdec, st


def _ssd_direction(x_ref, xp_ref, xn_ref, dt_ref, h0_ref, cw_ref, cb_ref, dtb_ref, av_ref, s_ref,
                   y_ref, st_ref, skip, p, *, reverse, npc, ppc, ppl):
    q = C_CHUNK
    is_lat = p >= npc
    pos = jnp.where(is_lat, jnp.maximum(p - npc, 0) % ppl, p % ppc)
    nper = jnp.where(is_lat, ppl, ppc)
    enter = (pos == nper - 1) if reverse else (pos == 0)

    @pl.when(enter)
    def _():
        s_ref[...] = h0_ref[0, 0]

    x = x_ref[...]
    rid = lax.broadcasted_iota(jnp.int32, (SSD_PAIR, 1), 0)
    prow = jnp.where(pos > 0, xp_ref[SUBLANES - 1:SUBLANES, :], 0.0)
    nrow = jnp.where(pos < nper - 1, xn_ref[0:1, :], 0.0)
    xm1 = jnp.where(rid == 0, prow, pltpu.roll(x, 1, 0))
    xp1 = jnp.where(rid == SSD_PAIR - 1, nrow, pltpu.roll(x, SSD_PAIR - 1, 0))
    cw = cw_ref[0]
    xbc = _silu(xm1 * cw[0:1] + x * cw[1:2] + xp1 * cw[2:3] + cb_ref[0])

    parts = [_ssd_intra(xbc[k * q:(k + 1) * q], dt_ref[k * q:(k + 1) * q, :], dtb_ref[...], av_ref[...],
                        reverse=reverse) for k in range(2)]
    rowh = lax.broadcasted_iota(jnp.int32, (WG, 1), 0) // HEAD_DIM
    colg = lax.broadcasted_iota(jnp.int32, (1, 128), 1) // C_STATE
    blk = (rowh // (N_HEADS // C_GROUPS)) == colg
    s = s_ref[...]
    for k in ((1, 0) if reverse else (0, 1)):
        y, xs, cm, ecum, cdec, st = parts[k]
        s_msk = jnp.where(blk, s, 0.0).astype(BF16)
        y = y + lax.dot_general(cm, s_msk, (((1,), (1,)), ((), ())), preferred_element_type=F32) * ecum
        if skip is not None:
            y = y + skip * xs
        y_ref[k * q:(k + 1) * q, :] = y
        s = s * cdec + st
    s_ref[...] = s
    st_ref[0] = s


def _ssd_kernel(xf_ref, xpf_ref, xnf_ref, dtf_ref, h0f_ref, xb_ref, xpb_ref, xnb_ref, dtr_ref, h0b_ref,
                cw_ref, cb_ref, dtb_ref, av_ref, dsk_ref,
                yf_ref, yb_ref, stf_ref, stb_ref, sf_ref, sb_ref, *, npc, ppc, ppl):
    t = pl.program_id(0)
    kw = dict(npc=npc, ppc=ppc, ppl=ppl)
    _ssd_direction(xf_ref, xpf_ref, xnf_ref, dtf_ref, h0f_ref, cw_ref, cb_ref, dtb_ref.at[0, 0], av_ref.at[0, 0],
                   sf_ref, yf_ref, stf_ref, dsk_ref[0], t, reverse=False, **kw)
    _ssd_direction(xb_ref, xpb_ref, xnb_ref, dtr_ref, h0b_ref, cw_ref, cb_ref, dtb_ref.at[0, 1], av_ref.at[0, 1],
                   sb_ref, yb_ref, stb_ref, None, pl.num_programs(0) - 1 - t, reverse=True, **kw)


def _ssd(cx, cdt, h0, conv_w, conv_b, dtb, av, dsk, *, layer, npc, ppc, ppl, n_ctx_seq):
    n = cx.shape[0]
    nc = n // SSD_PAIR
    n_seq = h0.shape[1]
    sub = SSD_PAIR // SUBLANES
    nb8 = n // SUBLANES
    lay3 = lambda t: (layer, 0, 0)

    def specs(cidx, d):
        def seq_of(t):
            c = cidx(t)
            return jnp.where(c < npc, c // ppc, n_ctx_seq + jnp.maximum(c - npc, 0) // ppl)

        row = lambda t: (cidx(t), 0)
        ins = [pl.BlockSpec((SSD_PAIR, 2 * WG), row),
               pl.BlockSpec((SUBLANES, 2 * WG), lambda t: (jnp.maximum(cidx(t) * sub - 1, 0), 0)),
               pl.BlockSpec((SUBLANES, 2 * WG), lambda t: (jnp.minimum((cidx(t) + 1) * sub, nb8 - 1), 0)),
               pl.BlockSpec((SSD_PAIR, 128), row),
               pl.BlockSpec((1, 1, WG, 128), lambda t: (d, seq_of(t), 0, 0))]
        outs = [pl.BlockSpec((SSD_PAIR, WG), row), pl.BlockSpec((1, WG, 128), lambda t: (seq_of(t), 0, 0))]
        return ins, outs

    ins_f, outs_f = specs(lambda t: t, 0)
    ins_b, outs_b = specs(lambda t: nc - 1 - t, 1)
    return pl.pallas_call(
        functools.partial(_ssd_kernel, npc=npc, ppc=ppc, ppl=ppl),
        grid=(nc,),
        in_specs=ins_f + ins_b + [pl.BlockSpec((1, 3, 2 * WG), lay3),
                                  pl.BlockSpec((1, 1, 2 * WG), lay3),
                                  pl.BlockSpec((1, 2, 1, 128), lambda t: (layer, 0, 0, 0)),
                                  pl.BlockSpec((1, 2, 1, 128), lambda t: (layer, 0, 0, 0)),
                                  pl.BlockSpec((1, 1, WG), lay3)],
        out_specs=[outs_f[0], outs_b[0], outs_f[1], outs_b[1]],
        out_shape=[jax.ShapeDtypeStruct((n, WG), F32), jax.ShapeDtypeStruct((n, WG), F32),
                   jax.ShapeDtypeStruct((n_seq, WG, 128), F32), jax.ShapeDtypeStruct((n_seq, WG, 128), F32)],
        scratch_shapes=[pltpu.VMEM((WG, 128), F32), pltpu.VMEM((WG, 128), F32)],
        compiler_params=_cparams(("arbitrary",)),
        name="ssd",
    )(cx, cx, cx, cdt, h0, cx, cx, cx, cdt, h0, conv_w, conv_b, dtb, av, dsk)


def _s5_kernel(*refs, tt, nl, nb_in, fold):
    u_cur, u_next = refs[:nb_in], refs[nb_in:2 * nb_in]
    (wb_ref, are_ref, aim_ref, wc_ref, h0r_ref, h0i_ref, y_ref, sr_ref, si_ref,
     ust_ref, ytm_ref, bu_ref, xs_ref, cr_ref, ci_ref) = refs[2 * nb_in:]
    hl = nl // 2
    d = pl.program_id(0)
    j = pl.program_id(2)
    lane = lax.broadcasted_iota(jnp.int32, (1, WG), 1)
    gw = WG // fold

    def project(u_refs, slot):
        for qq in range(fold):
            for b in range(nb_in):
                c = qq * nb_in + b
                u = u_refs[b][...]
                if fold > 1:
                    u = jnp.where((lane >= qq * gw) & (lane < (qq + 1) * gw), u, 0.0)
                for k in range(WG // 128):
                    ust_ref[k, pl.ds(c, tt, stride=SUBLANES), :] = u[:, k * 128:(k + 1) * 128]
        ust = jnp.concatenate([ust_ref[k].astype(BF16) for k in range(WG // 128)], axis=1)
        bu_ref[slot] = jnp.dot(ust, wb_ref[0, 0], preferred_element_type=F32)

    @pl.when(j == 0)
    def _():
        cr_ref[...] = h0r_ref[0, 0]
        ci_ref[...] = h0i_ref[0, 0]
        project(u_cur, 0)

    bu_cur = bu_ref.at[j % 2]
    a_re = are_ref[0, 0]
    a_im = aim_ref[0, 0]

    def step(t, carry):
        xr, xi = carry
        te = t + d * (tt - 1 - 2 * t)
        r0 = pl.multiple_of(te * SUBLANES, SUBLANES)
        nr = a_re * xr - a_im * xi + bu_cur[pl.ds(r0, SUBLANES), 0:hl]
        ni = a_re * xi + a_im * xr + bu_cur[pl.ds(r0, SUBLANES), hl:nl]
        xs_ref[pl.ds(r0, SUBLANES), 0:hl] = nr
        xs_ref[pl.ds(r0, SUBLANES), hl:nl] = ni
        return nr, ni

    xr, xi = lax.fori_loop(0, tt, step, (cr_ref[...], ci_ref[...]), unroll=8)
    cr_ref[...] = xr
    ci_ref[...] = xi
    sr_ref[0, 0] = xr
    si_ref[0, 0] = xi
    y = jnp.dot(xs_ref[...].astype(BF16), wc_ref[0, 0], preferred_element_type=F32)
    for k in range(WG // 128):
        ytm_ref[k] = y[:, k * 128:(k + 1) * 128]

    def chain_rows(c):
        return jnp.concatenate([ytm_ref[k, pl.ds(c, tt, stride=SUBLANES), :] for k in range(WG // 128)], axis=1)

    for b in range(nb_in):
        if fold == 1:
            y_ref[0, b] = chain_rows(b)
        else:
            acc = jnp.zeros((tt, WG), F32)
            for qq in range(fold):
                acc = jnp.where((lane >= qq * gw) & (lane < (qq + 1) * gw), chain_rows(qq * nb_in + b), acc)
            y_ref[0, b] = acc
    project(u_next, (j + 1) % 2)


def _s5_scan(du, wb, a_re, a_im, wc, h0r, h0i, *, layer, n_b, seq_len, row_off, tt, fold):
    nb_in = SUBLANES // fold
    n_slab = n_b // nb_in
    nl = wb.shape[-1]
    hl = nl // 2
    nblk = seq_len // tt
    assert seq_len % tt == 0 and row_off % tt == 0
    tblk = lambda d, j: j + d * (nblk - 1 - 2 * j)
    u_specs = [pl.BlockSpec((tt, WG), functools.partial(
        lambda d, s, j, b, ahead: (row_off // tt + (s * nb_in + b) * nblk
                                   + tblk(d, jnp.minimum(j + ahead, nblk - 1)), 0), b=b, ahead=ahead))
               for ahead in (0, 1) for b in range(nb_in)]
    wmap = lambda d, s, j: (layer, d, 0, 0)
    smap = lambda d, s, j: (d, s, 0, 0)
    st_spec = pl.BlockSpec((1, 1, SUBLANES, hl), smap)
    rows = SUBLANES * tt
    return pl.pallas_call(
        functools.partial(_s5_kernel, tt=tt, nl=nl, nb_in=nb_in, fold=fold),
        grid=(2, n_slab, nblk),
        in_specs=u_specs + [pl.BlockSpec((1, 1, WG, nl), wmap),
                            pl.BlockSpec((1, 1, SUBLANES, hl), wmap),
                            pl.BlockSpec((1, 1, SUBLANES, hl), wmap),
                            pl.BlockSpec((1, 1, nl, WG), wmap),
                            st_spec, st_spec],
        out_specs=[pl.BlockSpec((1, nb_in, tt, WG), lambda d, s, j: (d, s, tblk(d, j), 0)), st_spec, st_spec],
        out_shape=[jax.ShapeDtypeStruct((2, n_b, seq_len, WG), F32),
                   jax.ShapeDtypeStruct((2, n_slab, SUBLANES, hl), F32),
                   jax.ShapeDtypeStruct((2, n_slab, SUBLANES, hl), F32)],
        scratch_shapes=[pltpu.VMEM((WG // 128, rows, 128), F32), pltpu.VMEM((WG // 128, rows, 128), F32),
                        pltpu.VMEM((2, rows, nl), F32), pltpu.VMEM((rows, nl), F32),
                        pltpu.VMEM((SUBLANES, hl), F32), pltpu.VMEM((SUBLANES, hl), F32)],
        compiler_params=_cparams(("parallel", "parallel", "arbitrary")),
        name="s5_scan",
    )(*([du] * (2 * nb_in)), wb, a_re, a_im, wc, h0r, h0i)


def _s5_params(lam_re, lam_im, log_step, b_ri, c_ri):
    lam = lax.complex(lam_re, lam_im)
    a_bar = jnp.exp(lam * jnp.exp(log_step)[..., None])
    b_bar = ((a_bar - 1.0) / lam)[..., None] * lax.complex(b_ri[..., 0], b_ri[..., 1])
    return jnp.real(a_bar), jnp.imag(a_bar), jnp.real(b_bar), jnp.imag(b_bar), c_ri[..., 0], c_ri[..., 1]


def _s5_weights(a_re, a_im, bb_re, bb_im, c_re, c_im, fold, chain_rep):
    lead = a_re.shape[:2]
    m = lead[0] * lead[1]
    gpr = D_GROUPS // fold
    eye = jnp.eye(gpr, dtype=F32)

    def wb_part(bb):
        t = bb.reshape(m, fold, gpr, D_STATE, D_GROUP)
        src = jnp.transpose(t, (0, 1, 2, 4, 3))
        w = jnp.where(eye[None, None, :, None, :, None] > 0,
                      jnp.broadcast_to(src[:, :, :, :, None, :], (m, fold, gpr, D_GROUP, gpr, D_STATE)), 0.0)
        return w.reshape(m, WG, gpr * D_STATE)

    wb = jnp.concatenate([wb_part(bb_re), wb_part(bb_im)], axis=-1)

    def wc_part(cc):
        t = cc.reshape(m, fold, gpr, D_GROUP, D_STATE)
        src = jnp.transpose(t, (0, 2, 4, 1, 3))
        w = jnp.where(eye[None, :, None, None, :, None] > 0,
                      jnp.broadcast_to(src[:, :, :, :, None, :], (m, gpr, D_STATE, fold, gpr, D_GROUP)), 0.0)
        return w.reshape(m, gpr * D_STATE, WG)

    wc = jnp.concatenate([wc_part(c_re), -wc_part(c_im)], axis=1)

    def a_rows(a):
        t = a.reshape(m, fold, 1, gpr * D_STATE)
        return jnp.broadcast_to(t, (m, fold, chain_rep, gpr * D_STATE)).reshape(lead + (SUBLANES, gpr * D_STATE))

    nl = 2 * gpr * D_STATE
    return (wb.astype(BF16).reshape(lead + (WG, nl)), wc.astype(BF16).reshape(lead + (nl, WG)),
            a_rows(a_re), a_rows(a_im))


def _outproj_kernel(*refs, nct, n_x):
    x_refs = refs[:n_x]
    (mod_ref, yac_ref, yal_ref, ybc_ref, ybl_ref, ycf_ref, ycb_ref, cz_ref, cn_ref,
     ydc0_ref, ydc1_ref, ydl0_ref, ydl1_ref, du_ref, dd_ref, wglu_ref, wout_ref, g_ref, o_ref) = refs[n_x:]
    mod = mod_ref[0, 0]
    yc = _rms((ycf_ref[...] + ycb_ref[...]) * _silu(cz_ref[...]), cn_ref[0])
    is_ctx = pl.program_id(0) < nct
    ya = jnp.where(is_ctx, yac_ref[...], yal_ref[...])
    yb = jnp.where(is_ctx, ybc_ref[...], ybl_ref[...])
    yd = jnp.where(is_ctx, ydc0_ref[0] + ydc1_ref[0], ydl0_ref[0] + ydl1_ref[0]) + dd_ref[0] * du_ref[...]
    yd = yd * (0.5 * (1.0 + jnp.tanh(math.sqrt(2.0 / math.pi) * (yd + 0.044715 * (yd * yd * yd)))))
    gl = jnp.dot(yd.astype(BF16), wglu_ref[0], preferred_element_type=F32)
    yd = gl[:, 0:WG] * _sigmoid(gl[:, WG:2 * WG])
    cat = jnp.concatenate([ya.astype(BF16), yb.astype(BF16), yc.astype(BF16), yd.astype(BF16)], axis=-1)
    y = jnp.dot(cat, wout_ref[0], preferred_element_type=F32)
    o_ref[...] = _load_x(x_refs, nct) + mod[2:3] * _rms(y, g_ref[0])


def _out_proj(x_all, mod_all, ya, yb, ycf, ycb, cz, cnorm, ydc, ydl, du, dd, wglu, wout, g_post1,
              *, layer, nct, tpl):
    x_specs, x_args = _x_specs(x_all, nct)
    n = sum(a.shape[0] for a in x_args)
    d = x_args[0].shape[1]
    row = lambda i: (i, 0)
    lay3 = lambda i: (layer, 0, 0)
    mrow = _mod_row(nct, tpl)
    sm = pl.BlockSpec((TM, WG), row)
    smc = pl.BlockSpec((TM, WG), lambda i: (jnp.minimum(i, nct - 1), 0))
    sml = pl.BlockSpec((TM, WG), lambda i: (jnp.maximum(i - nct, 0), 0))
    ydc_spec = lambda dd_: pl.BlockSpec((1, TM, WG), lambda i: (dd_, jnp.minimum(i, nct - 1), 0))
    ydl_spec = lambda dd_: pl.BlockSpec((1, TM, WG), lambda i: (dd_, jnp.maximum(i - nct, 0), 0))
    return pl.pallas_call(
        functools.partial(_outproj_kernel, nct=nct, n_x=len(x_args)),
        grid=(n // TM,),
        in_specs=x_specs + [pl.BlockSpec((1, 1, 6, d), lambda i: (layer, mrow(i), 0, 0)),
                  smc, sml, smc, sml, sm, sm, sm, pl.BlockSpec((1, 1, WG), lay3),
                  ydc_spec(0), ydc_spec(1), ydl_spec(0), ydl_spec(1), sm,
                  pl.BlockSpec((1, 1, WG), lay3), pl.BlockSpec((1,) + wglu.shape[1:], lay3),
                  pl.BlockSpec((1,) + wout.shape[1:], lay3), pl.BlockSpec((1, 1, d), lay3)],
        out_specs=pl.BlockSpec((TM, d), row),
        out_shape=jax.ShapeDtypeStruct((n, d), F32),
        compiler_params=_cparams(("parallel",)),
        name="out_proj",
    )(*x_args, mod_all, ya[0], ya[1], yb[0], yb[1], ycf, ycb, cz, cnorm, ydc, ydc, ydl, ydl, du, dd,
      wglu, wout, g_post1)


def _ffn_kernel(x_ref, xp_ref, xn_ref, mod_ref, g_ref, wup_ref, cw_ref, cb_ref, wdn_ref, gp_ref,
                *rest, f, fc, nct, tpc, tpl, split):
    if split:
        oc_ref, ol_ref, hext_ref, hbf_ref, act_ref = rest
    else:
        o_ref, hext_ref, hbf_ref, act_ref = rest
    i = pl.program_id(0)
    is_lat = i >= nct
    pos = jnp.where(is_lat, jnp.maximum(i - nct, 0) % tpl, i % tpc)
    nper = jnp.where(is_lat, tpl, tpc)
    mod = mod_ref[0, 0]

    def hfun(x):
        return _rms(x, g_ref[0]) * (1.0 + mod[4:5]) + mod[3:4]

    x = x_ref[...]
    hext_ref[0:SUBLANES, :] = jnp.where(pos > 0, hfun(xp_ref[...]), 0.0)
    hext_ref[SUBLANES:SUBLANES + TM, :] = hfun(x)
    hext_ref[SUBLANES + TM:2 * SUBLANES + TM, :] = jnp.where(pos < nper - 1, hfun(xn_ref[...]), 0.0)
    hbf_ref[...] = hext_ref[...].astype(BF16)
    rows = TM + 2 * SUBLANES

    def conv(u, c0):
        cw = cw_ref[0, :, c0:c0 + fc]
        um1 = pltpu.roll(u, 1, 0)[SUBLANES:SUBLANES + TM]
        up1 = pltpu.roll(u, rows - 1, 0)[SUBLANES:SUBLANES + TM]
        return (um1 * cw[0:1] + u[SUBLANES:SUBLANES + TM] * cw[1:2] + up1 * cw[2:3]
                + cb_ref[0, :, c0:c0 + fc])

    for j in range(f // fc):
        ug = jnp.dot(hbf_ref[...], wup_ref[0, :, j * fc:(j + 1) * fc], preferred_element_type=F32)
        uv = jnp.dot(hbf_ref[...], wup_ref[0, :, f + j * fc:f + (j + 1) * fc], preferred_element_type=F32)
        act_ref[:, j * fc:(j + 1) * fc] = (_silu(conv(ug, j * fc)) * conv(uv, f + j * fc)).astype(BF16)
    acc = jnp.dot(act_ref[...], wdn_ref[0], preferred_element_type=F32)
    out = x + mod[5:6] * _rms(acc, gp_ref[0])
    if split:
        @pl.when(i < nct)
        def _():
            oc_ref[...] = out

        @pl.when(i >= nct)
        def _():
            ol_ref[...] = out
    else:
        o_ref[...] = out


def _ffn(x_all, mod_all, g_pre2, wup, cw, cb, wdn, g_post2, *, layer, nct, tpc, tpl, split):
    n, d = x_all.shape
    f = wdn.shape[1]
    fc = 256
    assert f % fc == 0
    sub = TM // SUBLANES
    nb8 = n // SUBLANES
    row = lambda i: (i, 0)
    lay3 = lambda i: (layer, 0, 0)
    mrow = _mod_row(nct, tpl)
    single = dict(pipeline_mode=pl.Buffered(1))
    if split:
        out_specs = [pl.BlockSpec((TM, d), lambda i: (jnp.minimum(i, nct - 1), 0)),
                     pl.BlockSpec((TM, d), lambda i: (jnp.maximum(i - nct, 0), 0))]
        out_shape = [jax.ShapeDtypeStruct((nct * TM, d), F32), jax.ShapeDtypeStruct((n - nct * TM, d), F32)]
    else:
        out_specs = pl.BlockSpec((TM, d), row)
        out_shape = jax.ShapeDtypeStruct((n, d), F32)
    return pl.pallas_call(
        functools.partial(_ffn_kernel, f=f, fc=fc, nct=nct, tpc=tpc, tpl=tpl, split=split),
        grid=(n // TM,),
        in_specs=[pl.BlockSpec((TM, d), row),
                  pl.BlockSpec((SUBLANES, d), lambda i: (jnp.maximum(i * sub - 1, 0), 0)),
                  pl.BlockSpec((SUBLANES, d), lambda i: (jnp.minimum((i + 1) * sub, nb8 - 1), 0)),
                  pl.BlockSpec((1, 1, 6, d), lambda i: (layer, mrow(i), 0, 0)),
                  pl.BlockSpec((1, 1, d), lay3),
                  pl.BlockSpec((1,) + wup.shape[1:], lay3, **single),
                  pl.BlockSpec((1,) + cw.shape[1:], lay3),
                  pl.BlockSpec((1,) + cb.shape[1:], lay3),
                  pl.BlockSpec((1,) + wdn.shape[1:], lay3, **single),
                  pl.BlockSpec((1, 1, d), lay3)],
        out_specs=out_specs,
        out_shape=out_shape,
        scratch_shapes=[pltpu.VMEM((TM + 2 * SUBLANES, d), F32), pltpu.VMEM((TM + 2 * SUBLANES, d), BF16),
                        pltpu.VMEM((TM, f), BF16)],
        compiler_params=_cparams(("arbitrary",)),
        name="conv_ffn",
    )(x_all, x_all, x_all, mod_all, g_pre2, wup, cw, cb, wdn, g_post2)


def _ssd_expand(st):
    z = jnp.zeros_like(st)
    hpg = N_HEADS // C_GROUPS
    left = jnp.concatenate([st[..., :hpg, :, :], z[..., hpg:, :, :]], axis=-3)
    right = jnp.concatenate([z[..., :hpg, :, :], st[..., hpg:, :, :]], axis=-3)
    return jnp.concatenate([left, right], axis=-1).reshape(st.shape[:-3] + (WG, 2 * C_STATE))


def _ssd_extract(s):
    s = s.reshape(s.shape[:-2] + (N_HEADS, HEAD_DIM, C_GROUPS, C_STATE))
    hpg = N_HEADS // C_GROUPS
    return jnp.stack([s[..., h, :, h // hpg, :] for h in range(N_HEADS)], axis=-3)


def kernel(x_prompt, x_sample, cache_a_k, cache_a_v, cache_b_k, cache_b_v, state_ssd, state_s5,
           c, c_ctx, w_mod, b_mod, g_pre1, g_post1, g_pre2, g_post2, w_in, a_lam, a_subln,
           b_qnorm, b_knorm, c_conv_w, c_conv_b, c_dt_bias, c_a_log, c_d, c_norm,
           d_lam_re, d_lam_im, d_log_step, d_b, d_c, d_d, d_glu, w_out, w_up,
           ffn_conv_w, ffn_conv_b, w_down):
    b1, l1, d = x_prompt.shape
    b2, l2, _ = x_sample.shape
    depth = w_mod.shape[0]
    past = cache_a_k.shape[2]
    n1, n2 = b1 * l1, b2 * l2
    assert d == 4 * WG and l1 % TM == 0 and l2 % TM == 0 and n1 % l2 == 0
    assert b1 % SUBLANES == 0 and SUBLANES % b2 == 0 and 1 + b2 <= SUBLANES
    nct, tpc, tpl = n1 // TM, l1 // TM, l2 // TM
    fold = SUBLANES // b2

    x_all = (x_prompt.reshape(n1, d), x_sample.reshape(n2, d))
    cond8 = jnp.concatenate([c_ctx[None, :], c, jnp.zeros((SUBLANES - 1 - b2, d), F32)], axis=0)
    mod_all = _modulation(cond8, w_mod, b_mod).reshape(depth, SUBLANES, 6, d)

    vec = lambda t: t.reshape(depth, 1, t.shape[-1])
    tile_h = lambda g: jnp.tile(g, (1, N_HEADS)).reshape(depth, 1, WG)
    hd = HEAD_DIM
    wcol = lambda a, b: w_in[:, :, a:b]
    kv_dup = lambda o: [wcol(o, o + hd), wcol(o, o + hd), wcol(o + hd, o + 2 * hd), wcol(o + hd, o + 2 * hd)]
    w_in_p = jnp.concatenate([wcol(0, 1024)] + kv_dup(1024) + kv_dup(1152)
                             + [wcol(1280, 2048), wcol(2056, 2312), wcol(2048, 2056),
                                jnp.zeros((depth, d, 120), F32)], axis=2).astype(BF16)
    w_up16, w_dn16, w_out16, w_glu16 = (t.astype(BF16) for t in (w_up, w_down, w_out, d_glu))
    rt, ct = _rope_tables(l2)
    ones_blk = jnp.asarray(np.kron(np.eye(N_HEADS, dtype=np.float32),
                                   np.full((HEAD_DIM, HEAD_DIM), 1.0 / HEAD_DIM, np.float32))).astype(BF16)
    rep_kv = lambda t: jnp.repeat(t, 2, axis=-2).reshape(*t.shape[:-2], WG)
    cak = cache_a_k.reshape(b2, depth, past, WG).astype(BF16)
    cav = cache_a_v.reshape(b2, depth, past, WG).astype(BF16)
    cbk = rep_kv(cache_b_k).astype(BF16)
    cbv = rep_kv(cache_b_v).astype(BF16)
    qn, kn, subln = tile_h(b_qnorm), tile_h(b_knorm), tile_h(a_subln)
    h0_lat = jnp.transpose(_ssd_expand(state_ssd), (1, 2, 0, 3, 4))
    h0_ssd = jnp.concatenate([jnp.zeros((depth, 2, b1, WG, 2 * C_STATE), F32), h0_lat], axis=2)
    dir_lanes = lambda t: jnp.concatenate(
        [jnp.stack([t[:, 0], jnp.zeros_like(t[:, 0])], axis=1), jnp.stack([jnp.zeros_like(t[:, 1]), t[:, 1]], axis=1),
         jnp.zeros((depth, 2, 128 - 2 * N_HEADS), F32)], axis=-1)[:, :, None, :]
    dtb = dir_lanes(c_dt_bias)
    av = dir_lanes(-jnp.exp(c_a_log))
    dsk = jnp.repeat(c_d, HEAD_DIM, axis=-1).reshape(depth, 1, WG)
    pr = _s5_params(d_lam_re, d_lam_im, d_log_step, d_b, d_c)
    wb1, wc1, ar1, ai1 = _s5_weights(*pr, 1, SUBLANES)
    wb2, wc2, ar2, ai2 = _s5_weights(*pr, fold, b2)
    gpr = D_GROUPS // fold
    zero1 = jnp.zeros((2, b1 // SUBLANES, SUBLANES, S5_LANES // 2), F32)
    h0s = state_s5.reshape(b2, depth, 2, fold, gpr * D_STATE, 2)
    h0s = jnp.transpose(h0s, (1, 2, 3, 0, 4, 5)).reshape(depth, 2, 1, SUBLANES, gpr * D_STATE, 2)

    new_ak, new_av, new_bk, new_bv, new_ssd, new_s5 = [], [], [], [], [], []
    y_prompt = y_sample = None
    for l in range(depth):
        lam_init = 0.8 - 0.6 * math.exp(-0.3 * l)
        qa, ka, va, qb, kb, vb, ka32, va32, kb32, vb32, cz, cx, du, cdt = _in_proj(
            x_all, mod_all, vec(g_pre1), w_in_p, qn, kn, ones_blk, rt, ct, layer=l, nct=nct, tpl=tpl)

        attn_c = functools.partial(_attention, n_seq=b1, seq_len=l1, row_off=0, tq=TM, ck=1024, layer=l)
        attn_l = functools.partial(_attention, n_seq=b2, seq_len=l2, row_off=n1, tq=TQ_LATENT, ck=1024, layer=l)
        diff_kw = dict(diff=True, lam=a_lam, subln=subln, lam_init=lam_init)
        ya = (attn_c(qa, ka, va, **diff_kw), attn_l(qa, ka, va, ctx=(cak, cav), **diff_kw))
        yb = (attn_c(qb, kb, vb, diff=False), attn_l(qb, kb, vb, diff=False, ctx=(cbk, cbv)))

        ycf, ycb, st_f, st_b = _ssd(cx, cdt, h0_ssd[l], c_conv_w, vec(c_conv_b), dtb, av, dsk, layer=l,
                                    npc=n1 // SSD_PAIR, ppc=l1 // SSD_PAIR, ppl=l2 // SSD_PAIR, n_ctx_seq=b1)
        new_ssd.append(jnp.stack([_ssd_extract(st_f[:b1]), _ssd_extract(st_b[:b1])], axis=1))

        ydc, s1r, s1i = _s5_scan(du, wb1, ar1, ai1, wc1, zero1, zero1, layer=l, n_b=b1, seq_len=l1,
                                 row_off=0, tt=min(l1, 128), fold=1)
        ydl, _, _ = _s5_scan(du, wb2, ar2, ai2, wc2, h0s[l, ..., 0], h0s[l, ..., 1], layer=l, n_b=b2,
                             seq_len=l2, row_off=n1, tt=min(l2, 512), fold=fold)
        st1 = jnp.stack([s1r, s1i], axis=-1).reshape(2, b1, D_GROUPS, D_STATE, 2)
        new_s5.append(jnp.transpose(st1, (1, 0, 2, 3, 4)))

        x_all = _out_proj(x_all, mod_all, ya, yb, ycf, ycb, cz, vec(c_norm), ydc.reshape(2, n1, WG),
                          ydl.reshape(2, n2, WG), du, vec(d_d), w_glu16, w_out16, vec(g_post1),
                          layer=l, nct=nct, tpl=tpl)
        last = l == depth - 1
        res = _ffn(x_all, mod_all, vec(g_pre2), w_up16, ffn_conv_w, vec(ffn_conv_b), w_dn16, vec(g_post2),
                   layer=l, nct=nct, tpc=tpc, tpl=tpl, split=last)
        if last:
            y_prompt, y_sample = res[0].reshape(b1, l1, d), res[1].reshape(b2, l2, d)
        else:
            x_all = res

        cache = lambda t, nh: jnp.transpose(
            t[:nct].reshape(b1, tpc, nh, HEAD_DIM, TM), (0, 1, 4, 2, 3)).reshape(b1, l1, nh, HEAD_DIM)
        new_ak.append(cache(ka32, N_HEADS))
        new_av.append(cache(va32, N_HEADS))
        new_bk.append(cache(kb32, N_HEADS // 2))
        new_bv.append(cache(vb32, N_HEADS // 2))

    st = lambda xs: jnp.stack(xs, axis=1)
    return (y_prompt, y_sample, st(new_ak), st(new_av), st(new_bk), st(new_bv), st(new_ssd), st(new_s5))
```

```python
import functools
import math

import numpy as np
import jax
import jax.numpy as jnp
from jax import lax
from jax.experimental import pallas as pl
from jax.experimental.pallas import tpu as pltpu

F32 = jnp.float32
BF16 = jnp.bfloat16

EPS = 1e-6
LOG2E = math.log2(math.e)
GRID_W = 64
ROPE_BASE = 10000.0
HEAD_DIM = 64
A_HALF = HEAD_DIM // 2
N_HEADS = 4
WG = N_HEADS * HEAD_DIM
C_GROUPS = 2
C_STATE = 64
C_CHUNK = 128
D_GROUP = 16
D_GROUPS = WG // D_GROUP
D_STATE = 64
S5_LANES = D_GROUPS * D_STATE * 2

TM = 256
TQ_LATENT = 512
SUBLANES = 8
VMEM_LIMIT = 56 * 1024 * 1024

_C_AQ, _C_AK, _C_AV, _C_BQ, _C_BK, _C_BV, _C_CZ, _C_CX, _C_DU, _C_DT, _C_END = (
    0, 256, 512, 768, 1024, 1280, 1536, 1792, 2304, 2560, 2688)


def _sigmoid(x):
    return 1.0 / (1.0 + jnp.exp(-x))


def _silu(x):
    return x * _sigmoid(x)


def _cparams(sem):
    return pltpu.CompilerParams(dimension_semantics=sem, vmem_limit_bytes=VMEM_LIMIT)


def _lane_mask(width, lo, hi):
    lane = lax.broadcasted_iota(jnp.int32, (1, width), 1)
    return (lane >= lo) & (lane < hi)


def _rms(x, g):
    return (x * lax.rsqrt(jnp.mean(x * x, axis=-1, keepdims=True) + EPS)) * g


def _mod_row(nct, tpl):
    return lambda i: jnp.where(i < nct, 0, 1 + jnp.maximum(i - nct, 0) // tpl)


def _mod_kernel(c_ref, w_ref, b_ref, o_ref):
    s = _silu(c_ref[...])
    o_ref[0] = jnp.dot(s.astype(BF16), w_ref[0].astype(BF16), preferred_element_type=F32) + b_ref[0]


def _modulation(cond8, w_mod, b_mod):
    depth, d, n = w_mod.shape
    tn = 1536
    return pl.pallas_call(
        _mod_kernel,
        grid=(depth, n // tn),
        in_specs=[pl.BlockSpec((SUBLANES, d), lambda l, j: (0, 0)),
                  pl.BlockSpec((1, d, tn), lambda l, j: (l, 0, j)),
                  pl.BlockSpec((1, 1, tn), lambda l, j: (l, 0, j))],
        out_specs=pl.BlockSpec((1, SUBLANES, tn), lambda l, j: (l, 0, j)),
        out_shape=jax.ShapeDtypeStruct((depth, SUBLANES, n), F32),
        compiler_params=_cparams(("parallel", "parallel")),
        name="modulation",
    )(cond8, w_mod, b_mod.reshape(depth, 1, n))


def _x_specs(x, nct):
    if isinstance(x, tuple):
        d = x[0].shape[1]
        return [pl.BlockSpec((TM, d), lambda i: (jnp.minimum(i, nct - 1), 0)),
                pl.BlockSpec((TM, d), lambda i: (jnp.maximum(i - nct, 0), 0))], list(x)
    return [pl.BlockSpec((TM, x.shape[1]), lambda i: (i, 0))], [x]


def _load_x(x_refs, nct):
    if len(x_refs) == 2:
        return jnp.where(pl.program_id(0) < nct, x_refs[0][...], x_refs[1][...])
    return x_refs[0][...]


def _inproj_kernel(*refs, nct, n_x):
    x_refs = refs[:n_x]
    (mod_ref, g_ref, w_ref, qn_ref, kn_ref, ones_ref, rt_ref, ct_ref,
     qa_ref, ka_ref, va_ref, qb_ref, kb_ref, vb_ref,
     ka32_ref, va32_ref, kb32_ref, vb32_ref, cz_ref, cx_ref, du_ref, dt_ref) = refs[n_x:]
    i = pl.program_id(0)
    mod = mod_ref[0, 0]
    h = _rms(_load_x(x_refs, nct), g_ref[0]) * (1.0 + mod[1:2]) + mod[0:1]
    p = jnp.dot(h.astype(BF16), w_ref[0], preferred_element_type=F32)
    va_ref[...] = p[:, _C_AV:_C_BQ].astype(BF16)
    vb_ref[...] = p[:, _C_BV:_C_CZ].astype(BF16)
    kv_rows = lambda t: jnp.concatenate([t[0:HEAD_DIM], t[2 * HEAD_DIM:3 * HEAD_DIM]], axis=0)
    va32_ref[0] = p[:, _C_AV:_C_BQ].T
    vb32_ref[0] = kv_rows(p[:, _C_BV:_C_CZ].T)
    cz_ref[...] = p[:, _C_CZ:_C_CX]
    cx_ref[...] = p[:, _C_CX:_C_DU]
    du_ref[...] = p[:, _C_DU:_C_DT]
    dt_ref[...] = p[:, _C_DT:_C_END]

    def headnorm(t, gain):
        sq = t * t
        hi = sq.astype(BF16)
        lo = (sq - hi.astype(F32)).astype(BF16)
        ms = (jnp.dot(hi, ones_ref[...], preferred_element_type=F32)
              + jnp.dot(lo, ones_ref[...], preferred_element_type=F32))
        return (t * lax.rsqrt(ms + EPS)) * gain

    aq = p[:, _C_AQ:_C_AK] * (A_HALF ** -0.5 * LOG2E)
    ak = p[:, _C_AK:_C_AV]
    bq = headnorm(p[:, _C_BQ:_C_BK], qn_ref[0]) * (HEAD_DIM ** -0.5 * LOG2E)
    bk = headnorm(p[:, _C_BK:_C_BV], kn_ref[0])
    ka32_ref[0] = ak.T
    kb32_ref[0] = kv_rows(bk.T)

    rows = TM // GRID_W
    is_lat = i >= nct
    rt = rt_ref[0]
    lane = lax.broadcasted_iota(jnp.int32, (1, WG), 1)
    row_lane_a = (lane % (A_HALF // 2)) < A_HALF // 4
    row_lane_b = (lane % (HEAD_DIM // 2)) < HEAD_DIM // 4

    def table(k, row_lane):
        ctk = jnp.where(is_lat, ct_ref[k], 1.0 if k % 3 == 0 else 0.0)
        parts = [jnp.where(row_lane, rt[k, r:r + 1, :], ctk) for r in range(rows)]
        return jnp.concatenate(parts, axis=0)

    def rope(t, base, row_lane, dist):
        return (t * table(base, row_lane)
                + pltpu.roll(t, WG - dist, 1) * table(base + 1, row_lane)
                + pltpu.roll(t, dist, 1) * table(base + 2, row_lane))

    qa_ref[...] = rope(aq, 0, row_lane_a, A_HALF // 2).astype(BF16)
    ka_ref[...] = rope(ak, 0, row_lane_a, A_HALF // 2).astype(BF16)
    qb_ref[...] = rope(bq, 3, row_lane_b, HEAD_DIM // 2).astype(BF16)
    kb_ref[...] = rope(bk, 3, row_lane_b, HEAD_DIM // 2).astype(BF16)


def _rope_tables(l2):
    rows = l2 // GRID_W
    lane = np.arange(WG)

    def one(block, n):
        freqs = (np.float32(ROPE_BASE) ** (-np.arange(n, dtype=np.float32) / np.float32(n))).astype(np.float32)
        p = lane % block
        idx = p % (block // 2)
        first = p < (block // 2)
        f = freqs[idx % n]

        def tabs(pos):
            ang = (pos[:, None].astype(np.float32) * f[None, :]).astype(np.float32)
            c, s = np.cos(ang).astype(np.float32), np.sin(ang).astype(np.float32)
            return [c, np.where(first[None, :], -s, 0.0).astype(np.float32),
                    np.where(first[None, :], 0.0, s).astype(np.float32)]

        return tabs(np.arange(rows)), tabs(np.arange(GRID_W))

    ra, ca = one(A_HALF, A_HALF // 4)
    rb, cb = one(HEAD_DIM, HEAD_DIM // 4)
    rt = np.stack(ra + rb, axis=0)
    ct = np.stack(ca + cb, axis=0)
    rpt = TM // GRID_W
    npt = l2 // TM
    rt = rt.reshape(6, npt, rpt, WG).transpose(1, 0, 2, 3)
    rt = np.concatenate([rt, np.zeros((npt, 6, SUBLANES - rpt, WG), np.float32)], axis=2)
    ident = np.zeros((1, 6, SUBLANES, WG), np.float32)
    ident[0, 0::3] = 1.0
    return jnp.asarray(np.concatenate([rt, ident], axis=0)), jnp.asarray(ct)


def _in_proj(x_all, mod_all, g_pre1, w_in_p, qn, kn, ones_blk, rt, ct, *, layer, nct, tpl):
    x_specs, x_args = _x_specs(x_all, nct)
    n = sum(a.shape[0] for a in x_args)
    d = x_args[0].shape[1]
    nt = n // TM
    row = lambda i: (i, 0)
    const2 = lambda i: (0, 0)
    lay3 = lambda i: (layer, 0, 0)
    mrow = _mod_row(nct, tpl)
    ctx_blk = lambda i: (jnp.minimum(i, nct), 0, 0)
    kvw = WG // 2
    out_specs = ([pl.BlockSpec((TM, WG), row)] * 6
                 + [pl.BlockSpec((1, w, TM), ctx_blk) for w in (WG, WG, kvw, kvw)]
                 + [pl.BlockSpec((TM, w), row) for w in (WG, 2 * WG, WG, 128)])
    out_shape = ([jax.ShapeDtypeStruct((n, WG), BF16)] * 6
                 + [jax.ShapeDtypeStruct((nct + 1, w, TM), F32) for w in (WG, WG, kvw, kvw)]
                 + [jax.ShapeDtypeStruct((n, w), F32) for w in (WG, 2 * WG, WG, 128)])
    return pl.pallas_call(
        functools.partial(_inproj_kernel, nct=nct, n_x=len(x_args)),
        grid=(nt,),
        in_specs=x_specs + [
                  pl.BlockSpec((1, 1, 6, d), lambda i: (layer, mrow(i), 0, 0)),
                  pl.BlockSpec((1, 1, d), lay3),
                  pl.BlockSpec((1,) + w_in_p.shape[1:], lay3),
                  pl.BlockSpec((1, 1, WG), lay3),
                  pl.BlockSpec((1, 1, WG), lay3),
                  pl.BlockSpec((WG, WG), const2),
                  pl.BlockSpec((1, 6, SUBLANES, WG),
                               lambda i: (jnp.where(i < nct, tpl, jnp.maximum(i - nct, 0) % tpl), 0, 0, 0)),
                  pl.BlockSpec(ct.shape, lambda i: (0, 0, 0))],
        out_specs=out_specs,
        out_shape=out_shape,
        compiler_params=_cparams(("arbitrary",)),
        name="in_proj",
    )(*x_args, mod_all, g_pre1, w_in_p, qn, kn, ones_blk, rt, ct)


def _attn_kernel(*refs, diff, tq, lk_new, lc, ck, lam_init):
    it = iter(refs)
    q_ref, k_ref, v_ref = next(it), next(it), next(it)
    kc_ref = vc_ref = None
    if lc:
        kc_ref, vc_ref = next(it).at[0, 0], next(it).at[0, 0]
    lam_ref = sub_ref = None
    if diff:
        lam_ref, sub_ref = next(it), next(it)
    o_ref = next(it)
    nsm = 2 if diff else 1
    nsub = q_ref.shape[0] // tq
    assert nsub == 1 or (tq == lk_new and not lc)
    width = HEAD_DIM // nsm
    subs = []
    for u in range(nsub):
        s_u = tuple(tuple(next(it) for _ in range(nsm)) for _ in range(2))
        m_u = tuple(tuple(next(it) for _ in range(nsm)) for _ in range(2))
        oall_u = next(it)
        blocks_u = [(k_ref, v_ref, u * lk_new + r0, min(ck, lk_new - r0), r0) for r0 in range(0, lk_new, ck)]
        if lc:
            blocks_u.append((kc_ref, vc_ref, 0, lc, lk_new))
        subs.append((q_ref[u * tq:(u + 1) * tq, :], blocks_u, s_u, m_u, oall_u))

    lane = lax.broadcasted_iota(jnp.int32, (1, WG), 1)
    if diff:
        lp = lam_ref[0]
        lam = (jnp.exp(jnp.sum(lp[0:1] * lp[1:2], axis=-1, keepdims=True))
               - jnp.exp(jnp.sum(lp[2:3] * lp[3:4], axis=-1, keepdims=True)) + lam_init)

    def lane_tiles(x):
        return [x[:, t * 128:(t + 1) * 128] for t in range(x.shape[1] // 128)]

    def step(h, slot, scores, values):
        for q, blocks, s_ref, m_ref, oall_ref in subs:
            step_one(q, blocks, s_ref, m_ref, oall_ref, h, slot, scores, values)

    def step_one(q, blocks, s_ref, m_ref, oall_ref, h, slot, scores, values):
        if scores:
            qms = [jnp.where((lane >= h * HEAD_DIM + e * width) & (lane < h * HEAD_DIM + (e + 1) * width),
                             q, jnp.zeros_like(q)) for e in range(nsm)]
            mrun = [jnp.full((tq, 128), -jnp.inf, F32) for _ in range(nsm)]
        if values:
            prev = s_ref[1 - slot]
            m = [jnp.max(m_ref[1 - slot][e][...], axis=-1, keepdims=True) for e in range(nsm)]
            lrun = [jnp.zeros((tq, 128), F32) for _ in range(nsm)]
            acc = jnp.zeros((tq, WG), F32)
        for kr, vr, r0, rows, c0 in blocks:
            for e in range(nsm):
                if scores:
                    s = lax.dot_general(qms[e], kr[r0:r0 + rows, :], (((1,), (1,)), ((), ())),
                                        preferred_element_type=F32)
                    s_ref[slot][e][:, c0:c0 + rows] = s
                    for t in lane_tiles(s):
                        mrun[e] = jnp.maximum(mrun[e], t)
                if values:
                    p = jnp.exp2(prev[e][:, c0:c0 + rows] - m[e])
                    for t in lane_tiles(p):
                        lrun[e] = lrun[e] + t
                    if diff:
                        prev[e][:, c0:c0 + rows] = p
                    else:
                        acc = acc + jnp.dot(p.astype(BF16), vr[r0:r0 + rows, :], preferred_element_type=F32)
        if scores:
            for e in range(nsm):
                m_ref[slot][e][...] = mrun[e]
        if values:
            inv = [1.0 / jnp.sum(lrun[e], axis=-1, keepdims=True) for e in range(nsm)]
            if diff:
                for _, vr, r0, rows, c0 in blocks:
                    p = prev[0][:, c0:c0 + rows] * inv[0] - prev[1][:, c0:c0 + rows] * (lam * inv[1])
                    acc = acc + jnp.dot(p.astype(BF16), vr[r0:r0 + rows, :], preferred_element_type=F32)
            else:
                acc = acc * inv[0]
            oall_ref[h - 1] = acc

    step(0, 0, True, False)

    def body(g, carry):
        step(2 * g + 1, 1, True, True)
        step(2 * g + 2, 0, True, True)
        return carry

    lax.fori_loop(0, (N_HEADS - 2) // 2, body, 0)
    step(N_HEADS - 1, 1, True, True)
    step(N_HEADS, 0, False, True)

    for u, (_, _, _, _, oall_ref) in enumerate(subs):
        out = jnp.zeros((tq, WG), F32)
        for h in range(N_HEADS):
            hm = _lane_mask(WG, h * HEAD_DIM, (h + 1) * HEAD_DIM)
            o = oall_ref[h]
            if diff:
                ms = jnp.sum(jnp.where(hm, o * o, 0.0), axis=-1, keepdims=True) * (1.0 / HEAD_DIM)
                o = ((o * lax.rsqrt(ms + EPS)) * sub_ref[0]) * (1.0 - lam_init)
            out = jnp.where(hm, o, out)
        o_ref[u * tq:(u + 1) * tq, :] = out


def _attention(q, k, v, *, diff, n_seq, seq_len, row_off, tq, ck, layer, ctx=None,
               lam=None, subln=None, lam_init=0.0, nsub=1):
    assert row_off % (nsub * seq_len) == 0 and seq_len % tq == 0 and n_seq % nsub == 0
    assert nsub == 1 or (tq == seq_len and ctx is None)
    qpb = seq_len // tq
    q_map = lambda s, j: (row_off // (nsub * tq) + s * qpb + j, 0)
    kv_map = lambda s, j: (row_off // (nsub * seq_len) + s, 0)
    in_specs = [pl.BlockSpec((nsub * tq, WG), q_map), pl.BlockSpec((nsub * seq_len, WG), kv_map),
                pl.BlockSpec((nsub * seq_len, WG), kv_map)]
    args = [q, k, v]
    lc = 0
    if ctx is not None:
        kc, vc = ctx
        lc = kc.shape[2]
        cmap = lambda s, j: (s, layer, 0, 0)
        in_specs += [pl.BlockSpec((1, 1, lc, WG), cmap), pl.BlockSpec((1, 1, lc, WG), cmap)]
        args += [kc, vc]
    if diff:
        lay3 = lambda s, j: (layer, 0, 0)
        in_specs += [pl.BlockSpec((1,) + lam.shape[1:], lay3), pl.BlockSpec((1, 1, WG), lay3)]
        args += [lam, subln]
    lk = seq_len + lc
    return pl.pallas_call(
        functools.partial(_attn_kernel, diff=diff, tq=tq, lk_new=seq_len, lc=lc, ck=min(ck, seq_len),
                          lam_init=lam_init),
        grid=(n_seq // nsub, qpb),
        in_specs=in_specs,
        out_specs=pl.BlockSpec((nsub * tq, WG), lambda s, j: (s * qpb + j, 0)),
        out_shape=jax.ShapeDtypeStruct((n_seq * seq_len, WG), F32),
        scratch_shapes=([pltpu.VMEM((tq, lk), F32)] * (4 if diff else 2)
                        + [pltpu.VMEM((tq, 128), F32)] * (4 if diff else 2)
                        + [pltpu.VMEM((N_HEADS, tq, WG), F32)]) * nsub,
        compiler_params=_cparams(("parallel", "parallel")),
        name="attn_diff" if diff else "attn_gqa",
    )(*args)


SSD_PAIR = 2 * C_CHUNK


def _ssd_intra(xbc, dt_raw, dtb, av, *, reverse):
    q = C_CHUNK
    xs = xbc[:, 0:WG]
    bm = xbc[:, WG:WG + 128].astype(BF16)
    cm = xbc[:, WG + 128:WG + 256]
    raw = dt_raw + dtb
    dt = jnp.maximum(raw, 0.0) + jnp.log1p(jnp.exp(-jnp.abs(raw)))
    dta = dt * av
    li = lax.broadcasted_iota(jnp.int32, (q, q), 0)
    si = lax.broadcasted_iota(jnp.int32, (q, q), 1)
    causal = (si >= li) if reverse else (si <= li)
    cum = jnp.dot(causal.astype(F32), dta, preferred_element_type=F32, precision=lax.Precision.HIGHEST)
    cum_t = cum.T
    end = 0 if reverse else q - 1
    d0 = N_HEADS if reverse else 0
    rowh = lax.broadcasted_iota(jnp.int32, (WG, 1), 0) // HEAD_DIM

    y = jnp.zeros((q, WG), F32)
    xw = jnp.zeros((q, WG), F32)
    ecum = jnp.zeros((q, WG), F32)
    cdec = jnp.zeros((WG, 1), F32)
    gmat = None
    for h in range(N_HEADS):
        j = d0 + h
        g = h // (N_HEADS // C_GROUPS)
        if h % (N_HEADS // C_GROUPS) == 0:
            cg = jnp.where(_lane_mask(128, g * C_STATE, (g + 1) * C_STATE), cm, 0.0).astype(BF16)
            gmat = lax.dot_general(cg, bm, (((1,), (1,)), ((), ())), preferred_element_type=F32)
        col = jnp.broadcast_to(cum[:, j:j + 1], (q, 128))
        dtc = jnp.broadcast_to(dt[:, j:j + 1], (q, 128))
        wide = lambda a: jnp.concatenate([a, a], axis=1)
        seg = col - cum_t[j:j + 1, :]
        decay = jnp.where(causal, jnp.exp(jnp.where(causal, seg, 0.0)), 0.0)
        hm = _lane_mask(WG, h * HEAD_DIM, (h + 1) * HEAD_DIM)
        xdt = jnp.where(hm, xs * wide(dtc), 0.0)
        y = y + jnp.dot((gmat * decay).astype(BF16), xdt.astype(BF16), preferred_element_type=F32)
        cend = cum[end:end + 1, j:j + 1]
        xw = xw + xdt * wide(jnp.exp(cend - col))
        ecum = jnp.where(hm, wide(jnp.exp(col)), ecum)
        cdec = jnp.where(rowh == h, jnp.exp(cend), cdec)

    st = lax.dot_general(xw.astype(BF16), bm, (((0,), (0,)), ((), ())), preferred_element_type=F32)
    return y, xs, cm.astype(BF16), ecum, cdec, st


def _ssd_direction(x_ref, xp_ref, xn_ref, dt_ref, h0_ref, cw_ref, cb_ref, dtb_ref, av_ref, s_ref,
                   y_ref, st_ref, skip, p, *, reverse, npc, ppc, ppl):
    q = C_CHUNK
    is_lat = p >= npc
    pos = jnp.where(is_lat, jnp.maximum(p - npc, 0) % ppl, p % ppc)
    nper = jnp.where(is_lat, ppl, ppc)
    enter = (pos == nper - 1) if reverse else (pos == 0)

    @pl.when(enter)
    def _():
        s_ref[...] = h0_ref[0, 0]

    x = x_ref[...]
    rid = lax.broadcasted_iota(jnp.int32, (SSD_PAIR, 1), 0)
    prow = jnp.where(pos > 0, xp_ref[SUBLANES - 1:SUBLANES, :], 0.0)
    nrow = jnp.where(pos < nper - 1, xn_ref[0:1, :], 0.0)
    xm1 = jnp.where(rid == 0, prow, pltpu.roll(x, 1, 0))
    xp1 = jnp.where(rid == SSD_PAIR - 1, nrow, pltpu.roll(x, SSD_PAIR - 1, 0))
    cw = cw_ref[0]
    xbc = _silu(xm1 * cw[0:1] + x * cw[1:2] + xp1 * cw[2:3] + cb_ref[0])

    parts = [_ssd_intra(xbc[k * q:(k + 1) * q], dt_ref[k * q:(k + 1) * q, :], dtb_ref[...], av_ref[...],
                        reverse=reverse) for k in range(2)]
    rowh = lax.broadcasted_iota(jnp.int32, (WG, 1), 0) // HEAD_DIM
    colg = lax.broadcasted_iota(jnp.int32, (1, 128), 1) // C_STATE
    blk = (rowh // (N_HEADS // C_GROUPS)) == colg
    s = s_ref[...]
    for k in ((1, 0) if reverse else (0, 1)):
        y, xs, cm, ecum, cdec, st = parts[k]
        s_msk = jnp.where(blk, s, 0.0).astype(BF16)
        y = y + lax.dot_general(cm, s_msk, (((1,), (1,)), ((), ())), preferred_element_type=F32) * ecum
        if skip is not None:
            y = y + skip * xs
        y_ref[k * q:(k + 1) * q, :] = y
        s = s * cdec + st
    s_ref[...] = s
    st_ref[0] = s


def _ssd_kernel(xf_ref, xpf_ref, xnf_ref, dtf_ref, h0f_ref, xb_ref, xpb_ref, xnb_ref, dtr_ref, h0b_ref,
                cw_ref, cb_ref, dtb_ref, av_ref, dsk_ref,
                yf_ref, yb_ref, stf_ref, stb_ref, sf_ref, sb_ref, *, npc, ppc, ppl):
    t = pl.program_id(0)
    kw = dict(npc=npc, ppc=ppc, ppl=ppl)
    _ssd_direction(xf_ref, xpf_ref, xnf_ref, dtf_ref, h0f_ref, cw_ref, cb_ref, dtb_ref.at[0, 0], av_ref.at[0, 0],
                   sf_ref, yf_ref, stf_ref, dsk_ref[0], t, reverse=False, **kw)
    _ssd_direction(xb_ref, xpb_ref, xnb_ref, dtr_ref, h0b_ref, cw_ref, cb_ref, dtb_ref.at[0, 1], av_ref.at[0, 1],
                   sb_ref, yb_ref, stb_ref, None, pl.num_programs(0) - 1 - t, reverse=True, **kw)


def _ssd(cx, cdt, h0, conv_w, conv_b, dtb, av, dsk, *, layer, npc, ppc, ppl, n_ctx_seq):
    n = cx.shape[0]
    nc = n // SSD_PAIR
    n_seq = h0.shape[1]
    sub = SSD_PAIR // SUBLANES
    nb8 = n // SUBLANES
    lay3 = lambda t: (layer, 0, 0)

    def specs(cidx, d):
        def seq_of(t):
            c = cidx(t)
            return jnp.where(c < npc, c // ppc, n_ctx_seq + jnp.maximum(c - npc, 0) // ppl)

        row = lambda t: (cidx(t), 0)
        ins = [pl.BlockSpec((SSD_PAIR, 2 * WG), row),
               pl.BlockSpec((SUBLANES, 2 * WG), lambda t: (jnp.maximum(cidx(t) * sub - 1, 0), 0)),
               pl.BlockSpec((SUBLANES, 2 * WG), lambda t: (jnp.minimum((cidx(t) + 1) * sub, nb8 - 1), 0)),
               pl.BlockSpec((SSD_PAIR, 128), row),
               pl.BlockSpec((1, 1, WG, 128), lambda t: (d, seq_of(t), 0, 0))]
        outs = [pl.BlockSpec((SSD_PAIR, WG), row), pl.BlockSpec((1, WG, 128), lambda t: (seq_of(t), 0, 0))]
        return ins, outs

    ins_f, outs_f = specs(lambda t: t, 0)
    ins_b, outs_b = specs(lambda t: nc - 1 - t, 1)
    return pl.pallas_call(
        functools.partial(_ssd_kernel, npc=npc, ppc=ppc, ppl=ppl),
        grid=(nc,),
        in_specs=ins_f + ins_b + [pl.BlockSpec((1, 3, 2 * WG), lay3),
                                  pl.BlockSpec((1, 1, 2 * WG), lay3),
                                  pl.BlockSpec((1, 2, 1, 128), lambda t: (layer, 0, 0, 0)),
                                  pl.BlockSpec((1, 2, 1, 128), lambda t: (layer, 0, 0, 0)),
                                  pl.BlockSpec((1, 1, WG), lay3)],
        out_specs=[outs_f[0], outs_b[0], outs_f[1], outs_b[1]],
        out_shape=[jax.ShapeDtypeStruct((n, WG), F32), jax.ShapeDtypeStruct((n, WG), F32),
                   jax.ShapeDtypeStruct((n_seq, WG, 128), F32), jax.ShapeDtypeStruct((n_seq, WG, 128), F32)],
        scratch_shapes=[pltpu.VMEM((WG, 128), F32), pltpu.VMEM((WG, 128), F32)],
        compiler_params=_cparams(("arbitrary",)),
        name="ssd",
    )(cx, cx, cx, cdt, h0, cx, cx, cx, cdt, h0, conv_w, conv_b, dtb, av, dsk)


def _s5_kernel(*refs, tt, nl, nb_in, fold):
    u_cur, u_next = refs[:nb_in], refs[nb_in:2 * nb_in]
    (wb_ref, are_ref, aim_ref, wc_ref, h0r_ref, h0i_ref, y_ref, sr_ref, si_ref,
     ust_ref, ytm_ref, bu_ref, xs_ref, cr_ref, ci_ref) = refs[2 * nb_in:]
    hl = nl // 2
    d = pl.program_id(0)
    j = pl.program_id(2)
    lane = lax.broadcasted_iota(jnp.int32, (1, WG), 1)
    gw = WG // fold

    def project(u_refs, slot):
        for qq in range(fold):
            for b in range(nb_in):
                c = qq * nb_in + b
                u = u_refs[b][...]
                if fold > 1:
                    u = jnp.where((lane >= qq * gw) & (lane < (qq + 1) * gw), u, 0.0)
                for k in range(WG // 128):
                    ust_ref[k, pl.ds(c, tt, stride=SUBLANES), :] = u[:, k * 128:(k + 1) * 128]
        ust = jnp.concatenate([ust_ref[k].astype(BF16) for k in range(WG // 128)], axis=1)
        bu_ref[slot] = jnp.dot(ust, wb_ref[0, 0], preferred_element_type=F32)

    @pl.when(j == 0)
    def _():
        cr_ref[...] = h0r_ref[0, 0]
        ci_ref[...] = h0i_ref[0, 0]
        project(u_cur, 0)

    bu_cur = bu_ref.at[j % 2]
    a_re = are_ref[0, 0]
    a_im = aim_ref[0, 0]

    def step(t, carry):
        xr, xi = carry
        te = t + d * (tt - 1 - 2 * t)
        r0 = pl.multiple_of(te * SUBLANES, SUBLANES)
        nr = a_re * xr - a_im * xi + bu_cur[pl.ds(r0, SUBLANES), 0:hl]
        ni = a_re * xi + a_im * xr + bu_cur[pl.ds(r0, SUBLANES), hl:nl]
        xs_ref[pl.ds(r0, SUBLANES), 0:hl] = nr
        xs_ref[pl.ds(r0, SUBLANES), hl:nl] = ni
        return nr, ni

    xr, xi = lax.fori_loop(0, tt, step, (cr_ref[...], ci_ref[...]), unroll=8)
    cr_ref[...] = xr
    ci_ref[...] = xi
    sr_ref[0, 0] = xr
    si_ref[0, 0] = xi
    y = jnp.dot(xs_ref[...].astype(BF16), wc_ref[0, 0], preferred_element_type=F32)
    for k in range(WG // 128):
        ytm_ref[k] = y[:, k * 128:(k + 1) * 128]

    def chain_rows(c):
        return jnp.concatenate([ytm_ref[k, pl.ds(c, tt, stride=SUBLANES), :] for k in range(WG // 128)], axis=1)

    for b in range(nb_in):
        if fold == 1:
            y_ref[0, b] = chain_rows(b)
        else:
            acc = jnp.zeros((tt, WG), F32)
            for qq in range(fold):
                acc = jnp.where((lane >= qq * gw) & (lane < (qq + 1) * gw), chain_rows(qq * nb_in + b), acc)
            y_ref[0, b] = acc
    project(u_next, (j + 1) % 2)


def _s5_scan(du, wb, a_re, a_im, wc, h0r, h0i, *, layer, n_b, seq_len, row_off, tt, fold):
    nb_in = SUBLANES // fold
    n_slab = n_b // nb_in
    nl = wb.shape[-1]
    hl = nl // 2
    nblk = seq_len // tt
    assert seq_len % tt == 0 and row_off % tt == 0
    tblk = lambda d, j: j + d * (nblk - 1 - 2 * j)
    u_specs = [pl.BlockSpec((tt, WG), functools.partial(
        lambda d, s, j, b, ahead: (row_off // tt + (s * nb_in + b) * nblk
                                   + tblk(d, jnp.minimum(j + ahead, nblk - 1)), 0), b=b, ahead=ahead))
               for ahead in (0, 1) for b in range(nb_in)]
    wmap = lambda d, s, j: (layer, d, 0, 0)
    smap = lambda d, s, j: (d, s, 0, 0)
    st_spec = pl.BlockSpec((1, 1, SUBLANES, hl), smap)
    rows = SUBLANES * tt
    return pl.pallas_call(
        functools.partial(_s5_kernel, tt=tt, nl=nl, nb_in=nb_in, fold=fold),
        grid=(2, n_slab, nblk),
        in_specs=u_specs + [pl.BlockSpec((1, 1, WG, nl), wmap),
                            pl.BlockSpec((1, 1, SUBLANES, hl), wmap),
                            pl.BlockSpec((1, 1, SUBLANES, hl), wmap),
                            pl.BlockSpec((1, 1, nl, WG), wmap),
                            st_spec, st_spec],
        out_specs=[pl.BlockSpec((1, nb_in, tt, WG), lambda d, s, j: (d, s, tblk(d, j), 0)), st_spec, st_spec],
        out_shape=[jax.ShapeDtypeStruct((2, n_b, seq_len, WG), F32),
                   jax.ShapeDtypeStruct((2, n_slab, SUBLANES, hl), F32),
                   jax.ShapeDtypeStruct((2, n_slab, SUBLANES, hl), F32)],
        scratch_shapes=[pltpu.VMEM((WG // 128, rows, 128), F32), pltpu.VMEM((WG // 128, rows, 128), F32),
                        pltpu.VMEM((2, rows, nl), F32), pltpu.VMEM((rows, nl), F32),
                        pltpu.VMEM((SUBLANES, hl), F32), pltpu.VMEM((SUBLANES, hl), F32)],
        compiler_params=_cparams(("parallel", "parallel", "arbitrary")),
        name="s5_scan",
    )(*([du] * (2 * nb_in)), wb, a_re, a_im, wc, h0r, h0i)


def _s5_params(lam_re, lam_im, log_step, b_ri, c_ri):
    lam = lax.complex(lam_re, lam_im)
    a_bar = jnp.exp(lam * jnp.exp(log_step)[..., None])
    b_bar = ((a_bar - 1.0) / lam)[..., None] * lax.complex(b_ri[..., 0], b_ri[..., 1])
    return jnp.real(a_bar), jnp.imag(a_bar), jnp.real(b_bar), jnp.imag(b_bar), c_ri[..., 0], c_ri[..., 1]


def _s5_weights(a_re, a_im, bb_re, bb_im, c_re, c_im, fold, chain_rep):
    lead = a_re.shape[:2]
    m = lead[0] * lead[1]
    gpr = D_GROUPS // fold
    eye = jnp.eye(gpr, dtype=F32)

    def wb_part(bb):
        t = bb.reshape(m, fold, gpr, D_STATE, D_GROUP)
        src = jnp.transpose(t, (0, 1, 2, 4, 3))
        w = jnp.where(eye[None, None, :, None, :, None] > 0,
                      jnp.broadcast_to(src[:, :, :, :, None, :], (m, fold, gpr, D_GROUP, gpr, D_STATE)), 0.0)
        return w.reshape(m, WG, gpr * D_STATE)

    wb = jnp.concatenate([wb_part(bb_re), wb_part(bb_im)], axis=-1)

    def wc_part(cc):
        t = cc.reshape(m, fold, gpr, D_GROUP, D_STATE)
        src = jnp.transpose(t, (0, 2, 4, 1, 3))
        w = jnp.where(eye[None, :, None, None, :, None] > 0,
                      jnp.broadcast_to(src[:, :, :, :, None, :], (m, gpr, D_STATE, fold, gpr, D_GROUP)), 0.0)
        return w.reshape(m, gpr * D_STATE, WG)

    wc = jnp.concatenate([wc_part(c_re), -wc_part(c_im)], axis=1)

    def a_rows(a):
        t = a.reshape(m, fold, 1, gpr * D_STATE)
        return jnp.broadcast_to(t, (m, fold, chain_rep, gpr * D_STATE)).reshape(lead + (SUBLANES, gpr * D_STATE))

    nl = 2 * gpr * D_STATE
    return (wb.astype(BF16).reshape(lead + (WG, nl)), wc.astype(BF16).reshape(lead + (nl, WG)),
            a_rows(a_re), a_rows(a_im))


def _outproj_kernel(*refs, nct, n_x):
    x_refs = refs[:n_x]
    (mod_ref, yac_ref, yal_ref, ybc_ref, ybl_ref, ycf_ref, ycb_ref, cz_ref, cn_ref,
     ydc0_ref, ydc1_ref, ydl0_ref, ydl1_ref, du_ref, dd_ref, wglu_ref, wout_ref, g_ref, o_ref) = refs[n_x:]
    mod = mod_ref[0, 0]
    yc = _rms((ycf_ref[...] + ycb_ref[...]) * _silu(cz_ref[...]), cn_ref[0])
    is_ctx = pl.program_id(0) < nct
    ya = jnp.where(is_ctx, yac_ref[...], yal_ref[...])
    yb = jnp.where(is_ctx, ybc_ref[...], ybl_ref[...])
    yd = jnp.where(is_ctx, ydc0_ref[0] + ydc1_ref[0], ydl0_ref[0] + ydl1_ref[0]) + dd_ref[0] * du_ref[...]
    yd = yd * (0.5 * (1.0 + jnp.tanh(math.sqrt(2.0 / math.pi) * (yd + 0.044715 * (yd * yd * yd)))))
    gl = jnp.dot(yd.astype(BF16), wglu_ref[0], preferred_element_type=F32)
    yd = gl[:, 0:WG] * _sigmoid(gl[:, WG:2 * WG])
    cat = jnp.concatenate([ya.astype(BF16), yb.astype(BF16), yc.astype(BF16), yd.astype(BF16)], axis=-1)
    y = jnp.dot(cat, wout_ref[0], preferred_element_type=F32)
    o_ref[...] = _load_x(x_refs, nct) + mod[2:3] * _rms(y, g_ref[0])


def _out_proj(x_all, mod_all, ya, yb, ycf, ycb, cz, cnorm, ydc, ydl, du, dd, wglu, wout, g_post1,
              *, layer, nct, tpl):
    x_specs, x_args = _x_specs(x_all, nct)
    n = sum(a.shape[0] for a in x_args)
    d = x_args[0].shape[1]
    row = lambda i: (i, 0)
    lay3 = lambda i: (layer, 0, 0)
    mrow = _mod_row(nct, tpl)
    sm = pl.BlockSpec((TM, WG), row)
    smc = pl.BlockSpec((TM, WG), lambda i: (jnp.minimum(i, nct - 1), 0))
    sml = pl.BlockSpec((TM, WG), lambda i: (jnp.maximum(i - nct, 0), 0))
    ydc_spec = lambda dd_: pl.BlockSpec((1, TM, WG), lambda i: (dd_, jnp.minimum(i, nct - 1), 0))
    ydl_spec = lambda dd_: pl.BlockSpec((1, TM, WG), lambda i: (dd_, jnp.maximum(i - nct, 0), 0))
    return pl.pallas_call(
        functools.partial(_outproj_kernel, nct=nct, n_x=len(x_args)),
        grid=(n // TM,),
        in_specs=x_specs + [pl.BlockSpec((1, 1, 6, d), lambda i: (layer, mrow(i), 0, 0)),
                  smc, sml, smc, sml, sm, sm, sm, pl.BlockSpec((1, 1, WG), lay3),
                  ydc_spec(0), ydc_spec(1), ydl_spec(0), ydl_spec(1), sm,
                  pl.BlockSpec((1, 1, WG), lay3), pl.BlockSpec((1,) + wglu.shape[1:], lay3),
                  pl.BlockSpec((1,) + wout.shape[1:], lay3), pl.BlockSpec((1, 1, d), lay3)],
        out_specs=pl.BlockSpec((TM, d), row),
        out_shape=jax.ShapeDtypeStruct((n, d), F32),
        compiler_params=_cparams(("parallel",)),
        name="out_proj",
    )(*x_args, mod_all, ya[0], ya[1], yb[0], yb[1], ycf, ycb, cz, cnorm, ydc, ydc, ydl, ydl, du, dd,
      wglu, wout, g_post1)


def _ffn_kernel(x_ref, xp_ref, xn_ref, mod_ref, g_ref, wup_ref, cw_ref, cb_ref, wdn_ref, gp_ref,
                *rest, f, fc, nct, tpc, tpl, split):
    if split:
        oc_ref, ol_ref, hext_ref, hbf_ref, act_ref = rest
    else:
        o_ref, hext_ref, hbf_ref, act_ref = rest
    i = pl.program_id(0)
    is_lat = i >= nct
    pos = jnp.where(is_lat, jnp.maximum(i - nct, 0) % tpl, i % tpc)
    nper = jnp.where(is_lat, tpl, tpc)
    mod = mod_ref[0, 0]

    def hfun(x):
        return _rms(x, g_ref[0]) * (1.0 + mod[4:5]) + mod[3:4]

    x = x_ref[...]
    hext_ref[0:SUBLANES, :] = jnp.where(pos > 0, hfun(xp_ref[...]), 0.0)
    hext_ref[SUBLANES:SUBLANES + TM, :] = hfun(x)
    hext_ref[SUBLANES + TM:2 * SUBLANES + TM, :] = jnp.where(pos < nper - 1, hfun(xn_ref[...]), 0.0)
    hbf_ref[...] = hext_ref[...].astype(BF16)
    rows = TM + 2 * SUBLANES

    def conv(u, c0):
        cw = cw_ref[0, :, c0:c0 + fc]
        um1 = pltpu.roll(u, 1, 0)[SUBLANES:SUBLANES + TM]
        up1 = pltpu.roll(u, rows - 1, 0)[SUBLANES:SUBLANES + TM]
        return (um1 * cw[0:1] + u[SUBLANES:SUBLANES + TM] * cw[1:2] + up1 * cw[2:3]
                + cb_ref[0, :, c0:c0 + fc])

    for j in range(f // fc):
        ug = jnp.dot(hbf_ref[...], wup_ref[0, :, j * fc:(j + 1) * fc], preferred_element_type=F32)
        uv = jnp.dot(hbf_ref[...], wup_ref[0, :, f + j * fc:f + (j + 1) * fc], preferred_element_type=F32)
        act_ref[:, j * fc:(j + 1) * fc] = (_silu(conv(ug, j * fc)) * conv(uv, f + j * fc)).astype(BF16)
    acc = jnp.dot(act_ref[...], wdn_ref[0], preferred_element_type=F32)
    out = x + mod[5:6] * _rms(acc, gp_ref[0])
    if split:
        @pl.when(i < nct)
        def _():
            oc_ref[...] = out

        @pl.when(i >= nct)
        def _():
            ol_ref[...] = out
    else:
        o_ref[...] = out


def _ffn(x_all, mod_all, g_pre2, wup, cw, cb, wdn, g_post2, *, layer, nct, tpc, tpl, split):
    n, d = x_all.shape
    f = wdn.shape[1]
    fc = 256
    assert f % fc == 0
    sub = TM // SUBLANES
    nb8 = n // SUBLANES
    row = lambda i: (i, 0)
    lay3 = lambda i: (layer, 0, 0)
    mrow = _mod_row(nct, tpl)
    single = dict(pipeline_mode=pl.Buffered(1))
    if split:
        out_specs = [pl.BlockSpec((TM, d), lambda i: (jnp.minimum(i, nct - 1), 0)),
                     pl.BlockSpec((TM, d), lambda i: (jnp.maximum(i - nct, 0), 0))]
        out_shape = [jax.ShapeDtypeStruct((nct * TM, d), F32), jax.ShapeDtypeStruct((n - nct * TM, d), F32)]
    else:
        out_specs = pl.BlockSpec((TM, d), row)
        out_shape = jax.ShapeDtypeStruct((n, d), F32)
    return pl.pallas_call(
        functools.partial(_ffn_kernel, f=f, fc=fc, nct=nct, tpc=tpc, tpl=tpl, split=split),
        grid=(n // TM,),
        in_specs=[pl.BlockSpec((TM, d), row),
                  pl.BlockSpec((SUBLANES, d), lambda i: (jnp.maximum(i * sub - 1, 0), 0)),
                  pl.BlockSpec((SUBLANES, d), lambda i: (jnp.minimum((i + 1) * sub, nb8 - 1), 0)),
                  pl.BlockSpec((1, 1, 6, d), lambda i: (layer, mrow(i), 0, 0)),
                  pl.BlockSpec((1, 1, d), lay3),
                  pl.BlockSpec((1,) + wup.shape[1:], lay3, **single),
                  pl.BlockSpec((1,) + cw.shape[1:], lay3),
                  pl.BlockSpec((1,) + cb.shape[1:], lay3),
                  pl.BlockSpec((1,) + wdn.shape[1:], lay3, **single),
                  pl.BlockSpec((1, 1, d), lay3)],
        out_specs=out_specs,
        out_shape=out_shape,
        scratch_shapes=[pltpu.VMEM((TM + 2 * SUBLANES, d), F32), pltpu.VMEM((TM + 2 * SUBLANES, d), BF16),
                        pltpu.VMEM((TM, f), BF16)],
        compiler_params=_cparams(("arbitrary",)),
        name="conv_ffn",
    )(x_all, x_all, x_all, mod_all, g_pre2, wup, cw, cb, wdn, g_post2)


def _ssd_expand(st):
    z = jnp.zeros_like(st)
    hpg = N_HEADS // C_GROUPS
    left = jnp.concatenate([st[..., :hpg, :, :], z[..., hpg:, :, :]], axis=-3)
    right = jnp.concatenate([z[..., :hpg, :, :], st[..., hpg:, :, :]], axis=-3)
    return jnp.concatenate([left, right], axis=-1).reshape(st.shape[:-3] + (WG, 2 * C_STATE))


def _ssd_extract(s):
    s = s.reshape(s.shape[:-2] + (N_HEADS, HEAD_DIM, C_GROUPS, C_STATE))
    hpg = N_HEADS // C_GROUPS
    return jnp.stack([s[..., h, :, h // hpg, :] for h in range(N_HEADS)], axis=-3)


def kernel(x_prompt, x_sample, cache_a_k, cache_a_v, cache_b_k, cache_b_v, state_ssd, state_s5,
           c, c_ctx, w_mod, b_mod, g_pre1, g_post1, g_pre2, g_post2, w_in, a_lam, a_subln,
           b_qnorm, b_knorm, c_conv_w, c_conv_b, c_dt_bias, c_a_log, c_d, c_norm,
           d_lam_re, d_lam_im, d_log_step, d_b, d_c, d_d, d_glu, w_out, w_up,
           ffn_conv_w, ffn_conv_b, w_down):
    b1, l1, d = x_prompt.shape
    b2, l2, _ = x_sample.shape
    depth = w_mod.shape[0]
    past = cache_a_k.shape[2]
    n1, n2 = b1 * l1, b2 * l2
    assert d == 4 * WG and l1 % TM == 0 and l2 % TM == 0 and n1 % l2 == 0
    assert b1 % SUBLANES == 0 and SUBLANES % b2 == 0 and 1 + b2 <= SUBLANES
    nct, tpc, tpl = n1 // TM, l1 // TM, l2 // TM
    fold = SUBLANES // b2

    x_all = (x_prompt.reshape(n1, d), x_sample.reshape(n2, d))
    cond8 = jnp.concatenate([c_ctx[None, :], c, jnp.zeros((SUBLANES - 1 - b2, d), F32)], axis=0)
    mod_all = _modulation(cond8, w_mod, b_mod).reshape(depth, SUBLANES, 6, d)

    vec = lambda t: t.reshape(depth, 1, t.shape[-1])
    tile_h = lambda g: jnp.tile(g, (1, N_HEADS)).reshape(depth, 1, WG)
    hd = HEAD_DIM
    wcol = lambda a, b: w_in[:, :, a:b]
    kv_dup = lambda o: [wcol(o, o + hd), wcol(o, o + hd), wcol(o + hd, o + 2 * hd), wcol(o + hd, o + 2 * hd)]
    w_in_p = jnp.concatenate([wcol(0, 1024)] + kv_dup(1024) + kv_dup(1152)
                             + [wcol(1280, 2048), wcol(2056, 2312), wcol(2048, 2056),
                                jnp.zeros((depth, d, 120), F32)], axis=2).astype(BF16)
    w_up16, w_dn16, w_out16, w_glu16 = (t.astype(BF16) for t in (w_up, w_down, w_out, d_glu))
    rt, ct = _rope_tables(l2)
    ones_blk = jnp.asarray(np.kron(np.eye(N_HEADS, dtype=np.float32),
                                   np.full((HEAD_DIM, HEAD_DIM), 1.0 / HEAD_DIM, np.float32))).astype(BF16)
    rep_kv = lambda t: jnp.repeat(t, 2, axis=-2).reshape(*t.shape[:-2], WG)
    cak = cache_a_k.reshape(b2, depth, past, WG).astype(BF16)
    cav = cache_a_v.reshape(b2, depth, past, WG).astype(BF16)
    cbk = rep_kv(cache_b_k).astype(BF16)
    cbv = rep_kv(cache_b_v).astype(BF16)
    qn, kn, subln = tile_h(b_qnorm), tile_h(b_knorm), tile_h(a_subln)
    h0_lat = jnp.transpose(_ssd_expand(state_ssd), (1, 2, 0, 3, 4))
    h0_ssd = jnp.concatenate([jnp.zeros((depth, 2, b1, WG, 2 * C_STATE), F32), h0_lat], axis=2)
    dir_lanes = lambda t: jnp.concatenate(
        [jnp.stack([t[:, 0], jnp.zeros_like(t[:, 0])], axis=1), jnp.stack([jnp.zeros_like(t[:, 1]), t[:, 1]], axis=1),
         jnp.zeros((depth, 2, 128 - 2 * N_HEADS), F32)], axis=-1)[:, :, None, :]
    dtb = dir_lanes(c_dt_bias)
    av = dir_lanes(-jnp.exp(c_a_log))
    dsk = jnp.repeat(c_d, HEAD_DIM, axis=-1).reshape(depth, 1, WG)
    pr = _s5_params(d_lam_re, d_lam_im, d_log_step, d_b, d_c)
    wb1, wc1, ar1, ai1 = _s5_weights(*pr, 1, SUBLANES)
    wb2, wc2, ar2, ai2 = _s5_weights(*pr, fold, b2)
    gpr = D_GROUPS // fold
    zero1 = jnp.zeros((2, b1 // SUBLANES, SUBLANES, S5_LANES // 2), F32)
    h0s = state_s5.reshape(b2, depth, 2, fold, gpr * D_STATE, 2)
    h0s = jnp.transpose(h0s, (1, 2, 3, 0, 4, 5)).reshape(depth, 2, 1, SUBLANES, gpr * D_STATE, 2)

    new_ak, new_av, new_bk, new_bv, new_ssd, new_s5 = [], [], [], [], [], []
    y_prompt = y_sample = None
    for l in range(depth):
        lam_init = 0.8 - 0.6 * math.exp(-0.3 * l)
        qa, ka, va, qb, kb, vb, ka32, va32, kb32, vb32, cz, cx, du, cdt = _in_proj(
            x_all, mod_all, vec(g_pre1), w_in_p, qn, kn, ones_blk, rt, ct, layer=l, nct=nct, tpl=tpl)

        attn_c = functools.partial(_attention, n_seq=b1, seq_len=l1, row_off=0, tq=TM, ck=1024, layer=l,
                                   nsub=2 if l1 == TM else 1)
        attn_l = functools.partial(_attention, n_seq=b2, seq_len=l2, row_off=n1, tq=TQ_LATENT, ck=1024, layer=l)
        diff_kw = dict(diff=True, lam=a_lam, subln=subln, lam_init=lam_init)
        ya = (attn_c(qa, ka, va, **diff_kw), attn_l(qa, ka, va, ctx=(cak, cav), **diff_kw))
        yb = (attn_c(qb, kb, vb, diff=False), attn_l(qb, kb, vb, diff=False, ctx=(cbk, cbv)))

        ycf, ycb, st_f, st_b = _ssd(cx, cdt, h0_ssd[l], c_conv_w, vec(c_conv_b), dtb, av, dsk, layer=l,
                                    npc=n1 // SSD_PAIR, ppc=l1 // SSD_PAIR, ppl=l2 // SSD_PAIR, n_ctx_seq=b1)
        new_ssd.append(jnp.stack([_ssd_extract(st_f[:b1]), _ssd_extract(st_b[:b1])], axis=1))

        ydc, s1r, s1i = _s5_scan(du, wb1, ar1, ai1, wc1, zero1, zero1, layer=l, n_b=b1, seq_len=l1,
                                 row_off=0, tt=min(l1, 128), fold=1)
        ydl, _, _ = _s5_scan(du, wb2, ar2, ai2, wc2, h0s[l, ..., 0], h0s[l, ..., 1], layer=l, n_b=b2,
                             seq_len=l2, row_off=n1, tt=min(l2, 512), fold=fold)
        st1 = jnp.stack([s1r, s1i], axis=-1).reshape(2, b1, D_GROUPS, D_STATE, 2)
        new_s5.append(jnp.transpose(st1, (1, 0, 2, 3, 4)))

        x_all = _out_proj(x_all, mod_all, ya, yb, ycf, ycb, cz, vec(c_norm), ydc.reshape(2, n1, WG),
                          ydl.reshape(2, n2, WG), du, vec(d_d), w_glu16, w_out16, vec(g_post1),
                          layer=l, nct=nct, tpl=tpl)
        last = l == depth - 1
        res = _ffn(x_all, mod_all, vec(g_pre2), w_up16, ffn_conv_w, vec(ffn_conv_b), w_dn16, vec(g_post2),
                   layer=l, nct=nct, tpc=tpc, tpl=tpl, split=last)
        if last:
            y_prompt, y_sample = res[0].reshape(b1, l1, d), res[1].reshape(b2, l2, d)
        else:
            x_all = res

        cache = lambda t, nh: jnp.transpose(
            t[:nct].reshape(b1, tpc, nh, HEAD_DIM, TM), (0, 1, 4, 2, 3)).reshape(b1, l1, nh, HEAD_DIM)
        new_ak.append(cache(ka32, N_HEADS))
        new_av.append(cache(va32, N_HEADS))
        new_bk.append(cache(kb32, N_HEADS // 2))
        new_bv.append(cache(vb32, N_HEADS // 2))

    st = lambda xs: jnp.stack(xs, axis=1)
    return (y_prompt, y_sample, st(new_ak), st(new_av), st(new_bk), st(new_bv), st(new_ssd), st(new_s5))
```

```python
import functools
import math

import numpy as np
import jax
import jax.numpy as jnp
from jax import lax
from jax.experimental import pallas as pl
from jax.experimental.pallas import tpu as pltpu

F32 = jnp.float32
BF16 = jnp.bfloat16

EPS = 1e-6
LOG2E = math.log2(math.e)
GRID_W = 64
ROPE_BASE = 10000.0
HEAD_DIM = 64
A_HALF = HEAD_DIM // 2
N_HEADS = 4
WG = N_HEADS * HEAD_DIM
C_GROUPS = 2
C_STATE = 64
C_CHUNK = 128
D_GROUP = 16
D_GROUPS = WG // D_GROUP
D_STATE = 64
S5_LANES = D_GROUPS * D_STATE * 2

TM = 256
TQ_LATENT = 512
SUBLANES = 8
VMEM_LIMIT = 56 * 1024 * 1024

_C_AQ, _C_AK, _C_AV, _C_BQ, _C_BK, _C_BV, _C_CZ, _C_CX, _C_DU, _C_DT, _C_END = (
    0, 256, 512, 768, 1024, 1280, 1536, 1792, 2304, 2560, 2688)


def _sigmoid(x):
    return 1.0 / (1.0 + jnp.exp(-x))


def _silu(x):
    return x * _sigmoid(x)


def _cparams(sem):
    return pltpu.CompilerParams(dimension_semantics=sem, vmem_limit_bytes=VMEM_LIMIT)


def _lane_mask(width, lo, hi):
    lane = lax.broadcasted_iota(jnp.int32, (1, width), 1)
    return (lane >= lo) & (lane < hi)


def _rms(x, g):
    return (x * lax.rsqrt(jnp.mean(x * x, axis=-1, keepdims=True) + EPS)) * g


def _mod_row(nct, tpl):
    return lambda i: jnp.where(i < nct, 0, 1 + jnp.maximum(i - nct, 0) // tpl)


def _mod_kernel(c_ref, w_ref, b_ref, o_ref):
    s = _silu(c_ref[...])
    o_ref[0] = jnp.dot(s.astype(BF16), w_ref[0].astype(BF16), preferred_element_type=F32) + b_ref[0]


def _modulation(cond8, w_mod, b_mod):
    depth, d, n = w_mod.shape
    tn = 1536
    return pl.pallas_call(
        _mod_kernel,
        grid=(depth, n // tn),
        in_specs=[pl.BlockSpec((SUBLANES, d), lambda l, j: (0, 0)),
                  pl.BlockSpec((1, d, tn), lambda l, j: (l, 0, j)),
                  pl.BlockSpec((1, 1, tn), lambda l, j: (l, 0, j))],
        out_specs=pl.BlockSpec((1, SUBLANES, tn), lambda l, j: (l, 0, j)),
        out_shape=jax.ShapeDtypeStruct((depth, SUBLANES, n), F32),
        compiler_params=_cparams(("parallel", "parallel")),
        name="modulation",
    )(cond8, w_mod, b_mod.reshape(depth, 1, n))


def _x_specs(x, nct):
    if isinstance(x, tuple):
        d = x[0].shape[1]
        return [pl.BlockSpec((TM, d), lambda i: (jnp.minimum(i, nct - 1), 0)),
                pl.BlockSpec((TM, d), lambda i: (jnp.maximum(i - nct, 0), 0))], list(x)
    return [pl.BlockSpec((TM, x.shape[1]), lambda i: (i, 0))], [x]


def _load_x(x_refs, nct):
    if len(x_refs) == 2:
        return jnp.where(pl.program_id(0) < nct, x_refs[0][...], x_refs[1][...])
    return x_refs[0][...]


def _inproj_kernel(*refs, nct, n_x):
    x_refs = refs[:n_x]
    (mod_ref, g_ref, w_ref, qn_ref, kn_ref, ones_ref, rt_ref, ct_ref,
     qa_ref, ka_ref, va_ref, qb_ref, kb_ref, vb_ref,
     ka32_ref, va32_ref, kb32_ref, vb32_ref, cz_ref, cx_ref, du_ref, dt_ref) = refs[n_x:]
    i = pl.program_id(0)
    mod = mod_ref[0, 0]
    h = _rms(_load_x(x_refs, nct), g_ref[0]) * (1.0 + mod[1:2]) + mod[0:1]
    p = jnp.dot(h.astype(BF16), w_ref[0], preferred_element_type=F32)
    va_ref[...] = p[:, _C_AV:_C_BQ].astype(BF16)
    vb_ref[...] = p[:, _C_BV:_C_CZ].astype(BF16)
    kv_rows = lambda t: jnp.concatenate([t[0:HEAD_DIM], t[2 * HEAD_DIM:3 * HEAD_DIM]], axis=0)
    va32_ref[0] = p[:, _C_AV:_C_BQ].T
    vb32_ref[0] = kv_rows(p[:, _C_BV:_C_CZ].T)
    cz_ref[...] = p[:, _C_CZ:_C_CX]
    cx_ref[...] = p[:, _C_CX:_C_DU]
    du_ref[...] = p[:, _C_DU:_C_DT]
    dt_ref[...] = p[:, _C_DT:_C_END]

    def headnorm(t, gain):
        sq = t * t
        hi = sq.astype(BF16)
        lo = (sq - hi.astype(F32)).astype(BF16)
        ms = (jnp.dot(hi, ones_ref[...], preferred_element_type=F32)
              + jnp.dot(lo, ones_ref[...], preferred_element_type=F32))
        return (t * lax.rsqrt(ms + EPS)) * gain

    aq = p[:, _C_AQ:_C_AK] * (A_HALF ** -0.5 * LOG2E)
    ak = p[:, _C_AK:_C_AV]
    bq = headnorm(p[:, _C_BQ:_C_BK], qn_ref[0]) * (HEAD_DIM ** -0.5 * LOG2E)
    bk = headnorm(p[:, _C_BK:_C_BV], kn_ref[0])
    ka32_ref[0] = ak.T
    kb32_ref[0] = kv_rows(bk.T)

    rows = TM // GRID_W
    is_lat = i >= nct
    rt = rt_ref[0]
    lane = lax.broadcasted_iota(jnp.int32, (1, WG), 1)
    row_lane_a = (lane % (A_HALF // 2)) < A_HALF // 4
    row_lane_b = (lane % (HEAD_DIM // 2)) < HEAD_DIM // 4

    def table(k, row_lane):
        ctk = jnp.where(is_lat, ct_ref[k], 1.0 if k % 3 == 0 else 0.0)
        parts = [jnp.where(row_lane, rt[k, r:r + 1, :], ctk) for r in range(rows)]
        return jnp.concatenate(parts, axis=0)

    def rope(t, base, row_lane, dist):
        return (t * table(base, row_lane)
                + pltpu.roll(t, WG - dist, 1) * table(base + 1, row_lane)
                + pltpu.roll(t, dist, 1) * table(base + 2, row_lane))

    qa_ref[...] = rope(aq, 0, row_lane_a, A_HALF // 2).astype(BF16)
    ka_ref[...] = rope(ak, 0, row_lane_a, A_HALF // 2).astype(BF16)
    qb_ref[...] = rope(bq, 3, row_lane_b, HEAD_DIM // 2).astype(BF16)
    kb_ref[...] = rope(bk, 3, row_lane_b, HEAD_DIM // 2).astype(BF16)


def _rope_tables(l2):
    rows = l2 // GRID_W
    lane = np.arange(WG)

    def one(block, n):
        freqs = (np.float32(ROPE_BASE) ** (-np.arange(n, dtype=np.float32) / np.float32(n))).astype(np.float32)
        p = lane % block
        idx = p % (block // 2)
        first = p < (block // 2)
        f = freqs[idx % n]

        def tabs(pos):
            ang = (pos[:, None].astype(np.float32) * f[None, :]).astype(np.float32)
            c, s = np.cos(ang).astype(np.float32), np.sin(ang).astype(np.float32)
            return [c, np.where(first[None, :], -s, 0.0).astype(np.float32),
                    np.where(first[None, :], 0.0, s).astype(np.float32)]

        return tabs(np.arange(rows)), tabs(np.arange(GRID_W))

    ra, ca = one(A_HALF, A_HALF // 4)
    rb, cb = one(HEAD_DIM, HEAD_DIM // 4)
    rt = np.stack(ra + rb, axis=0)
    ct = np.stack(ca + cb, axis=0)
    rpt = TM // GRID_W
    npt = l2 // TM
    rt = rt.reshape(6, npt, rpt, WG).transpose(1, 0, 2, 3)
    rt = np.concatenate([rt, np.zeros((npt, 6, SUBLANES - rpt, WG), np.float32)], axis=2)
    ident = np.zeros((1, 6, SUBLANES, WG), np.float32)
    ident[0, 0::3] = 1.0
    return jnp.asarray(np.concatenate([rt, ident], axis=0)), jnp.asarray(ct)


def _in_proj(x_all, mod_all, g_pre1, w_in_p, qn, kn, ones_blk, rt, ct, *, layer, nct, tpl):
    x_specs, x_args = _x_specs(x_all, nct)
    n = sum(a.shape[0] for a in x_args)
    d = x_args[0].shape[1]
    nt = n // TM
    row = lambda i: (i, 0)
    const2 = lambda i: (0, 0)
    lay3 = lambda i: (layer, 0, 0)
    mrow = _mod_row(nct, tpl)
    ctx_blk = lambda i: (jnp.minimum(i, nct), 0, 0)
    kvw = WG // 2
    out_specs = ([pl.BlockSpec((TM, WG), row)] * 6
                 + [pl.BlockSpec((1, w, TM), ctx_blk) for w in (WG, WG, kvw, kvw)]
                 + [pl.BlockSpec((TM, w), row) for w in (WG, 2 * WG, WG, 128)])
    out_shape = ([jax.ShapeDtypeStruct((n, WG), BF16)] * 6
                 + [jax.ShapeDtypeStruct((nct + 1, w, TM), F32) for w in (WG, WG, kvw, kvw)]
                 + [jax.ShapeDtypeStruct((n, w), F32) for w in (WG, 2 * WG, WG, 128)])
    return pl.pallas_call(
        functools.partial(_inproj_kernel, nct=nct, n_x=len(x_args)),
        grid=(nt,),
        in_specs=x_specs + [
                  pl.BlockSpec((1, 1, 6, d), lambda i: (layer, mrow(i), 0, 0)),
                  pl.BlockSpec((1, 1, d), lay3),
                  pl.BlockSpec((1,) + w_in_p.shape[1:], lay3),
                  pl.BlockSpec((1, 1, WG), lay3),
                  pl.BlockSpec((1, 1, WG), lay3),
                  pl.BlockSpec((WG, WG), const2),
                  pl.BlockSpec((1, 6, SUBLANES, WG),
                               lambda i: (jnp.where(i < nct, tpl, jnp.maximum(i - nct, 0) % tpl), 0, 0, 0)),
                  pl.BlockSpec(ct.shape, lambda i: (0, 0, 0))],
        out_specs=out_specs,
        out_shape=out_shape,
        compiler_params=_cparams(("arbitrary",)),
        name="in_proj",
    )(*x_args, mod_all, g_pre1, w_in_p, qn, kn, ones_blk, rt, ct)


def _attn_kernel(*refs, diff, tq, lk_new, lc, ck, lam_init):
    it = iter(refs)
    q_ref, k_ref, v_ref = next(it), next(it), next(it)
    kc_ref = vc_ref = None
    if lc:
        kc_ref, vc_ref = next(it).at[0, 0], next(it).at[0, 0]
    lam_ref = sub_ref = None
    if diff:
        lam_ref, sub_ref = next(it), next(it)
    o_ref = next(it)
    nsm = 2 if diff else 1
    nsub = q_ref.shape[0] // tq
    assert nsub == 1 or (tq == lk_new and not lc)
    width = HEAD_DIM // nsm
    subs = []
    for u in range(nsub):
        s_u = tuple(tuple(next(it) for _ in range(nsm)) for _ in range(2))
        m_u = tuple(tuple(next(it) for _ in range(nsm)) for _ in range(2))
        oall_u = next(it)
        blocks_u = [(k_ref, v_ref, u * lk_new + r0, min(ck, lk_new - r0), r0) for r0 in range(0, lk_new, ck)]
        if lc:
            blocks_u.append((kc_ref, vc_ref, 0, lc, lk_new))
        subs.append((q_ref[u * tq:(u + 1) * tq, :], blocks_u, s_u, m_u, oall_u))

    lane = lax.broadcasted_iota(jnp.int32, (1, WG), 1)
    if diff:
        lp = lam_ref[0]
        lam = (jnp.exp(jnp.sum(lp[0:1] * lp[1:2], axis=-1, keepdims=True))
               - jnp.exp(jnp.sum(lp[2:3] * lp[3:4], axis=-1, keepdims=True)) + lam_init)

    def lane_tiles(x):
        return [x[:, t * 128:(t + 1) * 128] for t in range(x.shape[1] // 128)]

    def step(h, slot, scores, values):
        for q, blocks, s_ref, m_ref, oall_ref in subs:
            step_one(q, blocks, s_ref, m_ref, oall_ref, h, slot, scores, values)

    def step_one(q, blocks, s_ref, m_ref, oall_ref, h, slot, scores, values):
        if scores:
            qms = [jnp.where((lane >= h * HEAD_DIM + e * width) & (lane < h * HEAD_DIM + (e + 1) * width),
                             q, jnp.zeros_like(q)) for e in range(nsm)]
            mrun = [jnp.full((tq, 128), -jnp.inf, F32) for _ in range(nsm)]
        if values:
            prev = s_ref[1 - slot]
            m = [jnp.max(m_ref[1 - slot][e][...], axis=-1, keepdims=True) for e in range(nsm)]
            lrun = [jnp.zeros((tq, 128), F32) for _ in range(nsm)]
            acc = jnp.zeros((tq, WG), F32)
        for kr, vr, r0, rows, c0 in blocks:
            for e in range(nsm):
                if scores:
                    s = lax.dot_general(qms[e], kr[r0:r0 + rows, :], (((1,), (1,)), ((), ())),
                                        preferred_element_type=F32)
                    s_ref[slot][e][:, c0:c0 + rows] = s
                    for t in lane_tiles(s):
                        mrun[e] = jnp.maximum(mrun[e], t)
                if values:
                    p = jnp.exp2(prev[e][:, c0:c0 + rows] - m[e])
                    for t in lane_tiles(p):
                        lrun[e] = lrun[e] + t
                    if diff:
                        prev[e][:, c0:c0 + rows] = p
                    else:
                        acc = acc + jnp.dot(p.astype(BF16), vr[r0:r0 + rows, :], preferred_element_type=F32)
        if scores:
            for e in range(nsm):
                m_ref[slot][e][...] = mrun[e]
        if values:
            inv = [1.0 / jnp.sum(lrun[e], axis=-1, keepdims=True) for e in range(nsm)]
            if diff:
                for _, vr, r0, rows, c0 in blocks:
                    p = prev[0][:, c0:c0 + rows] * inv[0] - prev[1][:, c0:c0 + rows] * (lam * inv[1])
                    acc = acc + jnp.dot(p.astype(BF16), vr[r0:r0 + rows, :], preferred_element_type=F32)
            else:
                acc = acc * inv[0]
            oall_ref[h - 1] = acc

    step(0, 0, True, False)

    def body(g, carry):
        step(2 * g + 1, 1, True, True)
        step(2 * g + 2, 0, True, True)
        return carry

    lax.fori_loop(0, (N_HEADS - 2) // 2, body, 0)
    step(N_HEADS - 1, 1, True, True)
    step(N_HEADS, 0, False, True)

    for u, (_, _, _, _, oall_ref) in enumerate(subs):
        out = jnp.zeros((tq, WG), F32)
        for h in range(N_HEADS):
            hm = _lane_mask(WG, h * HEAD_DIM, (h + 1) * HEAD_DIM)
            o = oall_ref[h]
            if diff:
                ms = jnp.sum(jnp.where(hm, o * o, 0.0), axis=-1, keepdims=True) * (1.0 / HEAD_DIM)
                o = ((o * lax.rsqrt(ms + EPS)) * sub_ref[0]) * (1.0 - lam_init)
            out = jnp.where(hm, o, out)
        o_ref[u * tq:(u + 1) * tq, :] = out.astype(BF16)


def _attention(q, k, v, *, diff, n_seq, seq_len, row_off, tq, ck, layer, ctx=None,
               lam=None, subln=None, lam_init=0.0, nsub=1):
    assert row_off % (nsub * seq_len) == 0 and seq_len % tq == 0 and n_seq % nsub == 0
    assert nsub == 1 or (tq == seq_len and ctx is None)
    qpb = seq_len // tq
    q_map = lambda s, j: (row_off // (nsub * tq) + s * qpb + j, 0)
    kv_map = lambda s, j: (row_off // (nsub * seq_len) + s, 0)
    in_specs = [pl.BlockSpec((nsub * tq, WG), q_map), pl.BlockSpec((nsub * seq_len, WG), kv_map),
                pl.BlockSpec((nsub * seq_len, WG), kv_map)]
    args = [q, k, v]
    lc = 0
    if ctx is not None:
        kc, vc = ctx
        lc = kc.shape[2]
        cmap = lambda s, j: (s, layer, 0, 0)
        in_specs += [pl.BlockSpec((1, 1, lc, WG), cmap), pl.BlockSpec((1, 1, lc, WG), cmap)]
        args += [kc, vc]
    if diff:
        lay3 = lambda s, j: (layer, 0, 0)
        in_specs += [pl.BlockSpec((1,) + lam.shape[1:], lay3), pl.BlockSpec((1, 1, WG), lay3)]
        args += [lam, subln]
    lk = seq_len + lc
    return pl.pallas_call(
        functools.partial(_attn_kernel, diff=diff, tq=tq, lk_new=seq_len, lc=lc, ck=min(ck, seq_len),
                          lam_init=lam_init),
        grid=(n_seq // nsub, qpb),
        in_specs=in_specs,
        out_specs=pl.BlockSpec((nsub * tq, WG), lambda s, j: (s * qpb + j, 0)),
        out_shape=jax.ShapeDtypeStruct((n_seq * seq_len, WG), BF16),
        scratch_shapes=([pltpu.VMEM((tq, lk), F32)] * (4 if diff else 2)
                        + [pltpu.VMEM((tq, 128), F32)] * (4 if diff else 2)
                        + [pltpu.VMEM((N_HEADS, tq, WG), F32)]) * nsub,
        compiler_params=_cparams(("parallel", "parallel")),
        name="attn_diff" if diff else "attn_gqa",
    )(*args)


SSD_PAIR = 2 * C_CHUNK


def _ssd_intra(xbc, dt_raw, dtb, av, *, reverse):
    q = C_CHUNK
    xs = xbc[:, 0:WG]
    bm = xbc[:, WG:WG + 128].astype(BF16)
    cm = xbc[:, WG + 128:WG + 256]
    raw = dt_raw + dtb
    dt = jnp.maximum(raw, 0.0) + jnp.log1p(jnp.exp(-jnp.abs(raw)))
    dta = dt * av
    li = lax.broadcasted_iota(jnp.int32, (q, q), 0)
    si = lax.broadcasted_iota(jnp.int32, (q, q), 1)
    causal = (si >= li) if reverse else (si <= li)
    cum = jnp.dot(causal.astype(F32), dta, preferred_element_type=F32, precision=lax.Precision.HIGHEST)
    cum_t = cum.T
    end = 0 if reverse else q - 1
    d0 = N_HEADS if reverse else 0
    rowh = lax.broadcasted_iota(jnp.int32, (WG, 1), 0) // HEAD_DIM

    y = jnp.zeros((q, WG), F32)
    xw = jnp.zeros((q, WG), F32)
    ecum = jnp.zeros((q, WG), F32)
    cdec = jnp.zeros((WG, 1), F32)
    gmat = None
    for h in range(N_HEADS):
        j = d0 + h
        g = h // (N_HEADS // C_GROUPS)
        if h % (N_HEADS // C_GROUPS) == 0:
            cg = jnp.where(_lane_mask(128, g * C_STATE, (g + 1) * C_STATE), cm, 0.0).astype(BF16)
            gmat = lax.dot_general(cg, bm, (((1,), (1,)), ((), ())), preferred_element_type=F32)
        col = jnp.broadcast_to(cum[:, j:j + 1], (q, 128))
        dtc = jnp.broadcast_to(dt[:, j:j + 1], (q, 128))
        wide = lambda a: jnp.concatenate([a, a], axis=1)
        seg = col - cum_t[j:j + 1, :]
        decay = jnp.where(causal, jnp.exp(jnp.where(causal, seg, 0.0)), 0.0)
        hm = _lane_mask(WG, h * HEAD_DIM, (h + 1) * HEAD_DIM)
        xdt = jnp.where(hm, xs * wide(dtc), 0.0)
        y = y + jnp.dot((gmat * decay).astype(BF16), xdt.astype(BF16), preferred_element_type=F32)
        cend = cum[end:end + 1, j:j + 1]
        xw = xw + xdt * wide(jnp.exp(cend - col))
        ecum = jnp.where(hm, wide(jnp.exp(col)), ecum)
        cdec = jnp.where(rowh == h, jnp.exp(cend), cdec)

    st = lax.dot_general(xw.astype(BF16), bm, (((0,), (0,)), ((), ())), preferred_element_type=F32)
    return y, xs, cm.astype(BF16), ecum, cdec, st


def _ssd_direction(x_ref, xp_ref, xn_ref, dt_ref, h0_ref, cw_ref, cb_ref, dtb_ref, av_ref, s_ref,
                   y_ref, st_ref, skip, p, *, reverse, npc, ppc, ppl):
    q = C_CHUNK
    is_lat = p >= npc
    pos = jnp.where(is_lat, jnp.maximum(p - npc, 0) % ppl, p % ppc)
    nper = jnp.where(is_lat, ppl, ppc)
    enter = (pos == nper - 1) if reverse else (pos == 0)

    @pl.when(enter)
    def _():
        s_ref[...] = h0_ref[0, 0]

    x = x_ref[...]
    rid = lax.broadcasted_iota(jnp.int32, (SSD_PAIR, 1), 0)
    prow = jnp.where(pos > 0, xp_ref[SUBLANES - 1:SUBLANES, :], 0.0)
    nrow = jnp.where(pos < nper - 1, xn_ref[0:1, :], 0.0)
    xm1 = jnp.where(rid == 0, prow, pltpu.roll(x, 1, 0))
    xp1 = jnp.where(rid == SSD_PAIR - 1, nrow, pltpu.roll(x, SSD_PAIR - 1, 0))
    cw = cw_ref[0]
    xbc = _silu(xm1 * cw[0:1] + x * cw[1:2] + xp1 * cw[2:3] + cb_ref[0])

    parts = [_ssd_intra(xbc[k * q:(k + 1) * q], dt_ref[k * q:(k + 1) * q, :], dtb_ref[...], av_ref[...],
                        reverse=reverse) for k in range(2)]
    rowh = lax.broadcasted_iota(jnp.int32, (WG, 1), 0) // HEAD_DIM
    colg = lax.broadcasted_iota(jnp.int32, (1, 128), 1) // C_STATE
    blk = (rowh // (N_HEADS // C_GROUPS)) == colg
    s = s_ref[...]
    for k in ((1, 0) if reverse else (0, 1)):
        y, xs, cm, ecum, cdec, st = parts[k]
        s_msk = jnp.where(blk, s, 0.0).astype(BF16)
        y = y + lax.dot_general(cm, s_msk, (((1,), (1,)), ((), ())), preferred_element_type=F32) * ecum
        if skip is not None:
            y = y + skip * xs
        y_ref[k * q:(k + 1) * q, :] = y
        s = s * cdec + st
    s_ref[...] = s
    st_ref[0] = s


def _ssd_kernel(xf_ref, xpf_ref, xnf_ref, dtf_ref, h0f_ref, xb_ref, xpb_ref, xnb_ref, dtr_ref, h0b_ref,
                cw_ref, cb_ref, dtb_ref, av_ref, dsk_ref,
                yf_ref, yb_ref, stf_ref, stb_ref, sf_ref, sb_ref, *, npc, ppc, ppl):
    t = pl.program_id(0)
    kw = dict(npc=npc, ppc=ppc, ppl=ppl)
    _ssd_direction(xf_ref, xpf_ref, xnf_ref, dtf_ref, h0f_ref, cw_ref, cb_ref, dtb_ref.at[0, 0], av_ref.at[0, 0],
                   sf_ref, yf_ref, stf_ref, dsk_ref[0], t, reverse=False, **kw)
    _ssd_direction(xb_ref, xpb_ref, xnb_ref, dtr_ref, h0b_ref, cw_ref, cb_ref, dtb_ref.at[0, 1], av_ref.at[0, 1],
                   sb_ref, yb_ref, stb_ref, None, pl.num_programs(0) - 1 - t, reverse=True, **kw)


def _ssd(cx, cdt, h0, conv_w, conv_b, dtb, av, dsk, *, layer, npc, ppc, ppl, n_ctx_seq):
    n = cx.shape[0]
    nc = n // SSD_PAIR
    n_seq = h0.shape[1]
    sub = SSD_PAIR // SUBLANES
    nb8 = n // SUBLANES
    lay3 = lambda t: (layer, 0, 0)

    def specs(cidx, d):
        def seq_of(t):
            c = cidx(t)
            return jnp.where(c < npc, c // ppc, n_ctx_seq + jnp.maximum(c - npc, 0) // ppl)

        row = lambda t: (cidx(t), 0)
        ins = [pl.BlockSpec((SSD_PAIR, 2 * WG), row),
               pl.BlockSpec((SUBLANES, 2 * WG), lambda t: (jnp.maximum(cidx(t) * sub - 1, 0), 0)),
               pl.BlockSpec((SUBLANES, 2 * WG), lambda t: (jnp.minimum((cidx(t) + 1) * sub, nb8 - 1), 0)),
               pl.BlockSpec((SSD_PAIR, 128), row),
               pl.BlockSpec((1, 1, WG, 128), lambda t: (d, seq_of(t), 0, 0))]
        outs = [pl.BlockSpec((SSD_PAIR, WG), row), pl.BlockSpec((1, WG, 128), lambda t: (seq_of(t), 0, 0))]
        return ins, outs

    ins_f, outs_f = specs(lambda t: t, 0)
    ins_b, outs_b = specs(lambda t: nc - 1 - t, 1)
    return pl.pallas_call(
        functools.partial(_ssd_kernel, npc=npc, ppc=ppc, ppl=ppl),
        grid=(nc,),
        in_specs=ins_f + ins_b + [pl.BlockSpec((1, 3, 2 * WG), lay3),
                                  pl.BlockSpec((1, 1, 2 * WG), lay3),
                                  pl.BlockSpec((1, 2, 1, 128), lambda t: (layer, 0, 0, 0)),
                                  pl.BlockSpec((1, 2, 1, 128), lambda t: (layer, 0, 0, 0)),
                                  pl.BlockSpec((1, 1, WG), lay3)],
        out_specs=[outs_f[0], outs_b[0], outs_f[1], outs_b[1]],
        out_shape=[jax.ShapeDtypeStruct((n, WG), F32), jax.ShapeDtypeStruct((n, WG), F32),
                   jax.ShapeDtypeStruct((n_seq, WG, 128), F32), jax.ShapeDtypeStruct((n_seq, WG, 128), F32)],
        scratch_shapes=[pltpu.VMEM((WG, 128), F32), pltpu.VMEM((WG, 128), F32)],
        compiler_params=_cparams(("arbitrary",)),
        name="ssd",
    )(cx, cx, cx, cdt, h0, cx, cx, cx, cdt, h0, conv_w, conv_b, dtb, av, dsk)


def _s5_kernel(*refs, tt, nl, nb_in, fold):
    u_cur, u_next = refs[:nb_in], refs[nb_in:2 * nb_in]
    (wb_ref, are_ref, aim_ref, wc_ref, h0r_ref, h0i_ref, y_ref, sr_ref, si_ref,
     ust_ref, ytm_ref, bu_ref, xs_ref, cr_ref, ci_ref) = refs[2 * nb_in:]
    hl = nl // 2
    d = pl.program_id(0)
    j = pl.program_id(2)
    lane = lax.broadcasted_iota(jnp.int32, (1, WG), 1)
    gw = WG // fold

    def project(u_refs, slot):
        for qq in range(fold):
            for b in range(nb_in):
                c = qq * nb_in + b
                u = u_refs[b][...]
                if fold > 1:
                    u = jnp.where((lane >= qq * gw) & (lane < (qq + 1) * gw), u, 0.0)
                for k in range(WG // 128):
                    ust_ref[k, pl.ds(c, tt, stride=SUBLANES), :] = u[:, k * 128:(k + 1) * 128]
        ust = jnp.concatenate([ust_ref[k].astype(BF16) for k in range(WG // 128)], axis=1)
        bu_ref[slot] = jnp.dot(ust, wb_ref[0, 0], preferred_element_type=F32)

    @pl.when(j == 0)
    def _():
        cr_ref[...] = h0r_ref[0, 0]
        ci_ref[...] = h0i_ref[0, 0]
        project(u_cur, 0)

    bu_cur = bu_ref.at[j % 2]
    a_re = are_ref[0, 0]
    a_im = aim_ref[0, 0]

    def step(t, carry):
        xr, xi = carry
        te = t + d * (tt - 1 - 2 * t)
        r0 = pl.multiple_of(te * SUBLANES, SUBLANES)
        nr = a_re * xr - a_im * xi + bu_cur[pl.ds(r0, SUBLANES), 0:hl]
        ni = a_re * xi + a_im * xr + bu_cur[pl.ds(r0, SUBLANES), hl:nl]
        xs_ref[pl.ds(r0, SUBLANES), 0:hl] = nr
        xs_ref[pl.ds(r0, SUBLANES), hl:nl] = ni
        return nr, ni

    xr, xi = lax.fori_loop(0, tt, step, (cr_ref[...], ci_ref[...]), unroll=8)
    cr_ref[...] = xr
    ci_ref[...] = xi
    sr_ref[0, 0] = xr
    si_ref[0, 0] = xi
    y = jnp.dot(xs_ref[...].astype(BF16), wc_ref[0, 0], preferred_element_type=F32)
    for k in range(WG // 128):
        ytm_ref[k] = y[:, k * 128:(k + 1) * 128]

    def chain_rows(c):
        return jnp.concatenate([ytm_ref[k, pl.ds(c, tt, stride=SUBLANES), :] for k in range(WG // 128)], axis=1)

    for b in range(nb_in):
        if fold == 1:
            y_ref[0, b] = chain_rows(b)
        else:
            acc = jnp.zeros((tt, WG), F32)
            for qq in range(fold):
                acc = jnp.where((lane >= qq * gw) & (lane < (qq + 1) * gw), chain_rows(qq * nb_in + b), acc)
            y_ref[0, b] = acc
    project(u_next, (j + 1) % 2)


def _s5_scan(du, wb, a_re, a_im, wc, h0r, h0i, *, layer, n_b, seq_len, row_off, tt, fold):
    nb_in = SUBLANES // fold
    n_slab = n_b // nb_in
    nl = wb.shape[-1]
    hl = nl // 2
    nblk = seq_len // tt
    assert seq_len % tt == 0 and row_off % tt == 0
    tblk = lambda d, j: j + d * (nblk - 1 - 2 * j)
    u_specs = [pl.BlockSpec((tt, WG), functools.partial(
        lambda d, s, j, b, ahead: (row_off // tt + (s * nb_in + b) * nblk
                                   + tblk(d, jnp.minimum(j + ahead, nblk - 1)), 0), b=b, ahead=ahead))
               for ahead in (0, 1) for b in range(nb_in)]
    wmap = lambda d, s, j: (layer, d, 0, 0)
    smap = lambda d, s, j: (d, s, 0, 0)
    st_spec = pl.BlockSpec((1, 1, SUBLANES, hl), smap)
    rows = SUBLANES * tt
    return pl.pallas_call(
        functools.partial(_s5_kernel, tt=tt, nl=nl, nb_in=nb_in, fold=fold),
        grid=(2, n_slab, nblk),
        in_specs=u_specs + [pl.BlockSpec((1, 1, WG, nl), wmap),
                            pl.BlockSpec((1, 1, SUBLANES, hl), wmap),
                            pl.BlockSpec((1, 1, SUBLANES, hl), wmap),
                            pl.BlockSpec((1, 1, nl, WG), wmap),
                            st_spec, st_spec],
        out_specs=[pl.BlockSpec((1, nb_in, tt, WG), lambda d, s, j: (d, s, tblk(d, j), 0)), st_spec, st_spec],
        out_shape=[jax.ShapeDtypeStruct((2, n_b, seq_len, WG), F32),
                   jax.ShapeDtypeStruct((2, n_slab, SUBLANES, hl), F32),
                   jax.ShapeDtypeStruct((2, n_slab, SUBLANES, hl), F32)],
        scratch_shapes=[pltpu.VMEM((WG // 128, rows, 128), F32), pltpu.VMEM((WG // 128, rows, 128), F32),
                        pltpu.VMEM((2, rows, nl), F32), pltpu.VMEM((rows, nl), F32),
                        pltpu.VMEM((SUBLANES, hl), F32), pltpu.VMEM((SUBLANES, hl), F32)],
        compiler_params=_cparams(("parallel", "parallel", "arbitrary")),
        name="s5_scan",
    )(*([du] * (2 * nb_in)), wb, a_re, a_im, wc, h0r, h0i)


def _s5_params(lam_re, lam_im, log_step, b_ri, c_ri):
    lam = lax.complex(lam_re, lam_im)
    a_bar = jnp.exp(lam * jnp.exp(log_step)[..., None])
    b_bar = ((a_bar - 1.0) / lam)[..., None] * lax.complex(b_ri[..., 0], b_ri[..., 1])
    return jnp.real(a_bar), jnp.imag(a_bar), jnp.real(b_bar), jnp.imag(b_bar), c_ri[..., 0], c_ri[..., 1]


def _s5_weights(a_re, a_im, bb_re, bb_im, c_re, c_im, fold, chain_rep):
    lead = a_re.shape[:2]
    m = lead[0] * lead[1]
    gpr = D_GROUPS // fold
    eye = jnp.eye(gpr, dtype=F32)

    def wb_part(bb):
        t = bb.reshape(m, fold, gpr, D_STATE, D_GROUP)
        src = jnp.transpose(t, (0, 1, 2, 4, 3))
        w = jnp.where(eye[None, None, :, None, :, None] > 0,
                      jnp.broadcast_to(src[:, :, :, :, None, :], (m, fold, gpr, D_GROUP, gpr, D_STATE)), 0.0)
        return w.reshape(m, WG, gpr * D_STATE)

    wb = jnp.concatenate([wb_part(bb_re), wb_part(bb_im)], axis=-1)

    def wc_part(cc):
        t = cc.reshape(m, fold, gpr, D_GROUP, D_STATE)
        src = jnp.transpose(t, (0, 2, 4, 1, 3))
        w = jnp.where(eye[None, :, None, None, :, None] > 0,
                      jnp.broadcast_to(src[:, :, :, :, None, :], (m, gpr, D_STATE, fold, gpr, D_GROUP)), 0.0)
        return w.reshape(m, gpr * D_STATE, WG)

    wc = jnp.concatenate([wc_part(c_re), -wc_part(c_im)], axis=1)

    def a_rows(a):
        t = a.reshape(m, fold, 1, gpr * D_STATE)
        return jnp.broadcast_to(t, (m, fold, chain_rep, gpr * D_STATE)).reshape(lead + (SUBLANES, gpr * D_STATE))

    nl = 2 * gpr * D_STATE
    return (wb.astype(BF16).reshape(lead + (WG, nl)), wc.astype(BF16).reshape(lead + (nl, WG)),
            a_rows(a_re), a_rows(a_im))


def _outproj_kernel(*refs, nct, n_x):
    x_refs = refs[:n_x]
    (mod_ref, yac_ref, yal_ref, ybc_ref, ybl_ref, ycf_ref, ycb_ref, cz_ref, cn_ref,
     ydc0_ref, ydc1_ref, ydl0_ref, ydl1_ref, du_ref, dd_ref, wglu_ref, wout_ref, g_ref, o_ref) = refs[n_x:]
    mod = mod_ref[0, 0]
    yc = _rms((ycf_ref[...] + ycb_ref[...]) * _silu(cz_ref[...]), cn_ref[0])
    is_ctx = pl.program_id(0) < nct
    ya = jnp.where(is_ctx, yac_ref[...], yal_ref[...])
    yb = jnp.where(is_ctx, ybc_ref[...], ybl_ref[...])
    yd = jnp.where(is_ctx, ydc0_ref[0] + ydc1_ref[0], ydl0_ref[0] + ydl1_ref[0]) + dd_ref[0] * du_ref[...]
    yd = yd * (0.5 * (1.0 + jnp.tanh(math.sqrt(2.0 / math.pi) * (yd + 0.044715 * (yd * yd * yd)))))
    gl = jnp.dot(yd.astype(BF16), wglu_ref[0], preferred_element_type=F32)
    yd = gl[:, 0:WG] * _sigmoid(gl[:, WG:2 * WG])
    cat = jnp.concatenate([ya.astype(BF16), yb.astype(BF16), yc.astype(BF16), yd.astype(BF16)], axis=-1)
    y = jnp.dot(cat, wout_ref[0], preferred_element_type=F32)
    o_ref[...] = _load_x(x_refs, nct) + mod[2:3] * _rms(y, g_ref[0])


def _out_proj(x_all, mod_all, ya, yb, ycf, ycb, cz, cnorm, ydc, ydl, du, dd, wglu, wout, g_post1,
              *, layer, nct, tpl):
    x_specs, x_args = _x_specs(x_all, nct)
    n = sum(a.shape[0] for a in x_args)
    d = x_args[0].shape[1]
    row = lambda i: (i, 0)
    lay3 = lambda i: (layer, 0, 0)
    mrow = _mod_row(nct, tpl)
    sm = pl.BlockSpec((TM, WG), row)
    smc = pl.BlockSpec((TM, WG), lambda i: (jnp.minimum(i, nct - 1), 0))
    sml = pl.BlockSpec((TM, WG), lambda i: (jnp.maximum(i - nct, 0), 0))
    ydc_spec = lambda dd_: pl.BlockSpec((1, TM, WG), lambda i: (dd_, jnp.minimum(i, nct - 1), 0))
    ydl_spec = lambda dd_: pl.BlockSpec((1, TM, WG), lambda i: (dd_, jnp.maximum(i - nct, 0), 0))
    return pl.pallas_call(
        functools.partial(_outproj_kernel, nct=nct, n_x=len(x_args)),
        grid=(n // TM,),
        in_specs=x_specs + [pl.BlockSpec((1, 1, 6, d), lambda i: (layer, mrow(i), 0, 0)),
                  smc, sml, smc, sml, sm, sm, sm, pl.BlockSpec((1, 1, WG), lay3),
                  ydc_spec(0), ydc_spec(1), ydl_spec(0), ydl_spec(1), sm,
                  pl.BlockSpec((1, 1, WG), lay3), pl.BlockSpec((1,) + wglu.shape[1:], lay3),
                  pl.BlockSpec((1,) + wout.shape[1:], lay3), pl.BlockSpec((1, 1, d), lay3)],
        out_specs=pl.BlockSpec((TM, d), row),
        out_shape=jax.ShapeDtypeStruct((n, d), F32),
        compiler_params=_cparams(("parallel",)),
        name="out_proj",
    )(*x_args, mod_all, ya[0], ya[1], yb[0], yb[1], ycf, ycb, cz, cnorm, ydc, ydc, ydl, ydl, du, dd,
      wglu, wout, g_post1)


def _ffn_kernel(x_ref, xp_ref, xn_ref, mod_ref, g_ref, wup_ref, cw_ref, cb_ref, wdn_ref, gp_ref,
                *rest, f, fc, nct, tpc, tpl, split):
    if split:
        oc_ref, ol_ref, hext_ref, hbf_ref, act_ref = rest
    else:
        o_ref, hext_ref, hbf_ref, act_ref = rest
    i = pl.program_id(0)
    is_lat = i >= nct
    pos = jnp.where(is_lat, jnp.maximum(i - nct, 0) % tpl, i % tpc)
    nper = jnp.where(is_lat, tpl, tpc)
    mod = mod_ref[0, 0]

    def hfun(x):
        return _rms(x, g_ref[0]) * (1.0 + mod[4:5]) + mod[3:4]

    x = x_ref[...]
    hext_ref[0:SUBLANES, :] = jnp.where(pos > 0, hfun(xp_ref[...]), 0.0)
    hext_ref[SUBLANES:SUBLANES + TM, :] = hfun(x)
    hext_ref[SUBLANES + TM:2 * SUBLANES + TM, :] = jnp.where(pos < nper - 1, hfun(xn_ref[...]), 0.0)
    hbf_ref[...] = hext_ref[...].astype(BF16)
    rows = TM + 2 * SUBLANES

    def conv(u, c0):
        cw = cw_ref[0, :, c0:c0 + fc]
        um1 = pltpu.roll(u, 1, 0)[SUBLANES:SUBLANES + TM]
        up1 = pltpu.roll(u, rows - 1, 0)[SUBLANES:SUBLANES + TM]
        return (um1 * cw[0:1] + u[SUBLANES:SUBLANES + TM] * cw[1:2] + up1 * cw[2:3]
                + cb_ref[0, :, c0:c0 + fc])

    for j in range(f // fc):
        ug = jnp.dot(hbf_ref[...], wup_ref[0, :, j * fc:(j + 1) * fc], preferred_element_type=F32)
        uv = jnp.dot(hbf_ref[...], wup_ref[0, :, f + j * fc:f + (j + 1) * fc], preferred_element_type=F32)
        act_ref[:, j * fc:(j + 1) * fc] = (_silu(conv(ug, j * fc)) * conv(uv, f + j * fc)).astype(BF16)
    acc = jnp.dot(act_ref[...], wdn_ref[0], preferred_element_type=F32)
    out = x + mod[5:6] * _rms(acc, gp_ref[0])
    if split:
        @pl.when(i < nct)
        def _():
            oc_ref[...] = out

        @pl.when(i >= nct)
        def _():
            ol_ref[...] = out
    else:
        o_ref[...] = out


def _ffn(x_all, mod_all, g_pre2, wup, cw, cb, wdn, g_post2, *, layer, nct, tpc, tpl, split):
    n, d = x_all.shape
    f = wdn.shape[1]
    fc = 256
    assert f % fc == 0
    sub = TM // SUBLANES
    nb8 = n // SUBLANES
    row = lambda i: (i, 0)
    lay3 = lambda i: (layer, 0, 0)
    mrow = _mod_row(nct, tpl)
    single = dict(pipeline_mode=pl.Buffered(1))
    if split:
        out_specs = [pl.BlockSpec((TM, d), lambda i: (jnp.minimum(i, nct - 1), 0)),
                     pl.BlockSpec((TM, d), lambda i: (jnp.maximum(i - nct, 0), 0))]
        out_shape = [jax.ShapeDtypeStruct((nct * TM, d), F32), jax.ShapeDtypeStruct((n - nct * TM, d), F32)]
    else:
        out_specs = pl.BlockSpec((TM, d), row)
        out_shape = jax.ShapeDtypeStruct((n, d), F32)
    return pl.pallas_call(
        functools.partial(_ffn_kernel, f=f, fc=fc, nct=nct, tpc=tpc, tpl=tpl, split=split),
        grid=(n // TM,),
        in_specs=[pl.BlockSpec((TM, d), row),
                  pl.BlockSpec((SUBLANES, d), lambda i: (jnp.maximum(i * sub - 1, 0), 0)),
                  pl.BlockSpec((SUBLANES, d), lambda i: (jnp.minimum((i + 1) * sub, nb8 - 1), 0)),
                  pl.BlockSpec((1, 1, 6, d), lambda i: (layer, mrow(i), 0, 0)),
                  pl.BlockSpec((1, 1, d), lay3),
                  pl.BlockSpec((1,) + wup.shape[1:], lay3, **single),
                  pl.BlockSpec((1,) + cw.shape[1:], lay3),
                  pl.BlockSpec((1,) + cb.shape[1:], lay3),
                  pl.BlockSpec((1,) + wdn.shape[1:], lay3, **single),
                  pl.BlockSpec((1, 1, d), lay3)],
        out_specs=out_specs,
        out_shape=out_shape,
        scratch_shapes=[pltpu.VMEM((TM + 2 * SUBLANES, d), F32), pltpu.VMEM((TM + 2 * SUBLANES, d), BF16),
                        pltpu.VMEM((TM, f), BF16)],
        compiler_params=_cparams(("arbitrary",)),
        name="conv_ffn",
    )(x_all, x_all, x_all, mod_all, g_pre2, wup, cw, cb, wdn, g_post2)


def _ssd_expand(st):
    z = jnp.zeros_like(st)
    hpg = N_HEADS // C_GROUPS
    left = jnp.concatenate([st[..., :hpg, :, :], z[..., hpg:, :, :]], axis=-3)
    right = jnp.concatenate([z[..., :hpg, :, :], st[..., hpg:, :, :]], axis=-3)
    return jnp.concatenate([left, right], axis=-1).reshape(st.shape[:-3] + (WG, 2 * C_STATE))


def _ssd_extract(s):
    s = s.reshape(s.shape[:-2] + (N_HEADS, HEAD_DIM, C_GROUPS, C_STATE))
    hpg = N_HEADS // C_GROUPS
    return jnp.stack([s[..., h, :, h // hpg, :] for h in range(N_HEADS)], axis=-3)


def kernel(x_prompt, x_sample, cache_a_k, cache_a_v, cache_b_k, cache_b_v, state_ssd, state_s5,
           c, c_ctx, w_mod, b_mod, g_pre1, g_post1, g_pre2, g_post2, w_in, a_lam, a_subln,
           b_qnorm, b_knorm, c_conv_w, c_conv_b, c_dt_bias, c_a_log, c_d, c_norm,
           d_lam_re, d_lam_im, d_log_step, d_b, d_c, d_d, d_glu, w_out, w_up,
           ffn_conv_w, ffn_conv_b, w_down):
    b1, l1, d = x_prompt.shape
    b2, l2, _ = x_sample.shape
    depth = w_mod.shape[0]
    past = cache_a_k.shape[2]
    n1, n2 = b1 * l1, b2 * l2
    assert d == 4 * WG and l1 % TM == 0 and l2 % TM == 0 and n1 % l2 == 0
    assert b1 % SUBLANES == 0 and SUBLANES % b2 == 0 and 1 + b2 <= SUBLANES
    nct, tpc, tpl = n1 // TM, l1 // TM, l2 // TM
    fold = SUBLANES // b2

    x_all = (x_prompt.reshape(n1, d), x_sample.reshape(n2, d))
    cond8 = jnp.concatenate([c_ctx[None, :], c, jnp.zeros((SUBLANES - 1 - b2, d), F32)], axis=0)
    mod_all = _modulation(cond8, w_mod, b_mod).reshape(depth, SUBLANES, 6, d)

    vec = lambda t: t.reshape(depth, 1, t.shape[-1])
    tile_h = lambda g: jnp.tile(g, (1, N_HEADS)).reshape(depth, 1, WG)
    hd = HEAD_DIM
    wcol = lambda a, b: w_in[:, :, a:b]
    kv_dup = lambda o: [wcol(o, o + hd), wcol(o, o + hd), wcol(o + hd, o + 2 * hd), wcol(o + hd, o + 2 * hd)]
    w_in_p = jnp.concatenate([wcol(0, 1024)] + kv_dup(1024) + kv_dup(1152)
                             + [wcol(1280, 2048), wcol(2056, 2312), wcol(2048, 2056),
                                jnp.zeros((depth, d, 120), F32)], axis=2).astype(BF16)
    w_up16, w_dn16, w_out16, w_glu16 = (t.astype(BF16) for t in (w_up, w_down, w_out, d_glu))
    rt, ct = _rope_tables(l2)
    ones_blk = jnp.asarray(np.kron(np.eye(N_HEADS, dtype=np.float32),
                                   np.full((HEAD_DIM, HEAD_DIM), 1.0 / HEAD_DIM, np.float32))).astype(BF16)
    rep_kv = lambda t: jnp.repeat(t, 2, axis=-2).reshape(*t.shape[:-2], WG)
    cak = cache_a_k.reshape(b2, depth, past, WG).astype(BF16)
    cav = cache_a_v.reshape(b2, depth, past, WG).astype(BF16)
    cbk = rep_kv(cache_b_k).astype(BF16)
    cbv = rep_kv(cache_b_v).astype(BF16)
    qn, kn, subln = tile_h(b_qnorm), tile_h(b_knorm), tile_h(a_subln)
    h0_lat = jnp.transpose(_ssd_expand(state_ssd), (1, 2, 0, 3, 4))
    h0_ssd = jnp.concatenate([jnp.zeros((depth, 2, b1, WG, 2 * C_STATE), F32), h0_lat], axis=2)
    dir_lanes = lambda t: jnp.concatenate(
        [jnp.stack([t[:, 0], jnp.zeros_like(t[:, 0])], axis=1), jnp.stack([jnp.zeros_like(t[:, 1]), t[:, 1]], axis=1),
         jnp.zeros((depth, 2, 128 - 2 * N_HEADS), F32)], axis=-1)[:, :, None, :]
    dtb = dir_lanes(c_dt_bias)
    av = dir_lanes(-jnp.exp(c_a_log))
    dsk = jnp.repeat(c_d, HEAD_DIM, axis=-1).reshape(depth, 1, WG)
    pr = _s5_params(d_lam_re, d_lam_im, d_log_step, d_b, d_c)
    wb1, wc1, ar1, ai1 = _s5_weights(*pr, 1, SUBLANES)
    wb2, wc2, ar2, ai2 = _s5_weights(*pr, fold, b2)
    gpr = D_GROUPS // fold
    zero1 = jnp.zeros((2, b1 // SUBLANES, SUBLANES, S5_LANES // 2), F32)
    h0s = state_s5.reshape(b2, depth, 2, fold, gpr * D_STATE, 2)
    h0s = jnp.transpose(h0s, (1, 2, 3, 0, 4, 5)).reshape(depth, 2, 1, SUBLANES, gpr * D_STATE, 2)

    new_ak, new_av, new_bk, new_bv, new_ssd, new_s5 = [], [], [], [], [], []
    y_prompt = y_sample = None
    for l in range(depth):
        lam_init = 0.8 - 0.6 * math.exp(-0.3 * l)
        qa, ka, va, qb, kb, vb, ka32, va32, kb32, vb32, cz, cx, du, cdt = _in_proj(
            x_all, mod_all, vec(g_pre1), w_in_p, qn, kn, ones_blk, rt, ct, layer=l, nct=nct, tpl=tpl)

        attn_c = functools.partial(_attention, n_seq=b1, seq_len=l1, row_off=0, tq=TM, ck=1024, layer=l,
                                   nsub=2 if l1 == TM else 1)
        attn_l = functools.partial(_attention, n_seq=b2, seq_len=l2, row_off=n1, tq=TQ_LATENT, ck=1024, layer=l)
        diff_kw = dict(diff=True, lam=a_lam, subln=subln, lam_init=lam_init)
        ya = (attn_c(qa, ka, va, **diff_kw), attn_l(qa, ka, va, ctx=(cak, cav), **diff_kw))
        yb = (attn_c(qb, kb, vb, diff=False), attn_l(qb, kb, vb, diff=False, ctx=(cbk, cbv)))

        ycf, ycb, st_f, st_b = _ssd(cx, cdt, h0_ssd[l], c_conv_w, vec(c_conv_b), dtb, av, dsk, layer=l,
                                    npc=n1 // SSD_PAIR, ppc=l1 // SSD_PAIR, ppl=l2 // SSD_PAIR, n_ctx_seq=b1)
        new_ssd.append(jnp.stack([_ssd_extract(st_f[:b1]), _ssd_extract(st_b[:b1])], axis=1))

        ydc, s1r, s1i = _s5_scan(du, wb1, ar1, ai1, wc1, zero1, zero1, layer=l, n_b=b1, seq_len=l1,
                                 row_off=0, tt=min(l1, 128), fold=1)
        ydl, _, _ = _s5_scan(du, wb2, ar2, ai2, wc2, h0s[l, ..., 0], h0s[l, ..., 1], layer=l, n_b=b2,
                             seq_len=l2, row_off=n1, tt=min(l2, 512), fold=fold)
        st1 = jnp.stack([s1r, s1i], axis=-1).reshape(2, b1, D_GROUPS, D_STATE, 2)
        new_s5.append(jnp.transpose(st1, (1, 0, 2, 3, 4)))

        x_all = _out_proj(x_all, mod_all, ya, yb, ycf, ycb, cz, vec(c_norm), ydc.reshape(2, n1, WG),
                          ydl.reshape(2, n2, WG), du, vec(d_d), w_glu16, w_out16, vec(g_post1),
                          layer=l, nct=nct, tpl=tpl)
        last = l == depth - 1
        res = _ffn(x_all, mod_all, vec(g_pre2), w_up16, ffn_conv_w, vec(ffn_conv_b), w_dn16, vec(g_post2),
                   layer=l, nct=nct, tpc=tpc, tpl=tpl, split=last)
        if last:
            y_prompt, y_sample = res[0].reshape(b1, l1, d), res[1].reshape(b2, l2, d)
        else:
            x_all = res

        cache = lambda t, nh: jnp.transpose(
            t[:nct].reshape(b1, tpc, nh, HEAD_DIM, TM), (0, 1, 4, 2, 3)).reshape(b1, l1, nh, HEAD_DIM)
        new_ak.append(cache(ka32, N_HEADS))
        new_av.append(cache(va32, N_HEADS))
        new_bk.append(cache(kb32, N_HEADS // 2))
        new_bv.append(cache(vb32, N_HEADS // 2))

    st = lambda xs: jnp.stack(xs, axis=1)
    return (y_prompt, y_sample, st(new_ak), st(new_av), st(new_bk), st(new_bv), st(new_ssd), st(new_s5))
```
